```python
import jax
import jax.numpy as jnp
from jax import lax
import numpy as np

D_MODEL = 2048
BATCH = 4
SEQ = 8192
DEPTH = 2

GRID_W = 64
CTX_LEN = 256
NORM_EPS = 1e-6
D_FF = 5632
N_BRANCH = 3
N_MOD = 9
POOL_WIDTH = 512
POOL_WINDOWS = (2, 4, 8, 16)
POOL_GROUP_DIM = POOL_WIDTH // len(POOL_WINDOWS)
RWKV_HEADS = 16
RWKV_HEAD_DIM = 64
RWKV_WIDTH = RWKV_HEADS * RWKV_HEAD_DIM
DECAY_LORA = 64
ICLR_LORA = 64
GATE_LORA = 128
RWKV_COLS = 3 * RWKV_WIDTH + 2 * DECAY_LORA + 2 * ICLR_LORA + GATE_LORA
RWKV_GN_EPS = 64e-5
GMLP_WIDTH = 512
GMLP_GROUPS = 4
GMLP_GROUP_DIM = GMLP_WIDTH // GMLP_GROUPS
GMLP_CHUNK = 128
LN_EPS = 1e-5
OFF_POOL = 0
OFF_RWKV = OFF_POOL + POOL_WIDTH
OFF_GMLP = OFF_RWKV + RWKV_COLS
OFF_GATE = OFF_GMLP + 2 * GMLP_WIDTH
IN_COLS = OFF_GATE + N_BRANCH * D_MODEL

kernel_name = 'hybrid_pool_rwkv7_gmlp_prefix_dit'


def rms_norm(x, g):
    x32 = x.astype(jnp.float32)
    y = x32 * lax.rsqrt(jnp.mean(x32 * x32, axis=-1, keepdims=True) + NORM_EPS)
    return (y * g.astype(jnp.float32)).astype(x.dtype)


def adaln(cond, ada_w, ada_b):
    m = jax.nn.silu(cond) @ ada_w + ada_b
    return m.reshape(m.shape[:-1] + (N_MOD, D_MODEL))


def swiglu(h, wg, wu, wd):
    return (jax.nn.silu(h @ wg) * (h @ wu)) @ wd


def ffn_sublayer(h, mod, base, g, wg, wu, wd):
    hn = rms_norm(h, g) * (1.0 + mod[:, :, base + 1]) + mod[:, :, base]
    return h + 0.5 * mod[:, :, base + 2] * swiglu(hn, wg, wu, wd)


def shift_seq(z):
    h = z.shape[-1] // 2
    prev = jnp.pad(z[:, :-1, :h], ((0, 0), (1, 0), (0, 0)))
    nxt = jnp.pad(z[:, 1:, h:], ((0, 0), (0, 1), (0, 0)))
    return jnp.concatenate([prev, nxt], axis=-1)


def shift_grid(z, rows):
    b, l, c = z.shape
    q = c // 4
    g = z.reshape(b, rows, GRID_W, c)
    left = jnp.pad(g[:, :, :-1, :q], ((0, 0), (0, 0), (1, 0), (0, 0)))
    right = jnp.pad(g[:, :, 1:, q:2 * q], ((0, 0), (0, 0), (0, 1), (0, 0)))
    up = jnp.pad(g[:, :-1, :, 2 * q:3 * q], ((0, 0), (1, 0), (0, 0), (0, 0)))
    down = jnp.pad(g[:, 1:, :, 3 * q:], ((0, 0), (0, 1), (0, 0), (0, 0)))
    return jnp.concatenate([left, right, up, down], axis=-1).reshape(b, l, c)


def pool_mixer(za, pool_w, pool_scale):
    b, l, c = za.shape
    x32 = za.astype(jnp.float32)
    cs = jnp.concatenate([jnp.zeros((b, 1, c), jnp.float32), jnp.cumsum(x32, axis=1)], axis=1)
    t = jnp.arange(l)
    outs = []
    for gi, w in enumerate(POOL_WINDOWS):
        sl = slice(gi * POOL_GROUP_DIM, (gi + 1) * POOL_GROUP_DIM)
        lo = jnp.clip(t - w // 2, 0, l)
        hi = jnp.clip(t - w // 2 + w, 0, l)
        csg = cs[..., sl]
        s = jnp.take(csg, hi, axis=1) - jnp.take(csg, lo, axis=1)
        cnt = (hi - lo).astype(jnp.float32)[None, :, None]
        outs.append(s / cnt - x32[..., sl])
    p = jnp.stack(outs, axis=2).astype(za.dtype)
    y = jnp.einsum('blgc,gcd->blgd', p, pool_w).reshape(b, l, c)
    return y * pool_scale


def gmlp_mixer(zu, zv, ln_g, ln_b, ws, bs):
    u = jax.nn.gelu(zu)
    v = jax.nn.gelu(zv)
    v32 = v.astype(jnp.float32)
    mu = jnp.mean(v32, axis=-1, keepdims=True)
    var = jnp.mean(jnp.square(v32 - mu), axis=-1, keepdims=True)
    v = ((v32 - mu) * lax.rsqrt(var + LN_EPS)).astype(zv.dtype) * ln_g + ln_b
    b, l, c = v.shape
    vr = v.reshape(b, l // GMLP_CHUNK, GMLP_CHUNK, GMLP_GROUPS, GMLP_GROUP_DIM)
    s = jnp.einsum('gts,bnsgc->bntgc', ws, vr) + bs.T[None, None, :, :, None]
    return u * s.reshape(b, l, c)


def rwkv_time_mix(zb, shifted, init_state, mu, w0, w_up, a0, a_up, g_up, k_k, k_a, r_k,
                  ln_g, ln_b, with_output):
    f32 = jnp.float32
    dt = zb.dtype
    b, l, _ = zb.shape
    H, N, W = RWKV_HEADS, RWKV_HEAD_DIM, RWKV_WIDTH
    zs = (zb + (shifted - zb) * mu).astype(f32)
    r = zs[..., :W]
    k = zs[..., W:2 * W]
    v = zs[..., 2 * W:3 * W]
    o = 3 * W
    wd = zs[..., o:o + 2 * DECAY_LORA].reshape(b, l, 2, DECAY_LORA)
    o += 2 * DECAY_LORA
    ad = zs[..., o:o + 2 * ICLR_LORA].reshape(b, l, 2, ICLR_LORA)
    o += 2 * ICLR_LORA
    gd = zs[..., o:o + GATE_LORA]
    w_pre = w0.astype(f32) + jnp.einsum('bldr,drc->bldc', jnp.tanh(wd), w_up.astype(f32))
    decay = jnp.exp(-jnp.exp(-jax.nn.softplus(-w_pre) - 0.5))
    a = jax.nn.sigmoid(a0.astype(f32) + jnp.einsum('bldr,drc->bldc', ad, a_up.astype(f32)))

    def heads(t):
        return t.reshape(t.shape[:-1] + (H, N))

    kk = heads(k * k_k.astype(f32))
    kk = kk / jnp.maximum(jnp.sqrt(jnp.sum(kk * kk, axis=-1, keepdims=True)), 1e-12)
    k_dir = k[:, :, None, :] * (1.0 + (a - 1.0) * k_a.astype(f32))

    def both(t):
        return jnp.stack([t, jnp.flip(t, 1)], 0).transpose(2, 0, 1, 3, 4)

    def per_dir(t):
        t = heads(t)
        return jnp.stack([t[:, :, 0], jnp.flip(t[:, :, 1], 1)], 0).transpose(2, 0, 1, 3, 4)

    rh = heads(r)
    vh = heads(v)
    xs = (per_dir(decay), per_dir(k_dir), both(vh), both(kk), per_dir(a))
    if with_output:
        xs = xs + (both(rh),)

    def step(S, inp):
        w_t, k_t, v_t, kk_t, a_t = inp[:5]
        sa = jnp.einsum('dbhvk,dbhk->dbhv', S, -kk_t)
        S = (S * w_t[..., None, :] + sa[..., None] * (kk_t * a_t)[..., None, :]
             + v_t[..., None] * k_t[..., None, :])
        if with_output:
            return S, jnp.einsum('dbhvk,dbhk->dbhv', S, inp[5])
        return S, None

    s_fin, ys = lax.scan(step, init_state, xs)
    if not with_output:
        return None, s_fin
    y = (ys[:, 0] + jnp.flip(ys[:, 1], 0)).transpose(1, 0, 2, 3)
    m = jnp.mean(y, axis=-1, keepdims=True)
    var = jnp.mean(jnp.square(y - m), axis=-1, keepdims=True)
    y = ((y - m) * lax.rsqrt(var + RWKV_GN_EPS)).reshape(b, l, W)
    y = y * ln_g.astype(f32) + ln_b.astype(f32)
    bonus = jnp.sum(rh[:, :, None] * heads(k_dir) * r_k.astype(f32), axis=(2, -1))
    y = y + (bonus[..., None] * vh).reshape(b, l, W)
    g = jax.nn.sigmoid(gd) @ g_up.astype(f32)
    return (y * g).astype(dt), s_fin


def rwkv_from_lp(zb, shifted, init_state, lp, with_output):
    return rwkv_time_mix(zb, shifted, init_state, lp['rwkv_mu'], lp['rwkv_w0'], lp['rwkv_w_up'],
                         lp['rwkv_a0'], lp['rwkv_a_up'], lp['rwkv_g_up'], lp['rwkv_k_k'],
                         lp['rwkv_k_a'], lp['rwkv_r_k'], lp['rwkv_ln_g'], lp['rwkv_ln_b'],
                         with_output)


def token_mixer(hn, rows, init_state, lp):
    z = hn @ lp['w_in']
    za = z[..., OFF_POOL:OFF_RWKV]
    zb = z[..., OFF_RWKV:OFF_GMLP]
    zu = z[..., OFF_GMLP:OFF_GMLP + GMLP_WIDTH]
    zv = z[..., OFF_GMLP + GMLP_WIDTH:OFF_GATE]
    gates = jax.nn.sigmoid(z[..., OFF_GATE:].reshape(z.shape[:-1] + (N_BRANCH, D_MODEL)))
    y_pool = pool_mixer(za, lp['pool_w'], lp['pool_scale'])
    shifted = shift_seq(zb) if rows is None else shift_grid(zb, rows)
    y_rwkv, s_fin = rwkv_from_lp(zb, shifted, init_state, lp, True)
    y_gmlp = gmlp_mixer(zu, zv, lp['gmlp_ln_g'], lp['gmlp_ln_b'], lp['gmlp_ws'], lp['gmlp_bs'])
    merged = (gates[..., 0, :] * (y_pool @ lp['proj_pool'])
              + gates[..., 1, :] * (y_rwkv @ lp['proj_rwkv'])
              + gates[..., 2, :] * (y_gmlp @ lp['proj_gmlp']))
    return merged @ lp['w_out'], s_fin


def setup_inputs(seed: int = 0) -> dict:
    key = jax.random.key(seed)
    ks = jax.random.split(key, 40)
    f32 = jnp.float32
    D, F = D_MODEL, D_FF

    def nrm(k, shape, scale):
        return jax.random.normal(k, shape, f32) * scale

    return {
        'x': nrm(ks[0], (BATCH, SEQ, D), 1.0),
        'c': nrm(ks[1], (BATCH, D), 1.0),
        'ctx': nrm(ks[2], (BATCH, CTX_LEN, D), 1.0),
        'c_ctx': nrm(ks[3], (D,), 1.0),
        'ada_w': nrm(ks[4], (DEPTH, D, N_MOD * D), 0.5 * D ** -0.5),
        'ada_b': nrm(ks[5], (DEPTH, N_MOD * D), 0.02),
        'norm_g': 1.0 + nrm(ks[6], (DEPTH, 3, D), 0.02),
        'ffn_w_gate': nrm(ks[7], (DEPTH, 2, D, F), D ** -0.5),
        'ffn_w_up': nrm(ks[8], (DEPTH, 2, D, F), D ** -0.5),
        'ffn_w_down': nrm(ks[9], (DEPTH, 2, F, D), F ** -0.5),
        'w_in': nrm(ks[10], (DEPTH, D, IN_COLS), D ** -0.5),
        'pool_w': nrm(ks[11], (DEPTH, len(POOL_WINDOWS), POOL_GROUP_DIM, POOL_GROUP_DIM), POOL_GROUP_DIM ** -0.5),
        'pool_scale': 1.0 + nrm(ks[12], (DEPTH, POOL_WIDTH), 0.02),
        'rwkv_mu': 0.5 + nrm(ks[13], (DEPTH, RWKV_COLS), 0.1),
        'rwkv_w0': nrm(ks[14], (DEPTH, 2, RWKV_WIDTH), 0.5),
        'rwkv_w_up': nrm(ks[15], (DEPTH, 2, DECAY_LORA, RWKV_WIDTH), DECAY_LORA ** -0.5),
        'rwkv_a0': nrm(ks[16], (DEPTH, 2, RWKV_WIDTH), 0.5),
        'rwkv_a_up': nrm(ks[17], (DEPTH, 2, ICLR_LORA, RWKV_WIDTH), 0.5 * ICLR_LORA ** -0.5),
        'rwkv_g_up': nrm(ks[18], (DEPTH, GATE_LORA, RWKV_WIDTH), GATE_LORA ** -0.5),
        'rwkv_k_k': 1.0 + nrm(ks[19], (DEPTH, RWKV_WIDTH), 0.1),
        'rwkv_k_a': 1.0 + nrm(ks[20], (DEPTH, RWKV_WIDTH), 0.1),
        'rwkv_r_k': nrm(ks[21], (DEPTH, RWKV_HEADS, RWKV_HEAD_DIM), 0.1),
        'rwkv_ln_g': 1.0 + nrm(ks[22], (DEPTH, RWKV_WIDTH), 0.02),
        'rwkv_ln_b': nrm(ks[23], (DEPTH, RWKV_WIDTH), 0.02),
        'gmlp_ln_g': 1.0 + nrm(ks[24], (DEPTH, GMLP_WIDTH), 0.02),
        'gmlp_ln_b': nrm(ks[25], (DEPTH, GMLP_WIDTH), 0.02),
        'gmlp_ws': nrm(ks[26], (DEPTH, GMLP_GROUPS, GMLP_CHUNK, GMLP_CHUNK), GMLP_CHUNK ** -0.5),
        'gmlp_bs': 1.0 + nrm(ks[27], (DEPTH, GMLP_GROUPS, GMLP_CHUNK), 0.02),
        'proj_pool': nrm(ks[28], (DEPTH, POOL_WIDTH, D), POOL_WIDTH ** -0.5),
        'proj_rwkv': nrm(ks[29], (DEPTH, RWKV_WIDTH, D), RWKV_WIDTH ** -0.5),
        'proj_gmlp': nrm(ks[30], (DEPTH, GMLP_WIDTH, D), GMLP_WIDTH ** -0.5),
        'w_out': nrm(ks[31], (DEPTH, D, D), D ** -0.5),
        'final_norm': 1.0 + nrm(ks[32], (D,), 0.02),
    }


def reference(x, c, ctx, c_ctx, ada_w, ada_b, norm_g, ffn_w_gate, ffn_w_up, ffn_w_down, w_in,
              pool_w, pool_scale, rwkv_mu, rwkv_w0, rwkv_w_up, rwkv_a0, rwkv_a_up, rwkv_g_up,
              rwkv_k_k, rwkv_k_a, rwkv_r_k, rwkv_ln_g, rwkv_ln_b, gmlp_ln_g, gmlp_ln_b, gmlp_ws,
              gmlp_bs, proj_pool, proj_rwkv, proj_gmlp, w_out, final_norm):
    b, l, _ = x.shape
    rows = l // GRID_W
    hc = ctx
    zero_state = jnp.zeros((2, ctx.shape[0], RWKV_HEADS, RWKV_HEAD_DIM, RWKV_HEAD_DIM), jnp.float32)
    for li in range(DEPTH):
        last = li == DEPTH - 1
        lp = {
            'w_in': w_in[li], 'pool_w': pool_w[li], 'pool_scale': pool_scale[li],
            'rwkv_mu': rwkv_mu[li], 'rwkv_w0': rwkv_w0[li], 'rwkv_w_up': rwkv_w_up[li],
            'rwkv_a0': rwkv_a0[li], 'rwkv_a_up': rwkv_a_up[li], 'rwkv_g_up': rwkv_g_up[li],
            'rwkv_k_k': rwkv_k_k[li], 'rwkv_k_a': rwkv_k_a[li], 'rwkv_r_k': rwkv_r_k[li],
            'rwkv_ln_g': rwkv_ln_g[li], 'rwkv_ln_b': rwkv_ln_b[li],
            'gmlp_ln_g': gmlp_ln_g[li], 'gmlp_ln_b': gmlp_ln_b[li],
            'gmlp_ws': gmlp_ws[li], 'gmlp_bs': gmlp_bs[li],
            'proj_pool': proj_pool[li], 'proj_rwkv': proj_rwkv[li], 'proj_gmlp': proj_gmlp[li],
            'w_out': w_out[li],
        }
        mod_c = adaln(c_ctx, ada_w[li], ada_b[li])[None, None]
        mod_x = adaln(c, ada_w[li], ada_b[li])[:, None]

        hc = ffn_sublayer(hc, mod_c, 0, norm_g[li, 0], ffn_w_gate[li, 0], ffn_w_up[li, 0], ffn_w_down[li, 0])
        hnc = rms_norm(hc, norm_g[li, 1]) * (1.0 + mod_c[:, :, 4]) + mod_c[:, :, 3]
        if last:
            zb_c = hnc @ w_in[li][:, OFF_RWKV:OFF_GMLP]
            _, s_ctx = rwkv_from_lp(zb_c, shift_seq(zb_c), zero_state, lp, False)
        else:
            m_c, s_ctx = token_mixer(hnc, None, zero_state, lp)
            hc = hc + mod_c[:, :, 5] * m_c
            hc = ffn_sublayer(hc, mod_c, 6, norm_g[li, 2], ffn_w_gate[li, 1], ffn_w_up[li, 1], ffn_w_down[li, 1])

        x = ffn_sublayer(x, mod_x, 0, norm_g[li, 0], ffn_w_gate[li, 0], ffn_w_up[li, 0], ffn_w_down[li, 0])
        hn = rms_norm(x, norm_g[li, 1]) * (1.0 + mod_x[:, :, 4]) + mod_x[:, :, 3]
        m_x, _ = token_mixer(hn, rows, s_ctx, lp)
        x = x + mod_x[:, :, 5] * m_x
        x = ffn_sublayer(x, mod_x, 6, norm_g[li, 2], ffn_w_gate[li, 1], ffn_w_up[li, 1], ffn_w_down[li, 1])
    return rms_norm(x, final_norm)
```

```python
import functools
import math

import jax
import jax.numpy as jnp
from jax import lax
from jax.experimental import pallas as pl
from jax.experimental.pallas import tpu as pltpu

F32 = jnp.float32
BF16 = jnp.bfloat16
HIGHEST = lax.Precision.HIGHEST

NORM_EPS = 1e-6
LN_EPS = 1e-5
RWKV_GN_EPS = 64e-5
GRID_W = 64
N_MOD = 9
POOL_WINDOWS = (2, 4, 8, 16)
HEAD_DIM = 64
GMLP_CHUNK = 128
LANES = 128
SCAN_CHUNK = 64
VMEM_LIMIT = 56 * 1024 * 1024


def _params(*sem):
    return pltpu.CompilerParams(dimension_semantics=sem, vmem_limit_bytes=VMEM_LIMIT)


def _sigmoid(x):
    return 1.0 / (1.0 + jnp.exp(-x))


def _mm(a, b):
    return jnp.dot(a.astype(BF16), b.astype(BF16), preferred_element_type=F32)


def _mm_hi(a, b):
    return jnp.dot(a, b, precision=HIGHEST, preferred_element_type=F32)


def _mm_nt_hi(a, b):
    return lax.dot_general(a, b, (((1,), (1,)), ((), ())), precision=HIGHEST,
                           preferred_element_type=F32)


def _mm_tn_hi(a, b):
    return lax.dot_general(a, b, (((0,), (0,)), ((), ())), precision=HIGHEST,
                           preferred_element_type=F32)


def _rms(x, g):
    return x * lax.rsqrt(jnp.mean(x * x, axis=-1, keepdims=True) + NORM_EPS) * g


def _adaln_kernel(c_ref, w_ref, b_ref, o_ref):
    c = c_ref[...]
    o_ref[0] = _mm_hi(c * _sigmoid(c), w_ref[0]) + b_ref[0]


def _adaln(cond8, ada_w, ada_b):
    depth, d, n = ada_w.shape
    tn = 1024
    return pl.pallas_call(
        _adaln_kernel,
        grid=(depth, n // tn),
        in_specs=[pl.BlockSpec((8, d), lambda l, j: (0, 0)),
                  pl.BlockSpec((1, d, tn), lambda l, j: (l, 0, j)),
                  pl.BlockSpec((1, 1, tn), lambda l, j: (l, 0, j))],
        out_specs=pl.BlockSpec((1, 8, tn), lambda l, j: (l, 0, j)),
        out_shape=jax.ShapeDtypeStruct((depth, 8, n), F32),
        compiler_params=_params("arbitrary", "arbitrary"),
        name="adaln",
    )(cond8, ada_w, ada_b.reshape(depth, 1, n))


def _ffn_kernel(h_ref, mod_ref, g_ref, wg_ref, wu_ref, wd_ref, fn_ref, o_ref, hn_ref,
                *, base, final):
    j = pl.program_id(2)

    @pl.when(j == 0)
    def _():
        hn = _rms(h_ref[0], g_ref[...]) * (1.0 + mod_ref[0, base + 1:base + 2, :]) \
            + mod_ref[0, base:base + 1, :]
        hn_ref[...] = hn.astype(BF16)
        o_ref[0] = jnp.zeros(o_ref.shape[1:], F32)

    hn = hn_ref[...]
    gate = jnp.dot(hn, wg_ref[...], preferred_element_type=F32)
    up = jnp.dot(hn, wu_ref[...], preferred_element_type=F32)
    act = (gate * _sigmoid(gate) * up).astype(BF16)
    o_ref[0] += jnp.dot(act, wd_ref[...], preferred_element_type=F32)

    @pl.when(j == pl.num_programs(2) - 1)
    def _():
        out = h_ref[0] + 0.5 * mod_ref[0, base + 2:base + 3, :] * o_ref[0]
        if final:
            out = _rms(out, fn_ref[...])
        o_ref[0] = out


def _ffn(h, mod, g, wg, wu, wd, fn, *, base, final, tm, tf):
    nb, l, d = h.shape
    f = wg.shape[1]
    per_batch = mod.shape[0] > 1
    mod_map = (lambda b, i, j: (b, 0, 0)) if per_batch else (lambda b, i, j: (0, 0, 0))
    return pl.pallas_call(
        functools.partial(_ffn_kernel, base=base, final=final),
        grid=(nb, l // tm, f // tf),
        in_specs=[pl.BlockSpec((1, tm, d), lambda b, i, j: (b, i, 0)),
                  pl.BlockSpec((1, N_MOD, d), mod_map),
                  pl.BlockSpec((1, d), lambda b, i, j: (0, 0)),
                  pl.BlockSpec((d, tf), lambda b, i, j: (0, j)),
                  pl.BlockSpec((d, tf), lambda b, i, j: (0, j)),
                  pl.BlockSpec((tf, d), lambda b, i, j: (j, 0)),
                  pl.BlockSpec((1, d), lambda b, i, j: (0, 0))],
        out_specs=pl.BlockSpec((1, tm, d), lambda b, i, j: (b, i, 0)),
        out_shape=jax.ShapeDtypeStruct((nb, l, d), F32),
        scratch_shapes=[pltpu.VMEM((tm, d), BF16)],
        compiler_params=_params("arbitrary", "arbitrary", "arbitrary"),
        name="ffn",
    )(h, mod, g.reshape(1, d), wg, wu, wd, fn.reshape(1, d))


def _norm_mm_kernel(h_ref, mod_ref, g_ref, w_ref, o_ref, hn_ref):
    @pl.when(pl.program_id(2) == 0)
    def _():
        hn = _rms(h_ref[0], g_ref[...]) * (1.0 + mod_ref[0, 4:5, :]) + mod_ref[0, 3:4, :]
        hn_ref[...] = hn.astype(BF16)

    o_ref[0] = jnp.dot(hn_ref[...], w_ref[...], preferred_element_type=F32)


def _norm_mm(h, mod, g, w, *, tm, tn):
    nb, l, d = h.shape
    n = w.shape[1]
    per_batch = mod.shape[0] > 1
    mod_map = (lambda b, i, j: (b, 0, 0)) if per_batch else (lambda b, i, j: (0, 0, 0))
    return pl.pallas_call(
        _norm_mm_kernel,
        grid=(nb, l // tm, n // tn),
        in_specs=[pl.BlockSpec((1, tm, d), lambda b, i, j: (b, i, 0)),
                  pl.BlockSpec((1, N_MOD, d), mod_map),
                  pl.BlockSpec((1, d), lambda b, i, j: (0, 0)),
                  pl.BlockSpec((d, tn), lambda b, i, j: (0, j))],
        out_specs=pl.BlockSpec((1, tm, tn), lambda b, i, j: (b, i, j)),
        out_shape=jax.ShapeDtypeStruct((nb, l, n), F32),
        scratch_shapes=[pltpu.VMEM((tm, d), BF16)],
        compiler_params=_params("arbitrary", "arbitrary", "arbitrary"),
        name="norm_mm",
    )(h, mod, g.reshape(1, d), w)


POOL_HALO = 8


def _pool_kernel(prev_ref, x_ref, next_ref, pw_ref, ps_ref, o_ref, scr, *, t_rows, seq):
    i = pl.program_id(1)
    last = pl.num_programs(1) - 1
    hal = POOL_HALO
    scr[0:hal] = jnp.where(i > 0, prev_ref[0], 0.0)
    scr[hal:hal + t_rows] = x_ref[0]
    scr[hal + t_rows:2 * hal + t_rows] = jnp.where(i < last, next_ref[0], 0.0)
    t = i * t_rows + lax.broadcasted_iota(jnp.int32, (t_rows, LANES), 0)
    for gi, w in enumerate(POOL_WINDOWS):
        c0 = gi * LANES
        s = scr[hal - w // 2:hal - w // 2 + t_rows, c0:c0 + LANES]
        for dlt in range(-(w // 2) + 1, w // 2):
            s = s + scr[hal + dlt:hal + dlt + t_rows, c0:c0 + LANES]
        lo = jnp.clip(t - w // 2, 0, seq)
        hi = jnp.clip(t - w // 2 + w, 0, seq)
        cnt = (hi - lo).astype(F32)
        p = s / cnt - scr[hal:hal + t_rows, c0:c0 + LANES]
        y = _mm(p, pw_ref[gi]) * ps_ref[:, c0:c0 + LANES]
        o_ref[0, :, c0:c0 + LANES] = y.astype(o_ref.dtype)


def _pool(za, pool_w, pool_scale, *, t_rows):
    nb, l, c = za.shape
    hal = POOL_HALO
    r = t_rows // hal
    nh = l // hal
    return pl.pallas_call(
        functools.partial(_pool_kernel, t_rows=t_rows, seq=l),
        grid=(nb, l // t_rows),
        in_specs=[pl.BlockSpec((1, hal, c), lambda b, i: (b, jnp.maximum(i * r - 1, 0), 0)),
                  pl.BlockSpec((1, t_rows, c), lambda b, i: (b, i, 0)),
                  pl.BlockSpec((1, hal, c), lambda b, i: (b, jnp.minimum((i + 1) * r, nh - 1), 0)),
                  pl.BlockSpec(pool_w.shape, lambda b, i: (0, 0, 0)),
                  pl.BlockSpec((1, c), lambda b, i: (0, 0))],
        out_specs=pl.BlockSpec((1, t_rows, c), lambda b, i: (b, i, 0)),
        out_shape=jax.ShapeDtypeStruct((nb, l, c), BF16),
        scratch_shapes=[pltpu.VMEM((t_rows + 2 * hal, c), F32)],
        compiler_params=_params("arbitrary", "arbitrary"),
        name="pool",
    )(za, za, za, pool_w.astype(BF16), pool_scale.reshape(1, c))


def _gelu(x):
    return x * (0.5 * (1.0 + jnp.tanh(math.sqrt(2.0 / math.pi) * (x + 0.044715 * (x * x * x)))))


def _gmlp_kernel(z_ref, lng_ref, lnb_ref, ws_ref, bsb_ref, o_ref, *, t_rows):
    width = o_ref.shape[2]
    groups = ws_ref.shape[0]
    gd = width // groups
    for ci in range(t_rows // GMLP_CHUNK):
        rows = slice(ci * GMLP_CHUNK, (ci + 1) * GMLP_CHUNK)
        u = _gelu(z_ref[0, rows, 0:width])
        v = _gelu(z_ref[0, rows, width:2 * width])
        mu = jnp.mean(v, axis=-1, keepdims=True)
        var = jnp.mean(jnp.square(v - mu), axis=-1, keepdims=True)
        vn = ((v - mu) * lax.rsqrt(var + LN_EPS)) * lng_ref[...] + lnb_ref[...]
        for g in range(groups):
            cols = slice(g * gd, (g + 1) * gd)
            s = _mm(ws_ref[g], vn[:, cols]) + bsb_ref[g]
            o_ref[0, rows, cols] = (u[:, cols] * s).astype(o_ref.dtype)


def _gmlp(zuv, ln_g, ln_b, ws, bs, *, t_rows):
    nb, l, c2 = zuv.shape
    c = c2 // 2
    groups = ws.shape[0]
    bsb = jnp.broadcast_to(bs[:, :, None], (groups, GMLP_CHUNK, c // groups))
    return pl.pallas_call(
        functools.partial(_gmlp_kernel, t_rows=t_rows),
        grid=(nb, l // t_rows),
        in_specs=[pl.BlockSpec((1, t_rows, c2), lambda b, i: (b, i, 0)),
                  pl.BlockSpec((1, c), lambda b, i: (0, 0)),
                  pl.BlockSpec((1, c), lambda b, i: (0, 0)),
                  pl.BlockSpec(ws.shape, lambda b, i: (0, 0, 0)),
                  pl.BlockSpec(bsb.shape, lambda b, i: (0, 0, 0))],
        out_specs=pl.BlockSpec((1, t_rows, c), lambda b, i: (b, i, 0)),
        out_shape=jax.ShapeDtypeStruct((nb, l, c), BF16),
        compiler_params=_params("arbitrary", "arbitrary"),
        name="gmlp",
    )(zuv, ln_g.reshape(1, c), ln_b.reshape(1, c), ws.astype(BF16), bsb)


def _segsum(x, bd):
    hi = x.astype(BF16)
    lo = (x - hi.astype(F32)).astype(BF16)
    return (jnp.dot(hi, bd, preferred_element_type=F32)
            + jnp.dot(lo, bd, preferred_element_type=F32))


def _rwkv_prep_kernel(prev_ref, x_ref, next_ref, mu_ref, w0_ref, wup_ref, a0_ref, aup_ref,
                      gup_ref, kk_ref, ka_ref, rk_ref, bd_ref,
                      lw0_o, lw1_o, kd0_o, kd1_o, b0_o, b1_o, kk_o, v_o, r_o, g_o, bv_o,
                      scr, *, t_rows, grid_mode, width):
    i = pl.program_id(1)
    last = pl.num_programs(1) - 1
    hal = GRID_W
    scr[0:hal] = jnp.where(i > 0, prev_ref[0], 0.0)
    scr[hal:hal + t_rows] = x_ref[0]
    scr[hal + t_rows:2 * hal + t_rows] = jnp.where(i < last, next_ref[0], 0.0)
    cols = scr.shape[1]
    if grid_mode:
        q = cols // 4
        bounds = (0, q, 2 * q, 3 * q, cols)
        offs = (-1, 1, -GRID_W, GRID_W)
    else:
        bounds = (0, cols // 2, cols)
        offs = (-1, 1)

    def zs_cols(c0, c1):
        n = c1 - c0
        x = scr[hal:hal + t_rows, c0:c1]
        ch = c0 + lax.broadcasted_iota(jnp.int32, (t_rows, n), 1)
        col = lax.broadcasted_iota(jnp.int32, (t_rows, n), 0) & (GRID_W - 1)
        shifted = None
        for qi, off in enumerate(offs):
            lo, hi = bounds[qi], bounds[qi + 1]
            if hi <= c0 or lo >= c1:
                continue
            src = scr[hal + off:hal + off + t_rows, c0:c1]
            if grid_mode and off == -1:
                src = jnp.where(col == 0, 0.0, src)
            if grid_mode and off == 1:
                src = jnp.where(col == GRID_W - 1, 0.0, src)
            shifted = src if shifted is None else jnp.where(ch >= lo, src, shifted)
        return x + (shifted - x) * mu_ref[:, c0:c1]

    w = width
    r = zs_cols(0, w)
    k = zs_cols(w, 2 * w)
    v = zs_cols(2 * w, 3 * w)
    rest = zs_cols(3 * w, cols)
    wd = jnp.tanh(rest[:, 0:LANES])
    ad = rest[:, LANES:2 * LANES]
    gd = _sigmoid(rest[:, 2 * LANES:3 * LANES])

    bd = bd_ref[...]
    kk0 = k * kk_ref[...]
    ss = _segsum(kk0 * kk0, bd)
    kk = kk0 / jnp.maximum(jnp.sqrt(ss), 1e-12)
    npair = w // LANES

    def put(o_ref, val):
        for p in range(npair):
            o_ref[0, p] = val[:, p * LANES:(p + 1) * LANES]

    put(kk_o, kk)
    put(v_o, v)
    put(r_o, r)
    kd_sum = None
    for d, (lw_o, kd_o, b_o) in enumerate(((lw0_o, kd0_o, b0_o), (lw1_o, kd1_o, b1_o))):
        w_pre = w0_ref[d] + _mm_hi(wd, wup_ref[d])
        put(lw_o, -math.exp(-0.5) * _sigmoid(w_pre))
        a = _sigmoid(a0_ref[d] + _mm_hi(ad, aup_ref[d]))
        kd = k * (1.0 + (a - 1.0) * ka_ref[...])
        put(kd_o, kd)
        put(b_o, kk * a)
        kd_sum = kd if kd_sum is None else kd_sum + kd
    g_o[0] = _mm_hi(gd, gup_ref[...])
    bv_o[0] = _segsum(r * kd_sum * rk_ref[...], bd) * v


def _head_blockdiag(width):
    idx = jnp.arange(width) // HEAD_DIM
    return (idx[:, None] == idx[None, :]).astype(BF16)


def _rwkv_prep(zb, lp, *, t_rows, grid_mode):
    nb, l, cols = zb.shape
    w = lp['w0'].shape[-1]
    npair = w // LANES
    hal = GRID_W
    r = t_rows // hal
    nh = l // hal
    lora = lp['w_up'].shape[1]

    def pad_dir(up):
        z = jnp.zeros_like(up[0])
        return jnp.stack([jnp.concatenate([up[0], z], 0), jnp.concatenate([z, up[1]], 0)], 0)

    const2 = lambda b, i: (0, 0)
    const3 = lambda b, i: (0, 0, 0)
    pair_spec = pl.BlockSpec((1, npair, t_rows, LANES), lambda b, i: (b, 0, i, 0))
    wide_spec = pl.BlockSpec((1, t_rows, w), lambda b, i: (b, i, 0))
    pair_shape = jax.ShapeDtypeStruct((nb, npair, l, LANES), F32)
    wide_shape = jax.ShapeDtypeStruct((nb, l, w), F32)
    return pl.pallas_call(
        functools.partial(_rwkv_prep_kernel, t_rows=t_rows, grid_mode=grid_mode, width=w),
        grid=(nb, l // t_rows),
        in_specs=[pl.BlockSpec((1, hal, cols), lambda b, i: (b, jnp.maximum(i * r - 1, 0), 0)),
                  pl.BlockSpec((1, t_rows, cols), lambda b, i: (b, i, 0)),
                  pl.BlockSpec((1, hal, cols), lambda b, i: (b, jnp.minimum((i + 1) * r, nh - 1), 0)),
                  pl.BlockSpec((1, cols), const2),
                  pl.BlockSpec((2, 1, w), const3),
                  pl.BlockSpec((2, 2 * lora, w), const3),
                  pl.BlockSpec((2, 1, w), const3),
                  pl.BlockSpec((2, 2 * lora, w), const3),
                  pl.BlockSpec(lp['g_up'].shape, const2),
                  pl.BlockSpec((1, w), const2),
                  pl.BlockSpec((1, w), const2),
                  pl.BlockSpec((1, w), const2),
                  pl.BlockSpec((w, w), const2)],
        out_specs=[pair_spec] * 9 + [wide_spec] * 2,
        out_shape=[pair_shape] * 9 + [wide_shape] * 2,
        scratch_shapes=[pltpu.VMEM((t_rows + 2 * hal, cols), F32)],
        compiler_params=_params("arbitrary", "arbitrary"),
        name="rwkv_prep",
    )(zb, zb, zb, lp['mu'].reshape(1, cols), lp['w0'].reshape(2, 1, w), pad_dir(lp['w_up']),
      lp['a0'].reshape(2, 1, w), pad_dir(lp['a_up']), lp['g_up'], lp['k_k'].reshape(1, w),
      lp['k_a'].reshape(1, w), lp['r_k'].reshape(1, w), _head_blockdiag(w))


def _stack2(x, m0):
    return jnp.concatenate([jnp.where(m0, x, 0.0), jnp.where(m0, 0.0, x)], axis=0)


def _chunk_step(lw, kd, b, kk, v, r, s, *, rev, tri, strict, incl, m0):
    c = lw.shape[0]
    cum = _mm_hi(tri, lw)
    cum_prev = cum - lw
    end = 0 if rev else c - 1
    tot = cum[end:end + 1, :]
    mid = cum[c // 2:c // 2 + 1, :]
    e_inv = jnp.exp(mid - cum)
    nkk = -kk
    left = jnp.concatenate([_stack2(nkk * jnp.exp(cum_prev - mid), m0),
                            _stack2(r * jnp.exp(cum - mid), m0)], axis=0)
    right = jnp.concatenate([_stack2(b * e_inv, m0), _stack2(kd * e_inv, m0)], axis=0)
    gram = _mm_nt_hi(left, right)
    c2 = 2 * c
    a_ab = jnp.where(strict, gram[0:c2, 0:c2], 0.0)
    a_ak = jnp.where(strict, gram[0:c2, c2:2 * c2], 0.0)
    a_rb = jnp.where(incl, gram[c2:2 * c2, 0:c2], 0.0)
    a_rk = jnp.where(incl, gram[c2:2 * c2, c2:2 * c2], 0.0)
    vbd = _stack2(v, m0)
    x = _mm_nt_hi(_stack2(nkk * jnp.exp(cum_prev), m0), s) + _mm_hi(a_ak, vbd)
    p = a_ab
    steps = max(1, (c - 1).bit_length())
    for it in range(steps):
        x = x + _mm_hi(p, x)
        if it < steps - 1:
            p = _mm_hi(p, p)
    u = x
    ybd = _mm_nt_hi(_stack2(r * jnp.exp(cum), m0), s) + _mm_hi(a_rb, u) + _mm_hi(a_rk, vbd)
    y = ybd[0:c] + ybd[c:c2]
    e_end = jnp.exp(tot - cum)
    s_new = (s * jnp.exp(tot) + _mm_tn_hi(u, _stack2(b * e_end, m0))
             + _mm_tn_hi(vbd, _stack2(kd * e_end, m0)))
    return y, s_new


def _rwkv_scan_kernel(lw0, kd0, b0, kkf, vf, rf, lw1, kd1, b1, kkb, vb, rb, s0_ref,
                      yf_o, yb_o, sfin_o, st):
    ci = pl.program_id(1)
    c = lw0.shape[2]
    npair = lw0.shape[1]

    @pl.when(ci == 0)
    def _():
        st[...] = s0_ref[:, 0]

    row = lax.broadcasted_iota(jnp.int32, (c, c), 0)
    colm = lax.broadcasted_iota(jnp.int32, (c, c), 1)
    tri_f = (colm <= row).astype(F32)
    tri_b = (colm >= row).astype(F32)
    r2 = lax.broadcasted_iota(jnp.int32, (2 * c, 2 * c), 0) & (c - 1)
    c2 = lax.broadcasted_iota(jnp.int32, (2 * c, 2 * c), 1) & (c - 1)
    m0 = lax.broadcasted_iota(jnp.int32, (1, LANES), 1) < HEAD_DIM

    def body(p, carry):
        y, s_new = _chunk_step(lw0[0, p], kd0[0, p], b0[0, p], kkf[0, p], vf[0, p], rf[0, p],
                               st[0, p], rev=False, tri=tri_f, strict=c2 < r2, incl=c2 <= r2,
                               m0=m0)
        yf_o[0, p] = y
        st[0, p] = s_new
        y, s_new = _chunk_step(lw1[0, p], kd1[0, p], b1[0, p], kkb[0, p], vb[0, p], rb[0, p],
                               st[1, p], rev=True, tri=tri_b, strict=c2 > r2, incl=c2 >= r2,
                               m0=m0)
        yb_o[0, p] = y
        st[1, p] = s_new
        return carry

    lax.fori_loop(0, npair, body, 0)

    @pl.when(ci == pl.num_programs(1) - 1)
    def _():
        sfin_o[:, 0] = st[...]


def _rwkv_scan(prep, s0):
    lw0, lw1, kd0, kd1, b0, b1, kk, v, r = prep
    nb, npair, l, _ = lw0.shape
    c = SCAN_CHUNK
    n = l // c
    fwd = pl.BlockSpec((1, npair, c, LANES), lambda b, i: (b, 0, i, 0))
    bwd = pl.BlockSpec((1, npair, c, LANES), lambda b, i: (b, 0, n - 1 - i, 0))
    st_spec = pl.BlockSpec((2, 1, npair, LANES, LANES), lambda b, i: (0, b, 0, 0, 0))
    y_shape = jax.ShapeDtypeStruct((nb, npair, l, LANES), F32)
    return pl.pallas_call(
        _rwkv_scan_kernel,
        grid=(nb, n),
        in_specs=[fwd] * 6 + [bwd] * 6 + [st_spec],
        out_specs=[fwd, bwd, st_spec],
        out_shape=[y_shape, y_shape, jax.ShapeDtypeStruct(s0.shape, F32)],
        scratch_shapes=[pltpu.VMEM((2, npair, LANES, LANES), F32)],
        compiler_params=_params("arbitrary", "arbitrary"),
        name="rwkv_scan",
    )(lw0, kd0, b0, kk, v, r, lw1, kd1, b1, kk, v, r, s0)


def _rwkv_post_kernel(yf_ref, yb_ref, bv_ref, g_ref, lng_ref, lnb_ref, bd_ref, o_ref):
    npair = yf_ref.shape[1]
    y = jnp.concatenate([yf_ref[0, p] + yb_ref[0, p] for p in range(npair)], axis=-1)
    bd = bd_ref[...]
    inv = 1.0 / HEAD_DIM
    m = _segsum(y, bd) * inv
    dlt = y - m
    var = _segsum(dlt * dlt, bd) * inv
    yn = dlt * lax.rsqrt(var + RWKV_GN_EPS) * lng_ref[...] + lnb_ref[...]
    o_ref[0] = ((yn + bv_ref[0]) * g_ref[0]).astype(o_ref.dtype)


def _rwkv_post(yf, yb, bv, g, ln_g, ln_b, *, t_rows):
    nb, npair, l, _ = yf.shape
    w = npair * LANES
    pair_spec = pl.BlockSpec((1, npair, t_rows, LANES), lambda b, i: (b, 0, i, 0))
    wide_spec = pl.BlockSpec((1, t_rows, w), lambda b, i: (b, i, 0))
    const2 = lambda b, i: (0, 0)
    return pl.pallas_call(
        _rwkv_post_kernel,
        grid=(nb, l // t_rows),
        in_specs=[pair_spec, pair_spec, wide_spec, wide_spec,
                  pl.BlockSpec((1, w), const2), pl.BlockSpec((1, w), const2),
                  pl.BlockSpec((w, w), const2)],
        out_specs=wide_spec,
        out_shape=jax.ShapeDtypeStruct((nb, l, w), BF16),
        compiler_params=_params("arbitrary", "arbitrary"),
        name="rwkv_post",
    )(yf, yb, bv, g, ln_g.reshape(1, w), ln_b.reshape(1, w), _head_blockdiag(w))


def _merge_kernel(h_ref, mod_ref, zg_ref, yp_ref, yr_ref, yg_ref, pp_ref, pr_ref, pg_ref,
                  wo_ref, o_ref):
    d = h_ref.shape[2]
    merged = (_sigmoid(zg_ref[0, :, 0:d]) * jnp.dot(yp_ref[0], pp_ref[...], preferred_element_type=F32)
              + _sigmoid(zg_ref[0, :, d:2 * d]) * jnp.dot(yr_ref[0], pr_ref[...], preferred_element_type=F32)
              + _sigmoid(zg_ref[0, :, 2 * d:3 * d]) * jnp.dot(yg_ref[0], pg_ref[...], preferred_element_type=F32))
    m = jnp.dot(merged.astype(BF16), wo_ref[...], preferred_element_type=F32)
    o_ref[0] = h_ref[0] + mod_ref[0, 5:6, :] * m


def _merge(h, mod, zg, yp, yr, yg, pp, pr, pg, wo, *, tm):
    nb, l, d = h.shape
    per_batch = mod.shape[0] > 1
    mod_map = (lambda b, i: (b, 0, 0)) if per_batch else (lambda b, i: (0, 0, 0))
    row = lambda width: pl.BlockSpec((1, tm, width), lambda b, i: (b, i, 0))
    full = lambda a: pl.BlockSpec(a.shape, lambda b, i: (0, 0))
    return pl.pallas_call(
        _merge_kernel,
        grid=(nb, l // tm),
        in_specs=[row(d), pl.BlockSpec((1, N_MOD, d), mod_map), row(zg.shape[2]),
                  row(yp.shape[2]), row(yr.shape[2]), row(yg.shape[2]),
                  full(pp), full(pr), full(pg), full(wo)],
        out_specs=row(d),
        out_shape=jax.ShapeDtypeStruct((nb, l, d), F32),
        compiler_params=_params("arbitrary", "arbitrary"),
        name="merge",
    )(h, mod, zg, yp, yr, yg, pp, pr, pg, wo)


def _pick(n, pref):
    t = min(pref, n)
    while t > LANES and (n % t or t % LANES):
        t -= LANES
    return t if n % t == 0 else n


def kernel(x, c, ctx, c_ctx, ada_w, ada_b, norm_g, ffn_w_gate, ffn_w_up, ffn_w_down, w_in,
           pool_w, pool_scale, rwkv_mu, rwkv_w0, rwkv_w_up, rwkv_a0, rwkv_a_up, rwkv_g_up,
           rwkv_k_k, rwkv_k_a, rwkv_r_k, rwkv_ln_g, rwkv_ln_b, gmlp_ln_g, gmlp_ln_b, gmlp_ws,
           gmlp_bs, proj_pool, proj_rwkv, proj_gmlp, w_out, final_norm):
    depth = ada_w.shape[0]
    nb, l, d = x.shape
    lc = ctx.shape[1]
    width = rwkv_w0.shape[-1]
    pool_c = pool_scale.shape[-1]
    gmlp_c = gmlp_ln_g.shape[-1]
    rwkv_cols = rwkv_mu.shape[-1]
    off_rwkv = pool_c
    off_gmlp = off_rwkv + rwkv_cols
    off_gate = off_gmlp + 2 * gmlp_c
    npair = width // LANES

    cond8 = jnp.zeros((8, d), F32).at[:nb].set(c).at[nb].set(c_ctx)
    mod_all = _adaln(cond8, ada_w, ada_b).reshape(depth, 8, N_MOD, d)

    zero_state = jnp.zeros((2, nb, npair, LANES, LANES), F32)
    hc = ctx
    for li in range(depth):
        last = li == depth - 1
        mod_x = mod_all[li, :nb]
        mod_c = mod_all[li, nb:nb + 1]
        wg = ffn_w_gate[li].astype(BF16)
        wu = ffn_w_up[li].astype(BF16)
        wd = ffn_w_down[li].astype(BF16)
        win = w_in[li].astype(BF16)
        w_za = win[:, :off_rwkv]
        w_zb = win[:, off_rwkv:off_gmlp]
        w_zuv = win[:, off_gmlp:off_gate]
        w_zg = win[:, off_gate:]
        pp = proj_pool[li].astype(BF16)
        pr = proj_rwkv[li].astype(BF16)
        pg = proj_gmlp[li].astype(BF16)
        wo = w_out[li].astype(BF16)
        lp = {'mu': rwkv_mu[li], 'w0': rwkv_w0[li], 'w_up': rwkv_w_up[li], 'a0': rwkv_a0[li],
              'a_up': rwkv_a_up[li], 'g_up': rwkv_g_up[li], 'k_k': rwkv_k_k[li],
              'k_a': rwkv_k_a[li], 'r_k': rwkv_r_k[li]}
        tf = _pick(wg.shape[-1], 512)

        def mixer(h, mod, seq, grid_mode, s0, need_out):
            tm = _pick(seq, 512)
            zb = _norm_mm(h, mod, norm_g[li, 1], w_zb, tm=tm, tn=_pick(rwkv_cols, 1152))
            prep = _rwkv_prep(zb, lp, t_rows=_pick(seq, 128), grid_mode=grid_mode)
            yf, yb, s_fin = _rwkv_scan(prep[:9], s0)
            if not need_out:
                return None, s_fin
            y_rwkv = _rwkv_post(yf, yb, prep[10], prep[9], rwkv_ln_g[li], rwkv_ln_b[li],
                                t_rows=_pick(seq, 256))
            za = _norm_mm(h, mod, norm_g[li, 1], w_za, tm=tm, tn=_pick(pool_c, 512))
            y_pool = _pool(za, pool_w[li], pool_scale[li], t_rows=_pick(seq, 512))
            zuv = _norm_mm(h, mod, norm_g[li, 1], w_zuv, tm=tm, tn=_pick(2 * gmlp_c, 512))
            y_gmlp = _gmlp(zuv, gmlp_ln_g[li], gmlp_ln_b[li], gmlp_ws[li], gmlp_bs[li],
                           t_rows=_pick(seq, 512))
            zg = _norm_mm(h, mod, norm_g[li, 1], w_zg, tm=tm, tn=_pick(3 * d, 1024))
            out = _merge(h, mod, zg, y_pool, y_rwkv, y_gmlp, pp, pr, pg, wo, tm=_pick(seq, 256))
            return out, s_fin

        hc = _ffn(hc, mod_c, norm_g[li, 0], wg[0], wu[0], wd[0], final_norm,
                  base=0, final=False, tm=_pick(lc, 512), tf=tf)
        hc_new, s_ctx = mixer(hc, mod_c, lc, False, zero_state, not last)
        if not last:
            hc = _ffn(hc_new, mod_c, norm_g[li, 2], wg[1], wu[1], wd[1], final_norm,
                      base=6, final=False, tm=_pick(lc, 512), tf=tf)

        x = _ffn(x, mod_x, norm_g[li, 0], wg[0], wu[0], wd[0], final_norm,
                 base=0, final=False, tm=_pick(l, 512), tf=tf)
        x, _ = mixer(x, mod_x, l, True, s_ctx, True)
        x = _ffn(x, mod_x, norm_g[li, 2], wg[1], wu[1], wd[1], final_norm,
                 base=6, final=last, tm=_pick(l, 512), tf=tf)
    return x
```

```python
import functools
import math

import jax
import jax.numpy as jnp
from jax import lax
from jax.experimental import pallas as pl
from jax.experimental.pallas import tpu as pltpu

F32 = jnp.float32
BF16 = jnp.bfloat16
HIGHEST = lax.Precision.HIGHEST

NORM_EPS = 1e-6
LN_EPS = 1e-5
RWKV_GN_EPS = 64e-5
GRID_W = 64
N_MOD = 9
POOL_WINDOWS = (2, 4, 8, 16)
HEAD_DIM = 64
GMLP_CHUNK = 128
LANES = 128
SCAN_CHUNK = 64
VMEM_LIMIT = 56 * 1024 * 1024


def _params(*sem):
    return pltpu.CompilerParams(dimension_semantics=sem, vmem_limit_bytes=VMEM_LIMIT)


def _sigmoid(x):
    return 1.0 / (1.0 + jnp.exp(-x))


def _mm(a, b):
    return jnp.dot(a.astype(BF16), b.astype(BF16), preferred_element_type=F32)


def _mm_hi(a, b):
    return jnp.dot(a, b, precision=HIGHEST, preferred_element_type=F32)


def _mm_nt_hi(a, b):
    return lax.dot_general(a, b, (((1,), (1,)), ((), ())), precision=HIGHEST,
                           preferred_element_type=F32)


def _mm_tn_hi(a, b):
    return lax.dot_general(a, b, (((0,), (0,)), ((), ())), precision=HIGHEST,
                           preferred_element_type=F32)


def _rms(x, g):
    return x * lax.rsqrt(jnp.mean(x * x, axis=-1, keepdims=True) + NORM_EPS) * g


def _adaln_kernel(c_ref, w_ref, b_ref, o_ref):
    c = c_ref[...]
    o_ref[0] = _mm_hi(c * _sigmoid(c), w_ref[0]) + b_ref[0]


def _adaln(cond8, ada_w, ada_b):
    depth, d, n = ada_w.shape
    tn = 1024
    return pl.pallas_call(
        _adaln_kernel,
        grid=(depth, n // tn),
        in_specs=[pl.BlockSpec((8, d), lambda l, j: (0, 0)),
                  pl.BlockSpec((1, d, tn), lambda l, j: (l, 0, j)),
                  pl.BlockSpec((1, 1, tn), lambda l, j: (l, 0, j))],
        out_specs=pl.BlockSpec((1, 8, tn), lambda l, j: (l, 0, j)),
        out_shape=jax.ShapeDtypeStruct((depth, 8, n), F32),
        compiler_params=_params("arbitrary", "arbitrary"),
        name="adaln",
    )(cond8, ada_w, ada_b.reshape(depth, 1, n))


def _ffn_kernel(h_ref, mod_ref, g_ref, wg_ref, wu_ref, wd_ref, fn_ref, o_ref, hn_ref,
                *, base, final):
    j = pl.program_id(2)

    @pl.when(j == 0)
    def _():
        hn = _rms(h_ref[0], g_ref[...]) * (1.0 + mod_ref[0, base + 1:base + 2, :]) \
            + mod_ref[0, base:base + 1, :]
        hn_ref[...] = hn.astype(BF16)
        o_ref[0] = jnp.zeros(o_ref.shape[1:], F32)

    hn = hn_ref[...]
    gate = jnp.dot(hn, wg_ref[...], preferred_element_type=F32)
    up = jnp.dot(hn, wu_ref[...], preferred_element_type=F32)
    act = (gate * _sigmoid(gate) * up).astype(BF16)
    o_ref[0] += jnp.dot(act, wd_ref[...], preferred_element_type=F32)

    @pl.when(j == pl.num_programs(2) - 1)
    def _():
        out = h_ref[0] + 0.5 * mod_ref[0, base + 2:base + 3, :] * o_ref[0]
        if final:
            out = _rms(out, fn_ref[...])
        o_ref[0] = out


def _ffn(h, mod, g, wg, wu, wd, fn, *, base, final, tm, tf):
    nb, l, d = h.shape
    f = wg.shape[1]
    per_batch = mod.shape[0] > 1
    mod_map = (lambda b, i, j: (b, 0, 0)) if per_batch else (lambda b, i, j: (0, 0, 0))
    return pl.pallas_call(
        functools.partial(_ffn_kernel, base=base, final=final),
        grid=(nb, l // tm, f // tf),
        in_specs=[pl.BlockSpec((1, tm, d), lambda b, i, j: (b, i, 0)),
                  pl.BlockSpec((1, N_MOD, d), mod_map),
                  pl.BlockSpec((1, d), lambda b, i, j: (0, 0)),
                  pl.BlockSpec((d, tf), lambda b, i, j: (0, j)),
                  pl.BlockSpec((d, tf), lambda b, i, j: (0, j)),
                  pl.BlockSpec((tf, d), lambda b, i, j: (j, 0)),
                  pl.BlockSpec((1, d), lambda b, i, j: (0, 0))],
        out_specs=pl.BlockSpec((1, tm, d), lambda b, i, j: (b, i, 0)),
        out_shape=jax.ShapeDtypeStruct((nb, l, d), F32),
        scratch_shapes=[pltpu.VMEM((tm, d), BF16)],
        compiler_params=_params("arbitrary", "arbitrary", "arbitrary"),
        name="ffn",
    )(h, mod, g.reshape(1, d), wg, wu, wd, fn.reshape(1, d))


def _norm_mm_kernel(h_ref, mod_ref, g_ref, w_ref, o_ref, hn_ref):
    @pl.when(pl.program_id(2) == 0)
    def _():
        hn = _rms(h_ref[0], g_ref[...]) * (1.0 + mod_ref[0, 4:5, :]) + mod_ref[0, 3:4, :]
        hn_ref[...] = hn.astype(BF16)

    o_ref[0] = jnp.dot(hn_ref[...], w_ref[...], preferred_element_type=F32)


def _norm_mm(h, mod, g, w, *, tm, tn):
    nb, l, d = h.shape
    n = w.shape[1]
    per_batch = mod.shape[0] > 1
    mod_map = (lambda b, i, j: (b, 0, 0)) if per_batch else (lambda b, i, j: (0, 0, 0))
    return pl.pallas_call(
        _norm_mm_kernel,
        grid=(nb, l // tm, n // tn),
        in_specs=[pl.BlockSpec((1, tm, d), lambda b, i, j: (b, i, 0)),
                  pl.BlockSpec((1, N_MOD, d), mod_map),
                  pl.BlockSpec((1, d), lambda b, i, j: (0, 0)),
                  pl.BlockSpec((d, tn), lambda b, i, j: (0, j))],
        out_specs=pl.BlockSpec((1, tm, tn), lambda b, i, j: (b, i, j)),
        out_shape=jax.ShapeDtypeStruct((nb, l, n), F32),
        scratch_shapes=[pltpu.VMEM((tm, d), BF16)],
        compiler_params=_params("arbitrary", "arbitrary", "arbitrary"),
        name="norm_mm",
    )(h, mod, g.reshape(1, d), w)


POOL_HALO = 8


def _pool_kernel(prev_ref, x_ref, next_ref, pw_ref, ps_ref, o_ref, scr, *, t_rows, seq):
    i = pl.program_id(1)
    last = pl.num_programs(1) - 1
    hal = POOL_HALO
    scr[0:hal] = jnp.where(i > 0, prev_ref[0], 0.0)
    scr[hal:hal + t_rows] = x_ref[0]
    scr[hal + t_rows:2 * hal + t_rows] = jnp.where(i < last, next_ref[0], 0.0)
    t = i * t_rows + lax.broadcasted_iota(jnp.int32, (t_rows, LANES), 0)
    for gi, w in enumerate(POOL_WINDOWS):
        c0 = gi * LANES
        s = scr[hal - w // 2:hal - w // 2 + t_rows, c0:c0 + LANES]
        for dlt in range(-(w // 2) + 1, w // 2):
            s = s + scr[hal + dlt:hal + dlt + t_rows, c0:c0 + LANES]
        lo = jnp.clip(t - w // 2, 0, seq)
        hi = jnp.clip(t - w // 2 + w, 0, seq)
        cnt = (hi - lo).astype(F32)
        p = s / cnt - scr[hal:hal + t_rows, c0:c0 + LANES]
        y = _mm(p, pw_ref[gi]) * ps_ref[:, c0:c0 + LANES]
        o_ref[0, :, c0:c0 + LANES] = y.astype(o_ref.dtype)


def _pool(za, pool_w, pool_scale, *, t_rows):
    nb, l, c = za.shape
    hal = POOL_HALO
    r = t_rows // hal
    nh = l // hal
    return pl.pallas_call(
        functools.partial(_pool_kernel, t_rows=t_rows, seq=l),
        grid=(nb, l // t_rows),
        in_specs=[pl.BlockSpec((1, hal, c), lambda b, i: (b, jnp.maximum(i * r - 1, 0), 0)),
                  pl.BlockSpec((1, t_rows, c), lambda b, i: (b, i, 0)),
                  pl.BlockSpec((1, hal, c), lambda b, i: (b, jnp.minimum((i + 1) * r, nh - 1), 0)),
                  pl.BlockSpec(pool_w.shape, lambda b, i: (0, 0, 0)),
                  pl.BlockSpec((1, c), lambda b, i: (0, 0))],
        out_specs=pl.BlockSpec((1, t_rows, c), lambda b, i: (b, i, 0)),
        out_shape=jax.ShapeDtypeStruct((nb, l, c), BF16),
        scratch_shapes=[pltpu.VMEM((t_rows + 2 * hal, c), F32)],
        compiler_params=_params("arbitrary", "arbitrary"),
        name="pool",
    )(za, za, za, pool_w.astype(BF16), pool_scale.reshape(1, c))


def _gelu(x):
    return x * (0.5 * (1.0 + jnp.tanh(math.sqrt(2.0 / math.pi) * (x + 0.044715 * (x * x * x)))))


def _gmlp_kernel(z_ref, lng_ref, lnb_ref, ws_ref, bsb_ref, o_ref, *, t_rows):
    width = o_ref.shape[2]
    groups = ws_ref.shape[0]
    gd = width // groups
    for ci in range(t_rows // GMLP_CHUNK):
        rows = slice(ci * GMLP_CHUNK, (ci + 1) * GMLP_CHUNK)
        u = _gelu(z_ref[0, rows, 0:width])
        v = _gelu(z_ref[0, rows, width:2 * width])
        mu = jnp.mean(v, axis=-1, keepdims=True)
        var = jnp.mean(jnp.square(v - mu), axis=-1, keepdims=True)
        vn = ((v - mu) * lax.rsqrt(var + LN_EPS)) * lng_ref[...] + lnb_ref[...]
        for g in range(groups):
            cols = slice(g * gd, (g + 1) * gd)
            s = _mm(ws_ref[g], vn[:, cols]) + bsb_ref[g]
            o_ref[0, rows, cols] = (u[:, cols] * s).astype(o_ref.dtype)


def _gmlp(zuv, ln_g, ln_b, ws, bs, *, t_rows):
    nb, l, c2 = zuv.shape
    c = c2 // 2
    groups = ws.shape[0]
    bsb = jnp.broadcast_to(bs[:, :, None], (groups, GMLP_CHUNK, c // groups))
    return pl.pallas_call(
        functools.partial(_gmlp_kernel, t_rows=t_rows),
        grid=(nb, l // t_rows),
        in_specs=[pl.BlockSpec((1, t_rows, c2), lambda b, i: (b, i, 0)),
                  pl.BlockSpec((1, c), lambda b, i: (0, 0)),
                  pl.BlockSpec((1, c), lambda b, i: (0, 0)),
                  pl.BlockSpec(ws.shape, lambda b, i: (0, 0, 0)),
                  pl.BlockSpec(bsb.shape, lambda b, i: (0, 0, 0))],
        out_specs=pl.BlockSpec((1, t_rows, c), lambda b, i: (b, i, 0)),
        out_shape=jax.ShapeDtypeStruct((nb, l, c), BF16),
        compiler_params=_params("arbitrary", "arbitrary"),
        name="gmlp",
    )(zuv, ln_g.reshape(1, c), ln_b.reshape(1, c), ws.astype(BF16), bsb)


def _segsum(x, bd):
    hi = x.astype(BF16)
    lo = (x - hi.astype(F32)).astype(BF16)
    return (jnp.dot(hi, bd, preferred_element_type=F32)
            + jnp.dot(lo, bd, preferred_element_type=F32))


def _rwkv_prep_kernel(prev_ref, x_ref, next_ref, mu_ref, w0_ref, wup_ref, a0_ref, aup_ref,
                      gup_ref, kk_ref, ka_ref, rk_ref, bd_ref,
                      lw0_o, lw1_o, kd0_o, kd1_o, b0_o, b1_o, kk_o, v_o, r_o, g_o, bv_o,
                      scr, *, t_rows, grid_mode, width):
    i = pl.program_id(1)
    last = pl.num_programs(1) - 1
    hal = GRID_W
    scr[0:hal] = jnp.where(i > 0, prev_ref[0], 0.0)
    scr[hal:hal + t_rows] = x_ref[0]
    scr[hal + t_rows:2 * hal + t_rows] = jnp.where(i < last, next_ref[0], 0.0)
    cols = scr.shape[1]
    if grid_mode:
        q = cols // 4
        bounds = (0, q, 2 * q, 3 * q, cols)
        offs = (-1, 1, -GRID_W, GRID_W)
    else:
        bounds = (0, cols // 2, cols)
        offs = (-1, 1)

    def zs_cols(c0, c1):
        n = c1 - c0
        x = scr[hal:hal + t_rows, c0:c1]
        ch = c0 + lax.broadcasted_iota(jnp.int32, (t_rows, n), 1)
        col = lax.broadcasted_iota(jnp.int32, (t_rows, n), 0) & (GRID_W - 1)
        shifted = None
        for qi, off in enumerate(offs):
            lo, hi = bounds[qi], bounds[qi + 1]
            if hi <= c0 or lo >= c1:
                continue
            src = scr[hal + off:hal + off + t_rows, c0:c1]
            if grid_mode and off == -1:
                src = jnp.where(col == 0, 0.0, src)
            if grid_mode and off == 1:
                src = jnp.where(col == GRID_W - 1, 0.0, src)
            shifted = src if shifted is None else jnp.where(ch >= lo, src, shifted)
        return x + (shifted - x) * mu_ref[:, c0:c1]

    w = width
    r = zs_cols(0, w)
    k = zs_cols(w, 2 * w)
    v = zs_cols(2 * w, 3 * w)
    rest = zs_cols(3 * w, cols)
    wd = jnp.tanh(rest[:, 0:LANES])
    ad = rest[:, LANES:2 * LANES]
    gd = _sigmoid(rest[:, 2 * LANES:3 * LANES])

    bd = bd_ref[...]
    kk0 = k * kk_ref[...]
    ss = _segsum(kk0 * kk0, bd)
    kk = kk0 / jnp.maximum(jnp.sqrt(ss), 1e-12)
    npair = w // LANES

    def put(o_ref, val):
        for p in range(npair):
            o_ref[0, p] = val[:, p * LANES:(p + 1) * LANES]

    put(kk_o, kk)
    put(v_o, v)
    put(r_o, r)
    kd_sum = None
    for d, (lw_o, kd_o, b_o) in enumerate(((lw0_o, kd0_o, b0_o), (lw1_o, kd1_o, b1_o))):
        w_pre = w0_ref[d] + _mm_hi(wd, wup_ref[d])
        put(lw_o, -math.exp(-0.5) * _sigmoid(w_pre))
        a = _sigmoid(a0_ref[d] + _mm_hi(ad, aup_ref[d]))
        kd = k * (1.0 + (a - 1.0) * ka_ref[...])
        put(kd_o, kd)
        put(b_o, kk * a)
        kd_sum = kd if kd_sum is None else kd_sum + kd
    g_o[0] = _mm_hi(gd, gup_ref[...])
    bv_o[0] = _segsum(r * kd_sum * rk_ref[...], bd) * v


def _head_blockdiag(width):
    idx = jnp.arange(width) // HEAD_DIM
    return (idx[:, None] == idx[None, :]).astype(BF16)


def _rwkv_prep(zb, lp, *, t_rows, grid_mode):
    nb, l, cols = zb.shape
    w = lp['w0'].shape[-1]
    npair = w // LANES
    hal = GRID_W
    r = t_rows // hal
    nh = l // hal
    lora = lp['w_up'].shape[1]

    def pad_dir(up):
        z = jnp.zeros_like(up[0])
        return jnp.stack([jnp.concatenate([up[0], z], 0), jnp.concatenate([z, up[1]], 0)], 0)

    const2 = lambda b, i: (0, 0)
    const3 = lambda b, i: (0, 0, 0)
    pair_spec = pl.BlockSpec((1, npair, t_rows, LANES), lambda b, i: (b, 0, i, 0))
    wide_spec = pl.BlockSpec((1, t_rows, w), lambda b, i: (b, i, 0))
    pair_shape = jax.ShapeDtypeStruct((nb, npair, l, LANES), F32)
    wide_shape = jax.ShapeDtypeStruct((nb, l, w), F32)
    return pl.pallas_call(
        functools.partial(_rwkv_prep_kernel, t_rows=t_rows, grid_mode=grid_mode, width=w),
        grid=(nb, l // t_rows),
        in_specs=[pl.BlockSpec((1, hal, cols), lambda b, i: (b, jnp.maximum(i * r - 1, 0), 0)),
                  pl.BlockSpec((1, t_rows, cols), lambda b, i: (b, i, 0)),
                  pl.BlockSpec((1, hal, cols), lambda b, i: (b, jnp.minimum((i + 1) * r, nh - 1), 0)),
                  pl.BlockSpec((1, cols), const2),
                  pl.BlockSpec((2, 1, w), const3),
                  pl.BlockSpec((2, 2 * lora, w), const3),
                  pl.BlockSpec((2, 1, w), const3),
                  pl.BlockSpec((2, 2 * lora, w), const3),
                  pl.BlockSpec(lp['g_up'].shape, const2),
                  pl.BlockSpec((1, w), const2),
                  pl.BlockSpec((1, w), const2),
                  pl.BlockSpec((1, w), const2),
                  pl.BlockSpec((w, w), const2)],
        out_specs=[pair_spec] * 9 + [wide_spec] * 2,
        out_shape=[pair_shape] * 9 + [wide_shape] * 2,
        scratch_shapes=[pltpu.VMEM((t_rows + 2 * hal, cols), F32)],
        compiler_params=_params("arbitrary", "arbitrary"),
        name="rwkv_prep",
    )(zb, zb, zb, lp['mu'].reshape(1, cols), lp['w0'].reshape(2, 1, w), pad_dir(lp['w_up']),
      lp['a0'].reshape(2, 1, w), pad_dir(lp['a_up']), lp['g_up'], lp['k_k'].reshape(1, w),
      lp['k_a'].reshape(1, w), lp['r_k'].reshape(1, w), _head_blockdiag(w))


def _stack2(x, m0):
    return jnp.concatenate([jnp.where(m0, x, 0.0), jnp.where(m0, 0.0, x)], axis=0)


_NN = (((1,), (0,)), ((), ()))
_NT = (((1,), (1,)), ((), ()))
_TN = (((0,), (0,)), ((), ()))

SCAN_PREC = {"cum": "rhs2", "gram": "bf16", "init": "bf16", "square": "bf16", "apply": "bf16",
             "out": "bf16", "state": "bf16"}
SCAN_LOCKSTEP_PAIRS = 8

def _split(a):
    hi = a.astype(BF16)
    return hi, (a - hi.astype(F32)).astype(BF16)


def _dg(a, b, dims, site):
    mode = SCAN_PREC[site]
    if mode == "f32":
        return lax.dot_general(a, b, dims, precision=HIGHEST, preferred_element_type=F32)
    if mode == "bf16":
        return lax.dot_general(a.astype(BF16), b.astype(BF16), dims, preferred_element_type=F32)
    b_hi, b_lo = _split(b)
    if mode == "rhs2":
        a16 = a.astype(BF16)
        return (lax.dot_general(a16, b_hi, dims, preferred_element_type=F32)
                + lax.dot_general(a16, b_lo, dims, preferred_element_type=F32))
    a_hi, a_lo = _split(a)
    return (lax.dot_general(a_hi, b_hi, dims, preferred_element_type=F32)
            + lax.dot_general(a_hi, b_lo, dims, preferred_element_type=F32)
            + lax.dot_general(a_lo, b_hi, dims, preferred_element_type=F32))


def _chunk_step(refs, s_ref, *, rev, tri, strict, incl, m0):
    lw, kd, b, kk, v, r = (x[...] for x in refs)
    s = s_ref[...]
    c = lw.shape[0]
    cum = _dg(tri, lw, _NN, "cum")
    yield
    cum_prev = cum - lw
    end = 0 if rev else c - 1
    tot = cum[end:end + 1, :]
    mid = cum[c // 2:c // 2 + 1, :]
    e_inv = jnp.exp(mid - cum)
    nkk = -kk
    left = jnp.concatenate([_stack2(nkk * jnp.exp(cum_prev - mid), m0),
                            _stack2(r * jnp.exp(cum - mid), m0)], axis=0)
    right = jnp.concatenate([_stack2(b * e_inv, m0), _stack2(kd * e_inv, m0)], axis=0)
    gram = _dg(left, right, _NT, "gram")
    yield
    c2 = 2 * c
    a_ab = jnp.where(strict, gram[0:c2, 0:c2], 0.0)
    a_ak = jnp.where(strict, gram[0:c2, c2:2 * c2], 0.0)
    a_rb = jnp.where(incl, gram[c2:2 * c2, 0:c2], 0.0)
    a_rk = jnp.where(incl, gram[c2:2 * c2, c2:2 * c2], 0.0)
    vbd = _stack2(v, m0)
    e_end = jnp.exp(tot - cum)
    x = (_dg(_stack2(nkk * jnp.exp(cum_prev), m0), s, _NT, "init")
         + _dg(a_ak, vbd, _NN, "init"))
    ybd = _dg(_stack2(r * jnp.exp(cum), m0), s, _NT, "out") + _dg(a_rk, vbd, _NN, "out")
    s_new = s * jnp.exp(tot) + _dg(vbd, _stack2(kd * e_end, m0), _TN, "state")
    steps = max(1, (c - 1).bit_length())
    pw = a_ab
    pw_next = _dg(pw, pw, _NN, "square") if steps > 1 else None
    for it in range(steps):
        yield
        x = x + _dg(pw, x, _NN, "apply")
        if it < steps - 1:
            pw = pw_next
            if it < steps - 2:
                pw_next = _dg(pw, pw, _NN, "square")
    yield
    u = x
    ybd = ybd + _dg(a_rb, u, _NN, "out")
    y = ybd[0:c] + ybd[c:c2]
    s_new = s_new + _dg(u, _stack2(b * e_end, m0), _TN, "state")
    return y, s_new


def _lockstep(gens):
    results = [None] * len(gens)
    live = list(range(len(gens)))
    while live:
        still = []
        for i in live:
            try:
                next(gens[i])
                still.append(i)
            except StopIteration as stop:
                results[i] = stop.value
        live = still
    return results


def _rwkv_scan_kernel(lw0, kd0, b0, kkf, vf, rf, lw1, kd1, b1, kkb, vb, rb, s0_ref,
                      yf_o, yb_o, sfin_o, st):
    ci = pl.program_id(1)
    c = lw0.shape[2]
    npair = lw0.shape[1]

    @pl.when(ci == 0)
    def _():
        st[...] = s0_ref[:, 0]

    row = lax.broadcasted_iota(jnp.int32, (c, c), 0)
    colm = lax.broadcasted_iota(jnp.int32, (c, c), 1)
    tri_f = (colm <= row).astype(F32)
    tri_b = (colm >= row).astype(F32)
    r2 = lax.broadcasted_iota(jnp.int32, (2 * c, 2 * c), 0) & (c - 1)
    c2 = lax.broadcasted_iota(jnp.int32, (2 * c, 2 * c), 1) & (c - 1)
    m0 = lax.broadcasted_iota(jnp.int32, (1, LANES), 1) < HEAD_DIM

    fwd_refs = (lw0, kd0, b0, kkf, vf, rf)
    bwd_refs = (lw1, kd1, b1, kkb, vb, rb)
    for p0 in range(0, npair, SCAN_LOCKSTEP_PAIRS):
        gens, outs = [], []
        for p in range(p0, min(p0 + SCAN_LOCKSTEP_PAIRS, npair)):
            gens.append(_chunk_step([x.at[0, p] for x in fwd_refs], st.at[0, p], rev=False,
                                    tri=tri_f, strict=c2 < r2, incl=c2 <= r2, m0=m0))
            outs.append((yf_o, 0, p))
            gens.append(_chunk_step([x.at[0, p] for x in bwd_refs], st.at[1, p], rev=True,
                                    tri=tri_b, strict=c2 > r2, incl=c2 >= r2, m0=m0))
            outs.append((yb_o, 1, p))
        for (y_o, d, p), (y, s_new) in zip(outs, _lockstep(gens)):
            y_o[0, p] = y
            st[d, p] = s_new

    @pl.when(ci == pl.num_programs(1) - 1)
    def _():
        sfin_o[:, 0] = st[...]


def _rwkv_scan(prep, s0):
    lw0, lw1, kd0, kd1, b0, b1, kk, v, r = prep
    nb, npair, l, _ = lw0.shape
    c = SCAN_CHUNK
    n = l // c
    fwd = pl.BlockSpec((1, npair, c, LANES), lambda b, i: (b, 0, i, 0))
    bwd = pl.BlockSpec((1, npair, c, LANES), lambda b, i: (b, 0, n - 1 - i, 0))
    st_spec = pl.BlockSpec((2, 1, npair, LANES, LANES), lambda b, i: (0, b, 0, 0, 0))
    y_shape = jax.ShapeDtypeStruct((nb, npair, l, LANES), F32)
    return pl.pallas_call(
        _rwkv_scan_kernel,
        grid=(nb, n),
        in_specs=[fwd] * 6 + [bwd] * 6 + [st_spec],
        out_specs=[fwd, bwd, st_spec],
        out_shape=[y_shape, y_shape, jax.ShapeDtypeStruct(s0.shape, F32)],
        scratch_shapes=[pltpu.VMEM((2, npair, LANES, LANES), F32)],
        compiler_params=_params("arbitrary", "arbitrary"),
        name="rwkv_scan",
    )(lw0, kd0, b0, kk, v, r, lw1, kd1, b1, kk, v, r, s0)


def _rwkv_post_kernel(yf_ref, yb_ref, bv_ref, g_ref, lng_ref, lnb_ref, bd_ref, o_ref):
    npair = yf_ref.shape[1]
    y = jnp.concatenate([yf_ref[0, p] + yb_ref[0, p] for p in range(npair)], axis=-1)
    bd = bd_ref[...]
    inv = 1.0 / HEAD_DIM
    m = _segsum(y, bd) * inv
    dlt = y - m
    var = _segsum(dlt * dlt, bd) * inv
    yn = dlt * lax.rsqrt(var + RWKV_GN_EPS) * lng_ref[...] + lnb_ref[...]
    o_ref[0] = ((yn + bv_ref[0]) * g_ref[0]).astype(o_ref.dtype)


def _rwkv_post(yf, yb, bv, g, ln_g, ln_b, *, t_rows):
    nb, npair, l, _ = yf.shape
    w = npair * LANES
    pair_spec = pl.BlockSpec((1, npair, t_rows, LANES), lambda b, i: (b, 0, i, 0))
    wide_spec = pl.BlockSpec((1, t_rows, w), lambda b, i: (b, i, 0))
    const2 = lambda b, i: (0, 0)
    return pl.pallas_call(
        _rwkv_post_kernel,
        grid=(nb, l // t_rows),
        in_specs=[pair_spec, pair_spec, wide_spec, wide_spec,
                  pl.BlockSpec((1, w), const2), pl.BlockSpec((1, w), const2),
                  pl.BlockSpec((w, w), const2)],
        out_specs=wide_spec,
        out_shape=jax.ShapeDtypeStruct((nb, l, w), BF16),
        compiler_params=_params("arbitrary", "arbitrary"),
        name="rwkv_post",
    )(yf, yb, bv, g, ln_g.reshape(1, w), ln_b.reshape(1, w), _head_blockdiag(w))


def _merge_kernel(h_ref, mod_ref, zg_ref, yp_ref, yr_ref, yg_ref, pp_ref, pr_ref, pg_ref,
                  wo_ref, o_ref):
    d = h_ref.shape[2]
    merged = (_sigmoid(zg_ref[0, :, 0:d]) * jnp.dot(yp_ref[0], pp_ref[...], preferred_element_type=F32)
              + _sigmoid(zg_ref[0, :, d:2 * d]) * jnp.dot(yr_ref[0], pr_ref[...], preferred_element_type=F32)
              + _sigmoid(zg_ref[0, :, 2 * d:3 * d]) * jnp.dot(yg_ref[0], pg_ref[...], preferred_element_type=F32))
    m = jnp.dot(merged.astype(BF16), wo_ref[...], preferred_element_type=F32)
    o_ref[0] = h_ref[0] + mod_ref[0, 5:6, :] * m


def _merge(h, mod, zg, yp, yr, yg, pp, pr, pg, wo, *, tm):
    nb, l, d = h.shape
    per_batch = mod.shape[0] > 1
    mod_map = (lambda b, i: (b, 0, 0)) if per_batch else (lambda b, i: (0, 0, 0))
    row = lambda width: pl.BlockSpec((1, tm, width), lambda b, i: (b, i, 0))
    full = lambda a: pl.BlockSpec(a.shape, lambda b, i: (0, 0))
    return pl.pallas_call(
        _merge_kernel,
        grid=(nb, l // tm),
        in_specs=[row(d), pl.BlockSpec((1, N_MOD, d), mod_map), row(zg.shape[2]),
                  row(yp.shape[2]), row(yr.shape[2]), row(yg.shape[2]),
                  full(pp), full(pr), full(pg), full(wo)],
        out_specs=row(d),
        out_shape=jax.ShapeDtypeStruct((nb, l, d), F32),
        compiler_params=_params("arbitrary", "arbitrary"),
        name="merge",
    )(h, mod, zg, yp, yr, yg, pp, pr, pg, wo)


def _pick(n, pref):
    t = min(pref, n)
    while t > LANES and (n % t or t % LANES):
        t -= LANES
    return t if n % t == 0 else n


def kernel(x, c, ctx, c_ctx, ada_w, ada_b, norm_g, ffn_w_gate, ffn_w_up, ffn_w_down, w_in,
           pool_w, pool_scale, rwkv_mu, rwkv_w0, rwkv_w_up, rwkv_a0, rwkv_a_up, rwkv_g_up,
           rwkv_k_k, rwkv_k_a, rwkv_r_k, rwkv_ln_g, rwkv_ln_b, gmlp_ln_g, gmlp_ln_b, gmlp_ws,
           gmlp_bs, proj_pool, proj_rwkv, proj_gmlp, w_out, final_norm):
    depth = ada_w.shape[0]
    nb, l, d = x.shape
    lc = ctx.shape[1]
    width = rwkv_w0.shape[-1]
    pool_c = pool_scale.shape[-1]
    gmlp_c = gmlp_ln_g.shape[-1]
    rwkv_cols = rwkv_mu.shape[-1]
    off_rwkv = pool_c
    off_gmlp = off_rwkv + rwkv_cols
    off_gate = off_gmlp + 2 * gmlp_c
    npair = width // LANES

    cond8 = jnp.zeros((8, d), F32).at[:nb].set(c).at[nb].set(c_ctx)
    mod_all = _adaln(cond8, ada_w, ada_b).reshape(depth, 8, N_MOD, d)

    zero_state = jnp.zeros((2, nb, npair, LANES, LANES), F32)
    hc = ctx
    for li in range(depth):
        last = li == depth - 1
        mod_x = mod_all[li, :nb]
        mod_c = mod_all[li, nb:nb + 1]
        wg = ffn_w_gate[li].astype(BF16)
        wu = ffn_w_up[li].astype(BF16)
        wd = ffn_w_down[li].astype(BF16)
        win = w_in[li].astype(BF16)
        w_za = win[:, :off_rwkv]
        w_zb = win[:, off_rwkv:off_gmlp]
        w_zuv = win[:, off_gmlp:off_gate]
        w_zg = win[:, off_gate:]
        pp = proj_pool[li].astype(BF16)
        pr = proj_rwkv[li].astype(BF16)
        pg = proj_gmlp[li].astype(BF16)
        wo = w_out[li].astype(BF16)
        lp = {'mu': rwkv_mu[li], 'w0': rwkv_w0[li], 'w_up': rwkv_w_up[li], 'a0': rwkv_a0[li],
              'a_up': rwkv_a_up[li], 'g_up': rwkv_g_up[li], 'k_k': rwkv_k_k[li],
              'k_a': rwkv_k_a[li], 'r_k': rwkv_r_k[li]}
        tf = _pick(wg.shape[-1], 512)

        def mixer(h, mod, seq, grid_mode, s0, need_out):
            tm = _pick(seq, 512)
            zb = _norm_mm(h, mod, norm_g[li, 1], w_zb, tm=tm, tn=_pick(rwkv_cols, 1152))
            prep = _rwkv_prep(zb, lp, t_rows=_pick(seq, 128), grid_mode=grid_mode)
            yf, yb, s_fin = _rwkv_scan(prep[:9], s0)
            if not need_out:
                return None, s_fin
            y_rwkv = _rwkv_post(yf, yb, prep[10], prep[9], rwkv_ln_g[li], rwkv_ln_b[li],
                                t_rows=_pick(seq, 256))
            za = _norm_mm(h, mod, norm_g[li, 1], w_za, tm=tm, tn=_pick(pool_c, 512))
            y_pool = _pool(za, pool_w[li], pool_scale[li], t_rows=_pick(seq, 512))
            zuv = _norm_mm(h, mod, norm_g[li, 1], w_zuv, tm=tm, tn=_pick(2 * gmlp_c, 512))
            y_gmlp = _gmlp(zuv, gmlp_ln_g[li], gmlp_ln_b[li], gmlp_ws[li], gmlp_bs[li],
                           t_rows=_pick(seq, 512))
            zg = _norm_mm(h, mod, norm_g[li, 1], w_zg, tm=tm, tn=_pick(3 * d, 1024))
            out = _merge(h, mod, zg, y_pool, y_rwkv, y_gmlp, pp, pr, pg, wo, tm=_pick(seq, 256))
            return out, s_fin

        hc = _ffn(hc, mod_c, norm_g[li, 0], wg[0], wu[0], wd[0], final_norm,
                  base=0, final=False, tm=_pick(lc, 512), tf=tf)
        hc_new, s_ctx = mixer(hc, mod_c, lc, False, zero_state, not last)
        if not last:
            hc = _ffn(hc_new, mod_c, norm_g[li, 2], wg[1], wu[1], wd[1], final_norm,
                      base=6, final=False, tm=_pick(lc, 512), tf=tf)

        x = _ffn(x, mod_x, norm_g[li, 0], wg[0], wu[0], wd[0], final_norm,
                 base=0, final=False, tm=_pick(l, 512), tf=tf)
        x, _ = mixer(x, mod_x, l, True, s_ctx, True)
        x = _ffn(x, mod_x, norm_g[li, 2], wg[1], wu[1], wd[1], final_norm,
                 base=6, final=last, tm=_pick(l, 512), tf=tf)
    return x
```

```python
import functools
import math

import jax
import jax.numpy as jnp
from jax import lax
from jax.experimental import pallas as pl
from jax.experimental.pallas import tpu as pltpu

F32 = jnp.float32
BF16 = jnp.bfloat16
HIGHEST = lax.Precision.HIGHEST

NORM_EPS = 1e-6
LN_EPS = 1e-5
RWKV_GN_EPS = 64e-5
GRID_W = 64
N_MOD = 9
POOL_WINDOWS = (2, 4, 8, 16)
HEAD_DIM = 64
GMLP_CHUNK = 128
LANES = 128
SCAN_CHUNK = 64
VMEM_LIMIT = 56 * 1024 * 1024


def _params(*sem):
    return pltpu.CompilerParams(dimension_semantics=sem, vmem_limit_bytes=VMEM_LIMIT)


def _sigmoid(x):
    return 1.0 / (1.0 + jnp.exp(-x))


def _mm(a, b):
    return jnp.dot(a.astype(BF16), b.astype(BF16), preferred_element_type=F32)


def _mm_hi(a, b):
    return jnp.dot(a, b, precision=HIGHEST, preferred_element_type=F32)


def _mm_nt_hi(a, b):
    return lax.dot_general(a, b, (((1,), (1,)), ((), ())), precision=HIGHEST,
                           preferred_element_type=F32)


def _mm_tn_hi(a, b):
    return lax.dot_general(a, b, (((0,), (0,)), ((), ())), precision=HIGHEST,
                           preferred_element_type=F32)


def _rms(x, g):
    return x * lax.rsqrt(jnp.mean(x * x, axis=-1, keepdims=True) + NORM_EPS) * g


def _adaln_kernel(c_ref, w_ref, b_ref, o_ref):
    c = c_ref[...]
    o_ref[0] = _mm_hi(c * _sigmoid(c), w_ref[0]) + b_ref[0]


def _adaln(cond8, ada_w, ada_b):
    depth, d, n = ada_w.shape
    tn = 1024
    return pl.pallas_call(
        _adaln_kernel,
        grid=(depth, n // tn),
        in_specs=[pl.BlockSpec((8, d), lambda l, j: (0, 0)),
                  pl.BlockSpec((1, d, tn), lambda l, j: (l, 0, j)),
                  pl.BlockSpec((1, 1, tn), lambda l, j: (l, 0, j))],
        out_specs=pl.BlockSpec((1, 8, tn), lambda l, j: (l, 0, j)),
        out_shape=jax.ShapeDtypeStruct((depth, 8, n), F32),
        compiler_params=_params("arbitrary", "arbitrary"),
        name="adaln",
    )(cond8, ada_w, ada_b.reshape(depth, 1, n))


def _ffn_kernel(h_ref, mod_ref, g_ref, wg_ref, wu_ref, wd_ref, fn_ref, o_ref, hn_ref,
                *, base, final):
    j = pl.program_id(2)

    @pl.when(j == 0)
    def _():
        hn = _rms(h_ref[0], g_ref[...]) * (1.0 + mod_ref[0, base + 1:base + 2, :]) \
            + mod_ref[0, base:base + 1, :]
        hn_ref[...] = hn.astype(BF16)
        o_ref[0] = jnp.zeros(o_ref.shape[1:], F32)

    hn = hn_ref[...]
    gate = jnp.dot(hn, wg_ref[...], preferred_element_type=F32)
    up = jnp.dot(hn, wu_ref[...], preferred_element_type=F32)
    act = (gate * _sigmoid(gate) * up).astype(BF16)
    o_ref[0] += jnp.dot(act, wd_ref[...], preferred_element_type=F32)

    @pl.when(j == pl.num_programs(2) - 1)
    def _():
        out = h_ref[0] + 0.5 * mod_ref[0, base + 2:base + 3, :] * o_ref[0]
        if final:
            out = _rms(out, fn_ref[...])
        o_ref[0] = out


def _ffn(h, mod, g, wg, wu, wd, fn, *, base, final, tm, tf):
    nb, l, d = h.shape
    f = wg.shape[1]
    per_batch = mod.shape[0] > 1
    mod_map = (lambda b, i, j: (b, 0, 0)) if per_batch else (lambda b, i, j: (0, 0, 0))
    return pl.pallas_call(
        functools.partial(_ffn_kernel, base=base, final=final),
        grid=(nb, l // tm, f // tf),
        in_specs=[pl.BlockSpec((1, tm, d), lambda b, i, j: (b, i, 0)),
                  pl.BlockSpec((1, N_MOD, d), mod_map),
                  pl.BlockSpec((1, d), lambda b, i, j: (0, 0)),
                  pl.BlockSpec((d, tf), lambda b, i, j: (0, j)),
                  pl.BlockSpec((d, tf), lambda b, i, j: (0, j)),
                  pl.BlockSpec((tf, d), lambda b, i, j: (j, 0)),
                  pl.BlockSpec((1, d), lambda b, i, j: (0, 0))],
        out_specs=pl.BlockSpec((1, tm, d), lambda b, i, j: (b, i, 0)),
        out_shape=jax.ShapeDtypeStruct((nb, l, d), F32),
        scratch_shapes=[pltpu.VMEM((tm, d), BF16)],
        compiler_params=_params("arbitrary", "arbitrary", "arbitrary"),
        name="ffn",
    )(h, mod, g.reshape(1, d), wg, wu, wd, fn.reshape(1, d))


def _norm_mm_kernel(h_ref, mod_ref, g_ref, w_ref, o_ref, hn_ref):
    @pl.when(pl.program_id(2) == 0)
    def _():
        hn = _rms(h_ref[0], g_ref[...]) * (1.0 + mod_ref[0, 4:5, :]) + mod_ref[0, 3:4, :]
        hn_ref[...] = hn.astype(BF16)

    o_ref[0] = jnp.dot(hn_ref[...], w_ref[...], preferred_element_type=F32)


def _norm_mm(h, mod, g, w, *, tm, tn):
    nb, l, d = h.shape
    n = w.shape[1]
    per_batch = mod.shape[0] > 1
    mod_map = (lambda b, i, j: (b, 0, 0)) if per_batch else (lambda b, i, j: (0, 0, 0))
    return pl.pallas_call(
        _norm_mm_kernel,
        grid=(nb, l // tm, n // tn),
        in_specs=[pl.BlockSpec((1, tm, d), lambda b, i, j: (b, i, 0)),
                  pl.BlockSpec((1, N_MOD, d), mod_map),
                  pl.BlockSpec((1, d), lambda b, i, j: (0, 0)),
                  pl.BlockSpec((d, tn), lambda b, i, j: (0, j))],
        out_specs=pl.BlockSpec((1, tm, tn), lambda b, i, j: (b, i, j)),
        out_shape=jax.ShapeDtypeStruct((nb, l, n), F32),
        scratch_shapes=[pltpu.VMEM((tm, d), BF16)],
        compiler_params=_params("arbitrary", "arbitrary", "arbitrary"),
        name="norm_mm",
    )(h, mod, g.reshape(1, d), w)


POOL_HALO = 8


def _pool_kernel(prev_ref, x_ref, next_ref, pw_ref, ps_ref, o_ref, scr, *, t_rows, seq):
    i = pl.program_id(1)
    last = pl.num_programs(1) - 1
    hal = POOL_HALO
    scr[0:hal] = jnp.where(i > 0, prev_ref[0], 0.0)
    scr[hal:hal + t_rows] = x_ref[0]
    scr[hal + t_rows:2 * hal + t_rows] = jnp.where(i < last, next_ref[0], 0.0)
    t = i * t_rows + lax.broadcasted_iota(jnp.int32, (t_rows, LANES), 0)
    for gi, w in enumerate(POOL_WINDOWS):
        c0 = gi * LANES
        s = scr[hal - w // 2:hal - w // 2 + t_rows, c0:c0 + LANES]
        for dlt in range(-(w // 2) + 1, w // 2):
            s = s + scr[hal + dlt:hal + dlt + t_rows, c0:c0 + LANES]
        lo = jnp.clip(t - w // 2, 0, seq)
        hi = jnp.clip(t - w // 2 + w, 0, seq)
        cnt = (hi - lo).astype(F32)
        p = s / cnt - scr[hal:hal + t_rows, c0:c0 + LANES]
        y = _mm(p, pw_ref[gi]) * ps_ref[:, c0:c0 + LANES]
        o_ref[0, :, c0:c0 + LANES] = y.astype(o_ref.dtype)


def _pool(za, pool_w, pool_scale, *, t_rows):
    nb, l, c = za.shape
    hal = POOL_HALO
    r = t_rows // hal
    nh = l // hal
    return pl.pallas_call(
        functools.partial(_pool_kernel, t_rows=t_rows, seq=l),
        grid=(nb, l // t_rows),
        in_specs=[pl.BlockSpec((1, hal, c), lambda b, i: (b, jnp.maximum(i * r - 1, 0), 0)),
                  pl.BlockSpec((1, t_rows, c), lambda b, i: (b, i, 0)),
                  pl.BlockSpec((1, hal, c), lambda b, i: (b, jnp.minimum((i + 1) * r, nh - 1), 0)),
                  pl.BlockSpec(pool_w.shape, lambda b, i: (0, 0, 0)),
                  pl.BlockSpec((1, c), lambda b, i: (0, 0))],
        out_specs=pl.BlockSpec((1, t_rows, c), lambda b, i: (b, i, 0)),
        out_shape=jax.ShapeDtypeStruct((nb, l, c), BF16),
        scratch_shapes=[pltpu.VMEM((t_rows + 2 * hal, c), F32)],
        compiler_params=_params("arbitrary", "arbitrary"),
        name="pool",
    )(za, za, za, pool_w.astype(BF16), pool_scale.reshape(1, c))


def _gelu(x):
    return x * (0.5 * (1.0 + jnp.tanh(math.sqrt(2.0 / math.pi) * (x + 0.044715 * (x * x * x)))))


def _gmlp_kernel(z_ref, lng_ref, lnb_ref, ws_ref, bsb_ref, o_ref, *, t_rows):
    width = o_ref.shape[2]
    groups = ws_ref.shape[0]
    gd = width // groups
    for ci in range(t_rows // GMLP_CHUNK):
        rows = slice(ci * GMLP_CHUNK, (ci + 1) * GMLP_CHUNK)
        u = _gelu(z_ref[0, rows, 0:width])
        v = _gelu(z_ref[0, rows, width:2 * width])
        mu = jnp.mean(v, axis=-1, keepdims=True)
        var = jnp.mean(jnp.square(v - mu), axis=-1, keepdims=True)
        vn = ((v - mu) * lax.rsqrt(var + LN_EPS)) * lng_ref[...] + lnb_ref[...]
        for g in range(groups):
            cols = slice(g * gd, (g + 1) * gd)
            s = _mm(ws_ref[g], vn[:, cols]) + bsb_ref[g]
            o_ref[0, rows, cols] = (u[:, cols] * s).astype(o_ref.dtype)


def _gmlp(zuv, ln_g, ln_b, ws, bs, *, t_rows):
    nb, l, c2 = zuv.shape
    c = c2 // 2
    groups = ws.shape[0]
    bsb = jnp.broadcast_to(bs[:, :, None], (groups, GMLP_CHUNK, c // groups))
    return pl.pallas_call(
        functools.partial(_gmlp_kernel, t_rows=t_rows),
        grid=(nb, l // t_rows),
        in_specs=[pl.BlockSpec((1, t_rows, c2), lambda b, i: (b, i, 0)),
                  pl.BlockSpec((1, c), lambda b, i: (0, 0)),
                  pl.BlockSpec((1, c), lambda b, i: (0, 0)),
                  pl.BlockSpec(ws.shape, lambda b, i: (0, 0, 0)),
                  pl.BlockSpec(bsb.shape, lambda b, i: (0, 0, 0))],
        out_specs=pl.BlockSpec((1, t_rows, c), lambda b, i: (b, i, 0)),
        out_shape=jax.ShapeDtypeStruct((nb, l, c), BF16),
        compiler_params=_params("arbitrary", "arbitrary"),
        name="gmlp",
    )(zuv, ln_g.reshape(1, c), ln_b.reshape(1, c), ws.astype(BF16), bsb)


def _segsum(x, bd):
    hi = x.astype(BF16)
    lo = (x - hi.astype(F32)).astype(BF16)
    out = []
    for p in range(x.shape[-1] // LANES):
        cols = slice(p * LANES, (p + 1) * LANES)
        out.append(jnp.dot(hi[:, cols], bd, preferred_element_type=F32)
                   + jnp.dot(lo[:, cols], bd, preferred_element_type=F32))
    return jnp.concatenate(out, axis=-1)


def _rwkv_prep_kernel(prev_ref, x_ref, next_ref, mu_ref, w0_ref, wup_ref, a0_ref, aup_ref,
                      gup_ref, kk_ref, ka_ref, rk_ref, bd_ref,
                      lw0_o, lw1_o, kd0_o, kd1_o, b0_o, b1_o, kk_o, v_o, r_o, g_o, bv_o,
                      scr, *, t_rows, grid_mode, width):
    i = pl.program_id(1)
    last = pl.num_programs(1) - 1
    hal = GRID_W
    scr[0:hal] = jnp.where(i > 0, prev_ref[0], 0.0)
    scr[hal:hal + t_rows] = x_ref[0]
    scr[hal + t_rows:2 * hal + t_rows] = jnp.where(i < last, next_ref[0], 0.0)
    cols = scr.shape[1]
    if grid_mode:
        q = cols // 4
        bounds = (0, q, 2 * q, 3 * q, cols)
        offs = (-1, 1, -GRID_W, GRID_W)
    else:
        bounds = (0, cols // 2, cols)
        offs = (-1, 1)

    def zs_cols(c0, c1):
        n = c1 - c0
        x = scr[hal:hal + t_rows, c0:c1]
        ch = c0 + lax.broadcasted_iota(jnp.int32, (t_rows, n), 1)
        col = lax.broadcasted_iota(jnp.int32, (t_rows, n), 0) & (GRID_W - 1)
        shifted = None
        for qi, off in enumerate(offs):
            lo, hi = bounds[qi], bounds[qi + 1]
            if hi <= c0 or lo >= c1:
                continue
            src = scr[hal + off:hal + off + t_rows, c0:c1]
            if grid_mode and off == -1:
                src = jnp.where(col == 0, 0.0, src)
            if grid_mode and off == 1:
                src = jnp.where(col == GRID_W - 1, 0.0, src)
            shifted = src if shifted is None else jnp.where(ch >= lo, src, shifted)
        return x + (shifted - x) * mu_ref[:, c0:c1]

    w = width
    r = zs_cols(0, w)
    k = zs_cols(w, 2 * w)
    v = zs_cols(2 * w, 3 * w)
    rest = zs_cols(3 * w, cols)
    wd = jnp.tanh(rest[:, 0:LANES])
    ad = rest[:, LANES:2 * LANES]
    gd = _sigmoid(rest[:, 2 * LANES:3 * LANES])

    bd = bd_ref[...]
    kk0 = k * kk_ref[...]
    ss = _segsum(kk0 * kk0, bd)
    kk = kk0 / jnp.maximum(jnp.sqrt(ss), 1e-12)
    npair = w // LANES

    def put(o_ref, val):
        for p in range(npair):
            o_ref[0, p] = val[:, p * LANES:(p + 1) * LANES]

    put(kk_o, kk)
    put(v_o, v)
    put(r_o, r)
    kd_sum = None
    for d, (lw_o, kd_o, b_o) in enumerate(((lw0_o, kd0_o, b0_o), (lw1_o, kd1_o, b1_o))):
        w_pre = w0_ref[d] + _mm(wd, wup_ref[d])
        put(lw_o, -math.exp(-0.5) * _sigmoid(w_pre))
        a = _sigmoid(a0_ref[d] + _mm(ad, aup_ref[d]))
        kd = k * (1.0 + (a - 1.0) * ka_ref[...])
        put(kd_o, kd)
        put(b_o, kk * a)
        kd_sum = kd if kd_sum is None else kd_sum + kd
    g_o[0] = _mm(gd, gup_ref[...])
    bv_o[0] = _segsum(r * kd_sum * rk_ref[...], bd) * v


def _head_blockdiag(width):
    idx = jnp.arange(width) // HEAD_DIM
    return (idx[:, None] == idx[None, :]).astype(BF16)


def _rwkv_prep(zb, lp, *, t_rows, grid_mode):
    nb, l, cols = zb.shape
    w = lp['w0'].shape[-1]
    npair = w // LANES
    hal = GRID_W
    r = t_rows // hal
    nh = l // hal
    lora = lp['w_up'].shape[1]

    def pad_dir(up):
        z = jnp.zeros_like(up[0])
        return jnp.stack([jnp.concatenate([up[0], z], 0), jnp.concatenate([z, up[1]], 0)], 0)

    const2 = lambda b, i: (0, 0)
    const3 = lambda b, i: (0, 0, 0)
    pair_spec = pl.BlockSpec((1, npair, t_rows, LANES), lambda b, i: (b, 0, i, 0))
    wide_spec = pl.BlockSpec((1, t_rows, w), lambda b, i: (b, i, 0))
    pair_shape = jax.ShapeDtypeStruct((nb, npair, l, LANES), F32)
    wide_shape = jax.ShapeDtypeStruct((nb, l, w), F32)
    return pl.pallas_call(
        functools.partial(_rwkv_prep_kernel, t_rows=t_rows, grid_mode=grid_mode, width=w),
        grid=(nb, l // t_rows),
        in_specs=[pl.BlockSpec((1, hal, cols), lambda b, i: (b, jnp.maximum(i * r - 1, 0), 0)),
                  pl.BlockSpec((1, t_rows, cols), lambda b, i: (b, i, 0)),
                  pl.BlockSpec((1, hal, cols), lambda b, i: (b, jnp.minimum((i + 1) * r, nh - 1), 0)),
                  pl.BlockSpec((1, cols), const2),
                  pl.BlockSpec((2, 1, w), const3),
                  pl.BlockSpec((2, 2 * lora, w), const3),
                  pl.BlockSpec((2, 1, w), const3),
                  pl.BlockSpec((2, 2 * lora, w), const3),
                  pl.BlockSpec(lp['g_up'].shape, const2),
                  pl.BlockSpec((1, w), const2),
                  pl.BlockSpec((1, w), const2),
                  pl.BlockSpec((1, w), const2),
                  pl.BlockSpec((LANES, LANES), const2)],
        out_specs=[pair_spec] * 9 + [wide_spec] * 2,
        out_shape=[pair_shape] * 9 + [wide_shape] * 2,
        scratch_shapes=[pltpu.VMEM((t_rows + 2 * hal, cols), F32)],
        compiler_params=_params("arbitrary", "arbitrary"),
        name="rwkv_prep",
    )(zb, zb, zb, lp['mu'].reshape(1, cols), lp['w0'].reshape(2, 1, w),
      pad_dir(lp['w_up']).astype(BF16), lp['a0'].reshape(2, 1, w),
      pad_dir(lp['a_up']).astype(BF16), lp['g_up'].astype(BF16), lp['k_k'].reshape(1, w),
      lp['k_a'].reshape(1, w), lp['r_k'].reshape(1, w), _head_blockdiag(LANES))


def _stack2(x, m0):
    return jnp.concatenate([jnp.where(m0, x, 0.0), jnp.where(m0, 0.0, x)], axis=0)


_NN = (((1,), (0,)), ((), ()))
_NT = (((1,), (1,)), ((), ()))
_TN = (((0,), (0,)), ((), ()))

SCAN_PREC = {"cum": "rhs2", "gram": "bf16", "init": "bf16", "square": "bf16", "apply": "bf16",
             "out": "bf16", "state": "bf16"}
SCAN_LOCKSTEP_PAIRS = 8

def _split(a):
    hi = a.astype(BF16)
    return hi, (a - hi.astype(F32)).astype(BF16)


def _dg(a, b, dims, site):
    mode = SCAN_PREC[site]
    if mode == "f32":
        return lax.dot_general(a, b, dims, precision=HIGHEST, preferred_element_type=F32)
    if mode == "bf16":
        return lax.dot_general(a.astype(BF16), b.astype(BF16), dims, preferred_element_type=F32)
    b_hi, b_lo = _split(b)
    if mode == "rhs2":
        a16 = a.astype(BF16)
        return (lax.dot_general(a16, b_hi, dims, preferred_element_type=F32)
                + lax.dot_general(a16, b_lo, dims, preferred_element_type=F32))
    a_hi, a_lo = _split(a)
    return (lax.dot_general(a_hi, b_hi, dims, preferred_element_type=F32)
            + lax.dot_general(a_hi, b_lo, dims, preferred_element_type=F32)
            + lax.dot_general(a_lo, b_hi, dims, preferred_element_type=F32))


def _chunk_step(refs, s_ref, *, rev, tri, strict, incl, m0):
    lw, kd, b, kk, v, r = (x[...] for x in refs)
    s = s_ref[...]
    c = lw.shape[0]
    cum = _dg(tri, lw, _NN, "cum")
    yield
    cum_prev = cum - lw
    end = 0 if rev else c - 1
    tot = cum[end:end + 1, :]
    mid = cum[c // 2:c // 2 + 1, :]
    e_inv = jnp.exp(mid - cum)
    nkk = -kk
    left = jnp.concatenate([_stack2(nkk * jnp.exp(cum_prev - mid), m0),
                            _stack2(r * jnp.exp(cum - mid), m0)], axis=0)
    right = jnp.concatenate([_stack2(b * e_inv, m0), _stack2(kd * e_inv, m0)], axis=0)
    gram = _dg(left, right, _NT, "gram")
    yield
    c2 = 2 * c
    a_ab = jnp.where(strict, gram[0:c2, 0:c2], 0.0)
    a_ak = jnp.where(strict, gram[0:c2, c2:2 * c2], 0.0)
    a_rb = jnp.where(incl, gram[c2:2 * c2, 0:c2], 0.0)
    a_rk = jnp.where(incl, gram[c2:2 * c2, c2:2 * c2], 0.0)
    vbd = _stack2(v, m0)
    e_end = jnp.exp(tot - cum)
    x = (_dg(_stack2(nkk * jnp.exp(cum_prev), m0), s, _NT, "init")
         + _dg(a_ak, vbd, _NN, "init"))
    ybd = _dg(_stack2(r * jnp.exp(cum), m0), s, _NT, "out") + _dg(a_rk, vbd, _NN, "out")
    s_new = s * jnp.exp(tot) + _dg(vbd, _stack2(kd * e_end, m0), _TN, "state")
    steps = max(1, (c - 1).bit_length())
    pw = a_ab
    pw_next = _dg(pw, pw, _NN, "square") if steps > 1 else None
    for it in range(steps):
        yield
        x = x + _dg(pw, x, _NN, "apply")
        if it < steps - 1:
            pw = pw_next
            if it < steps - 2:
                pw_next = _dg(pw, pw, _NN, "square")
    yield
    u = x
    ybd = ybd + _dg(a_rb, u, _NN, "out")
    y = ybd[0:c] + ybd[c:c2]
    s_new = s_new + _dg(u, _stack2(b * e_end, m0), _TN, "state")
    return y, s_new


def _lockstep(gens):
    results = [None] * len(gens)
    live = list(range(len(gens)))
    while live:
        still = []
        for i in live:
            try:
                next(gens[i])
                still.append(i)
            except StopIteration as stop:
                results[i] = stop.value
        live = still
    return results


def _rwkv_scan_kernel(lw0, kd0, b0, kkf, vf, rf, lw1, kd1, b1, kkb, vb, rb, s0_ref,
                      yf_o, yb_o, sfin_o, st):
    ci = pl.program_id(1)
    c = lw0.shape[2]
    npair = lw0.shape[1]

    @pl.when(ci == 0)
    def _():
        st[...] = s0_ref[:, 0]

    row = lax.broadcasted_iota(jnp.int32, (c, c), 0)
    colm = lax.broadcasted_iota(jnp.int32, (c, c), 1)
    tri_f = (colm <= row).astype(F32)
    tri_b = (colm >= row).astype(F32)
    r2 = lax.broadcasted_iota(jnp.int32, (2 * c, 2 * c), 0) & (c - 1)
    c2 = lax.broadcasted_iota(jnp.int32, (2 * c, 2 * c), 1) & (c - 1)
    m0 = lax.broadcasted_iota(jnp.int32, (1, LANES), 1) < HEAD_DIM

    fwd_refs = (lw0, kd0, b0, kkf, vf, rf)
    bwd_refs = (lw1, kd1, b1, kkb, vb, rb)
    for p0 in range(0, npair, SCAN_LOCKSTEP_PAIRS):
        gens, outs = [], []
        for p in range(p0, min(p0 + SCAN_LOCKSTEP_PAIRS, npair)):
            gens.append(_chunk_step([x.at[0, p] for x in fwd_refs], st.at[0, p], rev=False,
                                    tri=tri_f, strict=c2 < r2, incl=c2 <= r2, m0=m0))
            outs.append((yf_o, 0, p))
            gens.append(_chunk_step([x.at[0, p] for x in bwd_refs], st.at[1, p], rev=True,
                                    tri=tri_b, strict=c2 > r2, incl=c2 >= r2, m0=m0))
            outs.append((yb_o, 1, p))
        for (y_o, d, p), (y, s_new) in zip(outs, _lockstep(gens)):
            y_o[0, p] = y
            st[d, p] = s_new

    @pl.when(ci == pl.num_programs(1) - 1)
    def _():
        sfin_o[:, 0] = st[...]


def _rwkv_scan(prep, s0):
    lw0, lw1, kd0, kd1, b0, b1, kk, v, r = prep
    nb, npair, l, _ = lw0.shape
    c = SCAN_CHUNK
    n = l // c
    fwd = pl.BlockSpec((1, npair, c, LANES), lambda b, i: (b, 0, i, 0))
    bwd = pl.BlockSpec((1, npair, c, LANES), lambda b, i: (b, 0, n - 1 - i, 0))
    st_spec = pl.BlockSpec((2, 1, npair, LANES, LANES), lambda b, i: (0, b, 0, 0, 0))
    y_shape = jax.ShapeDtypeStruct((nb, npair, l, LANES), F32)
    return pl.pallas_call(
        _rwkv_scan_kernel,
        grid=(nb, n),
        in_specs=[fwd] * 6 + [bwd] * 6 + [st_spec],
        out_specs=[fwd, bwd, st_spec],
        out_shape=[y_shape, y_shape, jax.ShapeDtypeStruct(s0.shape, F32)],
        scratch_shapes=[pltpu.VMEM((2, npair, LANES, LANES), F32)],
        compiler_params=_params("arbitrary", "arbitrary"),
        name="rwkv_scan",
    )(lw0, kd0, b0, kk, v, r, lw1, kd1, b1, kk, v, r, s0)


def _rwkv_post_kernel(yf_ref, yb_ref, bv_ref, g_ref, lng_ref, lnb_ref, bd_ref, o_ref):
    npair = yf_ref.shape[1]
    y = jnp.concatenate([yf_ref[0, p] + yb_ref[0, p] for p in range(npair)], axis=-1)
    bd = bd_ref[...]
    inv = 1.0 / HEAD_DIM
    m = _segsum(y, bd) * inv
    dlt = y - m
    var = _segsum(dlt * dlt, bd) * inv
    yn = dlt * lax.rsqrt(var + RWKV_GN_EPS) * lng_ref[...] + lnb_ref[...]
    o_ref[0] = ((yn + bv_ref[0]) * g_ref[0]).astype(o_ref.dtype)


def _rwkv_post(yf, yb, bv, g, ln_g, ln_b, *, t_rows):
    nb, npair, l, _ = yf.shape
    w = npair * LANES
    pair_spec = pl.BlockSpec((1, npair, t_rows, LANES), lambda b, i: (b, 0, i, 0))
    wide_spec = pl.BlockSpec((1, t_rows, w), lambda b, i: (b, i, 0))
    const2 = lambda b, i: (0, 0)
    return pl.pallas_call(
        _rwkv_post_kernel,
        grid=(nb, l // t_rows),
        in_specs=[pair_spec, pair_spec, wide_spec, wide_spec,
                  pl.BlockSpec((1, w), const2), pl.BlockSpec((1, w), const2),
                  pl.BlockSpec((LANES, LANES), const2)],
        out_specs=wide_spec,
        out_shape=jax.ShapeDtypeStruct((nb, l, w), BF16),
        compiler_params=_params("arbitrary", "arbitrary"),
        name="rwkv_post",
    )(yf, yb, bv, g, ln_g.reshape(1, w), ln_b.reshape(1, w), _head_blockdiag(LANES))


def _merge_kernel(h_ref, mod_ref, g_ref, yp_ref, yr_ref, yg_ref, wzg_ref, pp_ref, pr_ref, pg_ref,
                  wo_ref, o_ref, hn_ref):
    n = pl.program_id(2)

    @pl.when(n == 0)
    def _():
        hn = _rms(h_ref[0], g_ref[...]) * (1.0 + mod_ref[0, 4:5, :]) + mod_ref[0, 3:4, :]
        hn_ref[...] = hn.astype(BF16)
        o_ref[0] = jnp.zeros(o_ref.shape[1:], F32)

    hn = hn_ref[...]
    merged = None
    for br, (y_ref, p_ref) in enumerate(((yp_ref, pp_ref), (yr_ref, pr_ref), (yg_ref, pg_ref))):
        gate = _sigmoid(jnp.dot(hn, wzg_ref[br], preferred_element_type=F32))
        term = gate * jnp.dot(y_ref[0], p_ref[...], preferred_element_type=F32)
        merged = term if merged is None else merged + term
    o_ref[0] += jnp.dot(merged.astype(BF16), wo_ref[...], preferred_element_type=F32)

    @pl.when(n == pl.num_programs(2) - 1)
    def _():
        o_ref[0] = h_ref[0] + mod_ref[0, 5:6, :] * o_ref[0]


def _merge(h, mod, g, yp, yr, yg, wzg, pp, pr, pg, wo, *, tm, tn):
    nb, l, d = h.shape
    per_batch = mod.shape[0] > 1
    mod_map = (lambda b, i, n: (b, 0, 0)) if per_batch else (lambda b, i, n: (0, 0, 0))
    row = lambda width: pl.BlockSpec((1, tm, width), lambda b, i, n: (b, i, 0))
    col = lambda a: pl.BlockSpec((a.shape[0], tn), lambda b, i, n: (0, n))
    return pl.pallas_call(
        _merge_kernel,
        grid=(nb, l // tm, d // tn),
        in_specs=[row(d), pl.BlockSpec((1, N_MOD, d), mod_map),
                  pl.BlockSpec((1, d), lambda b, i, n: (0, 0)),
                  row(yp.shape[2]), row(yr.shape[2]), row(yg.shape[2]),
                  pl.BlockSpec((wzg.shape[0], d, tn), lambda b, i, n: (0, 0, n)),
                  col(pp), col(pr), col(pg),
                  pl.BlockSpec((tn, d), lambda b, i, n: (n, 0))],
        out_specs=row(d),
        out_shape=jax.ShapeDtypeStruct((nb, l, d), F32),
        scratch_shapes=[pltpu.VMEM((tm, d), BF16)],
        compiler_params=_params("arbitrary", "arbitrary", "arbitrary"),
        name="merge",
    )(h, mod, g.reshape(1, d), yp, yr, yg, wzg, pp, pr, pg, wo)


def _pick(n, pref):
    t = min(pref, n)
    while t > LANES and (n % t or t % LANES):
        t -= LANES
    return t if n % t == 0 else n


def kernel(x, c, ctx, c_ctx, ada_w, ada_b, norm_g, ffn_w_gate, ffn_w_up, ffn_w_down, w_in,
           pool_w, pool_scale, rwkv_mu, rwkv_w0, rwkv_w_up, rwkv_a0, rwkv_a_up, rwkv_g_up,
           rwkv_k_k, rwkv_k_a, rwkv_r_k, rwkv_ln_g, rwkv_ln_b, gmlp_ln_g, gmlp_ln_b, gmlp_ws,
           gmlp_bs, proj_pool, proj_rwkv, proj_gmlp, w_out, final_norm):
    depth = ada_w.shape[0]
    nb, l, d = x.shape
    lc = ctx.shape[1]
    width = rwkv_w0.shape[-1]
    pool_c = pool_scale.shape[-1]
    gmlp_c = gmlp_ln_g.shape[-1]
    rwkv_cols = rwkv_mu.shape[-1]
    off_rwkv = pool_c
    off_gmlp = off_rwkv + rwkv_cols
    off_gate = off_gmlp + 2 * gmlp_c
    npair = width // LANES

    cond8 = jnp.zeros((8, d), F32).at[:nb].set(c).at[nb].set(c_ctx)
    mod_all = _adaln(cond8, ada_w, ada_b).reshape(depth, 8, N_MOD, d)

    zero_state = jnp.zeros((2, nb, npair, LANES, LANES), F32)
    hc = ctx
    for li in range(depth):
        last = li == depth - 1
        mod_x = mod_all[li, :nb]
        mod_c = mod_all[li, nb:nb + 1]
        wg = ffn_w_gate[li].astype(BF16)
        wu = ffn_w_up[li].astype(BF16)
        wd = ffn_w_down[li].astype(BF16)
        win = w_in[li].astype(BF16)
        w_za = win[:, :off_rwkv]
        w_zb = win[:, off_rwkv:off_gmlp]
        w_zuv = win[:, off_gmlp:off_gate]
        w_zg = win[:, off_gate:].reshape(d, -1, d).transpose(1, 0, 2)
        pp = proj_pool[li].astype(BF16)
        pr = proj_rwkv[li].astype(BF16)
        pg = proj_gmlp[li].astype(BF16)
        wo = w_out[li].astype(BF16)
        lp = {'mu': rwkv_mu[li], 'w0': rwkv_w0[li], 'w_up': rwkv_w_up[li], 'a0': rwkv_a0[li],
              'a_up': rwkv_a_up[li], 'g_up': rwkv_g_up[li], 'k_k': rwkv_k_k[li],
              'k_a': rwkv_k_a[li], 'r_k': rwkv_r_k[li]}
        tf = _pick(wg.shape[-1], 512)

        def mixer(h, mod, seq, grid_mode, s0, need_out):
            tm = _pick(seq, 1024)
            zb = _norm_mm(h, mod, norm_g[li, 1], w_zb, tm=tm, tn=_pick(rwkv_cols, 1152))
            prep = _rwkv_prep(zb, lp, t_rows=_pick(seq, 256), grid_mode=grid_mode)
            yf, yb, s_fin = _rwkv_scan(prep[:9], s0)
            if not need_out:
                return None, s_fin
            y_rwkv = _rwkv_post(yf, yb, prep[10], prep[9], rwkv_ln_g[li], rwkv_ln_b[li],
                                t_rows=_pick(seq, 256))
            za = _norm_mm(h, mod, norm_g[li, 1], w_za, tm=tm, tn=_pick(pool_c, 512))
            y_pool = _pool(za, pool_w[li], pool_scale[li], t_rows=_pick(seq, 512))
            zuv = _norm_mm(h, mod, norm_g[li, 1], w_zuv, tm=tm, tn=_pick(2 * gmlp_c, 512))
            y_gmlp = _gmlp(zuv, gmlp_ln_g[li], gmlp_ln_b[li], gmlp_ws[li], gmlp_bs[li],
                           t_rows=_pick(seq, 512))
            out = _merge(h, mod, norm_g[li, 1], y_pool, y_rwkv, y_gmlp, w_zg, pp, pr, pg, wo,
                         tm=_pick(seq, 512), tn=_pick(d, 512))
            return out, s_fin

        hc = _ffn(hc, mod_c, norm_g[li, 0], wg[0], wu[0], wd[0], final_norm,
                  base=0, final=False, tm=_pick(lc, 512), tf=tf)
        hc_new, s_ctx = mixer(hc, mod_c, lc, False, zero_state, not last)
        if not last:
            hc = _ffn(hc_new, mod_c, norm_g[li, 2], wg[1], wu[1], wd[1], final_norm,
                      base=6, final=False, tm=_pick(lc, 512), tf=tf)

        x = _ffn(x, mod_x, norm_g[li, 0], wg[0], wu[0], wd[0], final_norm,
                 base=0, final=False, tm=_pick(l, 512), tf=tf)
        x, _ = mixer(x, mod_x, l, True, s_ctx, True)
        x = _ffn(x, mod_x, norm_g[li, 2], wg[1], wu[1], wd[1], final_norm,
                 base=6, final=last, tm=_pick(l, 512), tf=tf)
    return x
```

```python
import functools
import math

import jax
import jax.numpy as jnp
from jax import lax
from jax.experimental import pallas as pl
from jax.experimental.pallas import tpu as pltpu

F32 = jnp.float32
BF16 = jnp.bfloat16
HIGHEST = lax.Precision.HIGHEST

NORM_EPS = 1e-6
LN_EPS = 1e-5
RWKV_GN_EPS = 64e-5
GRID_W = 64
N_MOD = 9
POOL_WINDOWS = (2, 4, 8, 16)
HEAD_DIM = 64
GMLP_CHUNK = 128
LANES = 128
SCAN_CHUNK = 64
VMEM_LIMIT = 56 * 1024 * 1024


def _params(*sem):
    return pltpu.CompilerParams(dimension_semantics=sem, vmem_limit_bytes=VMEM_LIMIT)


def _sigmoid(x):
    return 1.0 / (1.0 + jnp.exp(-x))


def _mm(a, b):
    return jnp.dot(a.astype(BF16), b.astype(BF16), preferred_element_type=F32)


def _mm_hi(a, b):
    return jnp.dot(a, b, precision=HIGHEST, preferred_element_type=F32)


def _mm_nt_hi(a, b):
    return lax.dot_general(a, b, (((1,), (1,)), ((), ())), precision=HIGHEST,
                           preferred_element_type=F32)


def _mm_tn_hi(a, b):
    return lax.dot_general(a, b, (((0,), (0,)), ((), ())), precision=HIGHEST,
                           preferred_element_type=F32)


def _rms(x, g):
    return x * lax.rsqrt(jnp.mean(x * x, axis=-1, keepdims=True) + NORM_EPS) * g


def _adaln_kernel(c_ref, w_ref, b_ref, o_ref):
    c = c_ref[...]
    o_ref[0] = _mm_hi(c * _sigmoid(c), w_ref[0]) + b_ref[0]


def _adaln(cond8, ada_w, ada_b):
    depth, d, n = ada_w.shape
    tn = 1024
    return pl.pallas_call(
        _adaln_kernel,
        grid=(depth, n // tn),
        in_specs=[pl.BlockSpec((8, d), lambda l, j: (0, 0)),
                  pl.BlockSpec((1, d, tn), lambda l, j: (l, 0, j)),
                  pl.BlockSpec((1, 1, tn), lambda l, j: (l, 0, j))],
        out_specs=pl.BlockSpec((1, 8, tn), lambda l, j: (l, 0, j)),
        out_shape=jax.ShapeDtypeStruct((depth, 8, n), F32),
        compiler_params=_params("arbitrary", "arbitrary"),
        name="adaln",
    )(cond8, ada_w, ada_b.reshape(depth, 1, n))


def _ffn_kernel(h_ref, mod_ref, g_ref, wg_ref, wu_ref, wd_ref, fn_ref, o_ref, hn_ref,
                *, base, final):
    j = pl.program_id(2)

    @pl.when(j == 0)
    def _():
        hn = _rms(h_ref[0], g_ref[...]) * (1.0 + mod_ref[0, base + 1:base + 2, :]) \
            + mod_ref[0, base:base + 1, :]
        hn_ref[...] = hn.astype(BF16)
        o_ref[0] = jnp.zeros(o_ref.shape[1:], F32)

    hn = hn_ref[...]
    gate = jnp.dot(hn, wg_ref[...], preferred_element_type=F32)
    up = jnp.dot(hn, wu_ref[...], preferred_element_type=F32)
    act = (gate * _sigmoid(gate) * up).astype(BF16)
    o_ref[0] += jnp.dot(act, wd_ref[...], preferred_element_type=F32)

    @pl.when(j == pl.num_programs(2) - 1)
    def _():
        out = h_ref[0] + 0.5 * mod_ref[0, base + 2:base + 3, :] * o_ref[0]
        if final:
            out = _rms(out, fn_ref[...])
        o_ref[0] = out


def _ffn(h, mod, g, wg, wu, wd, fn, *, sel, base, final, tm, tf):
    nb, l, d = h.shape
    f = wg.shape[-1]
    li, k = sel
    per_batch = mod.shape[0] > 1
    mod_map = (lambda b, i, j: (b, 0, 0)) if per_batch else (lambda b, i, j: (0, 0, 0))
    return pl.pallas_call(
        functools.partial(_ffn_kernel, base=base, final=final),
        grid=(nb, l // tm, f // tf),
        in_specs=[pl.BlockSpec((1, tm, d), lambda b, i, j: (b, i, 0)),
                  pl.BlockSpec((1, N_MOD, d), mod_map),
                  pl.BlockSpec((1, d), lambda b, i, j: (0, 0)),
                  pl.BlockSpec((None, None, d, tf), lambda b, i, j: (li, k, 0, j)),
                  pl.BlockSpec((None, None, d, tf), lambda b, i, j: (li, k, 0, j)),
                  pl.BlockSpec((None, None, tf, d), lambda b, i, j: (li, k, j, 0)),
                  pl.BlockSpec((1, d), lambda b, i, j: (0, 0))],
        out_specs=pl.BlockSpec((1, tm, d), lambda b, i, j: (b, i, 0)),
        out_shape=jax.ShapeDtypeStruct((nb, l, d), F32),
        scratch_shapes=[pltpu.VMEM((tm, d), BF16)],
        compiler_params=_params("arbitrary", "arbitrary", "arbitrary"),
        name="ffn",
    )(h, mod, g.reshape(1, d), wg, wu, wd, fn.reshape(1, d))


def _norm_mm_kernel(h_ref, mod_ref, g_ref, w_ref, o_ref, hn_ref):
    @pl.when(pl.program_id(2) == 0)
    def _():
        hn = _rms(h_ref[0], g_ref[...]) * (1.0 + mod_ref[0, 4:5, :]) + mod_ref[0, 3:4, :]
        hn_ref[...] = hn.astype(BF16)

    o_ref[0] = jnp.dot(hn_ref[...], w_ref[...], preferred_element_type=F32)


def _norm_mm(h, mod, g, w, *, tm, tn):
    nb, l, d = h.shape
    n = w.shape[1]
    per_batch = mod.shape[0] > 1
    mod_map = (lambda b, i, j: (b, 0, 0)) if per_batch else (lambda b, i, j: (0, 0, 0))
    return pl.pallas_call(
        _norm_mm_kernel,
        grid=(nb, l // tm, n // tn),
        in_specs=[pl.BlockSpec((1, tm, d), lambda b, i, j: (b, i, 0)),
                  pl.BlockSpec((1, N_MOD, d), mod_map),
                  pl.BlockSpec((1, d), lambda b, i, j: (0, 0)),
                  pl.BlockSpec((d, tn), lambda b, i, j: (0, j))],
        out_specs=pl.BlockSpec((1, tm, tn), lambda b, i, j: (b, i, j)),
        out_shape=jax.ShapeDtypeStruct((nb, l, n), F32),
        scratch_shapes=[pltpu.VMEM((tm, d), BF16)],
        compiler_params=_params("arbitrary", "arbitrary", "arbitrary"),
        name="norm_mm",
    )(h, mod, g.reshape(1, d), w)


POOL_HALO = 8


def _pool_kernel(prev_ref, x_ref, next_ref, pw_ref, ps_ref, o_ref, scr, *, t_rows, seq):
    i = pl.program_id(1)
    last = pl.num_programs(1) - 1
    hal = POOL_HALO
    scr[0:hal] = jnp.where(i > 0, prev_ref[0], 0.0)
    scr[hal:hal + t_rows] = x_ref[0]
    scr[hal + t_rows:2 * hal + t_rows] = jnp.where(i < last, next_ref[0], 0.0)
    t = i * t_rows + lax.broadcasted_iota(jnp.int32, (t_rows, LANES), 0)
    for gi, w in enumerate(POOL_WINDOWS):
        c0 = gi * LANES
        s = scr[hal - w // 2:hal - w // 2 + t_rows, c0:c0 + LANES]
        for dlt in range(-(w // 2) + 1, w // 2):
            s = s + scr[hal + dlt:hal + dlt + t_rows, c0:c0 + LANES]
        lo = jnp.clip(t - w // 2, 0, seq)
        hi = jnp.clip(t - w // 2 + w, 0, seq)
        cnt = (hi - lo).astype(F32)
        p = s / cnt - scr[hal:hal + t_rows, c0:c0 + LANES]
        y = _mm(p, pw_ref[gi]) * ps_ref[:, c0:c0 + LANES]
        o_ref[0, :, c0:c0 + LANES] = y.astype(o_ref.dtype)


def _pool(za, pool_w, pool_scale, *, t_rows):
    nb, l, c = za.shape
    hal = POOL_HALO
    r = t_rows // hal
    nh = l // hal
    return pl.pallas_call(
        functools.partial(_pool_kernel, t_rows=t_rows, seq=l),
        grid=(nb, l // t_rows),
        in_specs=[pl.BlockSpec((1, hal, c), lambda b, i: (b, jnp.maximum(i * r - 1, 0), 0)),
                  pl.BlockSpec((1, t_rows, c), lambda b, i: (b, i, 0)),
                  pl.BlockSpec((1, hal, c), lambda b, i: (b, jnp.minimum((i + 1) * r, nh - 1), 0)),
                  pl.BlockSpec(pool_w.shape, lambda b, i: (0, 0, 0)),
                  pl.BlockSpec((1, c), lambda b, i: (0, 0))],
        out_specs=pl.BlockSpec((1, t_rows, c), lambda b, i: (b, i, 0)),
        out_shape=jax.ShapeDtypeStruct((nb, l, c), BF16),
        scratch_shapes=[pltpu.VMEM((t_rows + 2 * hal, c), F32)],
        compiler_params=_params("arbitrary", "arbitrary"),
        name="pool",
    )(za, za, za, pool_w.astype(BF16), pool_scale.reshape(1, c))


def _gelu(x):
    return x * (0.5 * (1.0 + jnp.tanh(math.sqrt(2.0 / math.pi) * (x + 0.044715 * (x * x * x)))))


def _gmlp_kernel(z_ref, lng_ref, lnb_ref, ws_ref, bsb_ref, o_ref, *, t_rows):
    width = o_ref.shape[2]
    groups = ws_ref.shape[0]
    gd = width // groups
    for ci in range(t_rows // GMLP_CHUNK):
        rows = slice(ci * GMLP_CHUNK, (ci + 1) * GMLP_CHUNK)
        u = _gelu(z_ref[0, rows, 0:width])
        v = _gelu(z_ref[0, rows, width:2 * width])
        mu = jnp.mean(v, axis=-1, keepdims=True)
        var = jnp.mean(jnp.square(v - mu), axis=-1, keepdims=True)
        vn = ((v - mu) * lax.rsqrt(var + LN_EPS)) * lng_ref[...] + lnb_ref[...]
        for g in range(groups):
            cols = slice(g * gd, (g + 1) * gd)
            s = _mm(ws_ref[g], vn[:, cols]) + bsb_ref[g]
            o_ref[0, rows, cols] = (u[:, cols] * s).astype(o_ref.dtype)


def _gmlp(zuv, ln_g, ln_b, ws, bs, *, t_rows):
    nb, l, c2 = zuv.shape
    c = c2 // 2
    groups = ws.shape[0]
    bsb = jnp.broadcast_to(bs[:, :, None], (groups, GMLP_CHUNK, c // groups))
    return pl.pallas_call(
        functools.partial(_gmlp_kernel, t_rows=t_rows),
        grid=(nb, l // t_rows),
        in_specs=[pl.BlockSpec((1, t_rows, c2), lambda b, i: (b, i, 0)),
                  pl.BlockSpec((1, c), lambda b, i: (0, 0)),
                  pl.BlockSpec((1, c), lambda b, i: (0, 0)),
                  pl.BlockSpec(ws.shape, lambda b, i: (0, 0, 0)),
                  pl.BlockSpec(bsb.shape, lambda b, i: (0, 0, 0))],
        out_specs=pl.BlockSpec((1, t_rows, c), lambda b, i: (b, i, 0)),
        out_shape=jax.ShapeDtypeStruct((nb, l, c), BF16),
        compiler_params=_params("arbitrary", "arbitrary"),
        name="gmlp",
    )(zuv, ln_g.reshape(1, c), ln_b.reshape(1, c), ws.astype(BF16), bsb)


def _segsum(x, bd):
    hi = x.astype(BF16)
    lo = (x - hi.astype(F32)).astype(BF16)
    out = []
    for p in range(x.shape[-1] // LANES):
        cols = slice(p * LANES, (p + 1) * LANES)
        out.append(jnp.dot(hi[:, cols], bd, preferred_element_type=F32)
                   + jnp.dot(lo[:, cols], bd, preferred_element_type=F32))
    return jnp.concatenate(out, axis=-1)


def _rwkv_prep_kernel(prev_ref, x_ref, next_ref, mu_ref, w0_ref, wup_ref, a0_ref, aup_ref,
                      gup_ref, kk_ref, ka_ref, rk_ref, bd_ref,
                      lw0_o, lw1_o, kd0_o, kd1_o, b0_o, b1_o, kk_o, v_o, r_o, g_o, bv_o,
                      scr, *, t_rows, grid_mode, width):
    i = pl.program_id(1)
    last = pl.num_programs(1) - 1
    hal = GRID_W
    scr[0:hal] = jnp.where(i > 0, prev_ref[0], 0.0)
    scr[hal:hal + t_rows] = x_ref[0]
    scr[hal + t_rows:2 * hal + t_rows] = jnp.where(i < last, next_ref[0], 0.0)
    cols = scr.shape[1]
    if grid_mode:
        q = cols // 4
        bounds = (0, q, 2 * q, 3 * q, cols)
        offs = (-1, 1, -GRID_W, GRID_W)
    else:
        bounds = (0, cols // 2, cols)
        offs = (-1, 1)

    def zs_cols(c0, c1):
        n = c1 - c0
        x = scr[hal:hal + t_rows, c0:c1]
        ch = c0 + lax.broadcasted_iota(jnp.int32, (t_rows, n), 1)
        col = lax.broadcasted_iota(jnp.int32, (t_rows, n), 0) & (GRID_W - 1)
        shifted = None
        for qi, off in enumerate(offs):
            lo, hi = bounds[qi], bounds[qi + 1]
            if hi <= c0 or lo >= c1:
                continue
            src = scr[hal + off:hal + off + t_rows, c0:c1]
            if grid_mode and off == -1:
                src = jnp.where(col == 0, 0.0, src)
            if grid_mode and off == 1:
                src = jnp.where(col == GRID_W - 1, 0.0, src)
            shifted = src if shifted is None else jnp.where(ch >= lo, src, shifted)
        return x + (shifted - x) * mu_ref[:, c0:c1]

    w = width
    r = zs_cols(0, w)
    k = zs_cols(w, 2 * w)
    v = zs_cols(2 * w, 3 * w)
    rest = zs_cols(3 * w, cols)
    wd = jnp.tanh(rest[:, 0:LANES])
    ad = rest[:, LANES:2 * LANES]
    gd = _sigmoid(rest[:, 2 * LANES:3 * LANES])

    bd = bd_ref[...]
    kk0 = k * kk_ref[...]
    ss = _segsum(kk0 * kk0, bd)
    kk = kk0 / jnp.maximum(jnp.sqrt(ss), 1e-12)
    npair = w // LANES

    def put(o_ref, val):
        for p in range(npair):
            o_ref[0, p] = val[:, p * LANES:(p + 1) * LANES]

    put(kk_o, kk)
    put(v_o, v)
    put(r_o, r)
    kd_sum = None
    for d, (lw_o, kd_o, b_o) in enumerate(((lw0_o, kd0_o, b0_o), (lw1_o, kd1_o, b1_o))):
        w_pre = w0_ref[d] + _mm(wd, wup_ref[d])
        put(lw_o, -math.exp(-0.5) * _sigmoid(w_pre))
        a = _sigmoid(a0_ref[d] + _mm(ad, aup_ref[d]))
        kd = k * (1.0 + (a - 1.0) * ka_ref[...])
        put(kd_o, kd)
        put(b_o, kk * a)
        kd_sum = kd if kd_sum is None else kd_sum + kd
    g_o[0] = _mm(gd, gup_ref[...])
    bv_o[0] = _segsum(r * kd_sum * rk_ref[...], bd) * v


def _head_blockdiag(width):
    idx = jnp.arange(width) // HEAD_DIM
    return (idx[:, None] == idx[None, :]).astype(BF16)


def _rwkv_prep(zb, lp, *, t_rows, grid_mode):
    nb, l, cols = zb.shape
    w = lp['w0'].shape[-1]
    npair = w // LANES
    hal = GRID_W
    r = t_rows // hal
    nh = l // hal
    lora = lp['w_up'].shape[1]

    def pad_dir(up):
        z = jnp.zeros_like(up[0])
        return jnp.stack([jnp.concatenate([up[0], z], 0), jnp.concatenate([z, up[1]], 0)], 0)

    const2 = lambda b, i: (0, 0)
    const3 = lambda b, i: (0, 0, 0)
    pair_spec = pl.BlockSpec((1, npair, t_rows, LANES), lambda b, i: (b, 0, i, 0))
    wide_spec = pl.BlockSpec((1, t_rows, w), lambda b, i: (b, i, 0))
    pair_shape = jax.ShapeDtypeStruct((nb, npair, l, LANES), F32)
    wide_shape = jax.ShapeDtypeStruct((nb, l, w), F32)
    return pl.pallas_call(
        functools.partial(_rwkv_prep_kernel, t_rows=t_rows, grid_mode=grid_mode, width=w),
        grid=(nb, l // t_rows),
        in_specs=[pl.BlockSpec((1, hal, cols), lambda b, i: (b, jnp.maximum(i * r - 1, 0), 0)),
                  pl.BlockSpec((1, t_rows, cols), lambda b, i: (b, i, 0)),
                  pl.BlockSpec((1, hal, cols), lambda b, i: (b, jnp.minimum((i + 1) * r, nh - 1), 0)),
                  pl.BlockSpec((1, cols), const2),
                  pl.BlockSpec((2, 1, w), const3),
                  pl.BlockSpec((2, 2 * lora, w), const3),
                  pl.BlockSpec((2, 1, w), const3),
                  pl.BlockSpec((2, 2 * lora, w), const3),
                  pl.BlockSpec(lp['g_up'].shape, const2),
                  pl.BlockSpec((1, w), const2),
                  pl.BlockSpec((1, w), const2),
                  pl.BlockSpec((1, w), const2),
                  pl.BlockSpec((LANES, LANES), const2)],
        out_specs=[pair_spec] * 9 + [wide_spec] * 2,
        out_shape=[pair_shape] * 9 + [wide_shape] * 2,
        scratch_shapes=[pltpu.VMEM((t_rows + 2 * hal, cols), F32)],
        compiler_params=_params("arbitrary", "arbitrary"),
        name="rwkv_prep",
    )(zb, zb, zb, lp['mu'].reshape(1, cols), lp['w0'].reshape(2, 1, w),
      pad_dir(lp['w_up']).astype(BF16), lp['a0'].reshape(2, 1, w),
      pad_dir(lp['a_up']).astype(BF16), lp['g_up'].astype(BF16), lp['k_k'].reshape(1, w),
      lp['k_a'].reshape(1, w), lp['r_k'].reshape(1, w), _head_blockdiag(LANES))


def _stack2(x, m0):
    return jnp.concatenate([jnp.where(m0, x, 0.0), jnp.where(m0, 0.0, x)], axis=0)


_NN = (((1,), (0,)), ((), ()))
_NT = (((1,), (1,)), ((), ()))
_TN = (((0,), (0,)), ((), ()))

SCAN_PREC = {"cum": "rhs2", "gram": "bf16", "init": "bf16", "apply": "bf16", "out": "bf16",
             "state": "bf16"}
SCAN_LOCKSTEP_PAIRS = 8

def _split(a):
    hi = a.astype(BF16)
    return hi, (a - hi.astype(F32)).astype(BF16)


def _dg(a, b, dims, site):
    mode = SCAN_PREC[site]
    if mode == "f32":
        return lax.dot_general(a, b, dims, precision=HIGHEST, preferred_element_type=F32)
    if mode == "bf16":
        return lax.dot_general(a.astype(BF16), b.astype(BF16), dims, preferred_element_type=F32)
    b_hi, b_lo = _split(b)
    if mode == "rhs2":
        a16 = a.astype(BF16)
        return (lax.dot_general(a16, b_hi, dims, preferred_element_type=F32)
                + lax.dot_general(a16, b_lo, dims, preferred_element_type=F32))
    a_hi, a_lo = _split(a)
    return (lax.dot_general(a_hi, b_hi, dims, preferred_element_type=F32)
            + lax.dot_general(a_hi, b_lo, dims, preferred_element_type=F32)
            + lax.dot_general(a_lo, b_hi, dims, preferred_element_type=F32))


def _chunk_step(refs, s_ref, *, rev, tri, strict, incl, diag, m0):
    lw, kd, b, kk, v, r = (x[...] for x in refs)
    s = s_ref[...]
    c = lw.shape[0]
    cum = _dg(tri, lw, _NN, "cum")
    yield
    cum_prev = cum - lw
    end = 0 if rev else c - 1
    tot = cum[end:end + 1, :]
    mid = cum[c // 2:c // 2 + 1, :]
    e_inv = jnp.exp(mid - cum)
    nkk = -kk
    left = jnp.concatenate([_stack2(nkk * jnp.exp(cum_prev - mid), m0),
                            _stack2(r * jnp.exp(cum - mid), m0)], axis=0)
    right = jnp.concatenate([_stack2(b * e_inv, m0), _stack2(kd * e_inv, m0)], axis=0)
    gram = _dg(left, right, _NT, "gram")
    yield
    c2 = 2 * c
    a_ab = jnp.where(strict, gram[0:c2, 0:c2], 0.0)
    a_ak = jnp.where(strict, gram[0:c2, c2:2 * c2], 0.0)
    a_rb = jnp.where(incl, gram[c2:2 * c2, 0:c2], 0.0)
    a_rk = jnp.where(incl, gram[c2:2 * c2, c2:2 * c2], 0.0)
    vsw = pltpu.roll(v, HEAD_DIM, 1)
    vbd = jnp.concatenate([jnp.where(m0, 0.0, vsw), jnp.where(m0, vsw, 0.0)], axis=0)
    e_end = jnp.exp(tot - cum)
    x = (_dg(_stack2(nkk * jnp.exp(cum_prev), m0), s, _NT, "init")
         + _dg(a_ak, vbd, _NN, "init"))
    ybd = _dg(_stack2(r * jnp.exp(cum), m0), s, _NT, "out") + _dg(a_rk, vbd, _NN, "out")
    s_new = s * jnp.exp(tot) + _dg(vbd, _stack2(kd * e_end, m0), _TN, "state")
    steps = max(1, (c - 1).bit_length())
    pw = a_ab
    for it in range(steps):
        yield
        if it < steps - 1:
            both = _dg(pw, pw + x, _NN, "apply")
            x = x + jnp.where(diag, 0.0, both)
            pw = jnp.where(diag, both, 0.0)
        else:
            x = x + _dg(pw, x, _NN, "apply")
    yield
    u = x
    ybd = ybd + _dg(a_rb, u, _NN, "out")
    y = pltpu.roll(ybd[0:c] + ybd[c:c2], HEAD_DIM, 1)
    s_new = s_new + _dg(u, _stack2(b * e_end, m0), _TN, "state")
    return y, s_new


def _lockstep(gens):
    results = [None] * len(gens)
    live = list(range(len(gens)))
    while live:
        still = []
        for i in live:
            try:
                next(gens[i])
                still.append(i)
            except StopIteration as stop:
                results[i] = stop.value
        live = still
    return results


def _rwkv_scan_kernel(lw0, kd0, b0, kkf, vf, rf, lw1, kd1, b1, kkb, vb, rb, s0_ref,
                      yf_o, yb_o, sfin_o, st):
    ci = pl.program_id(1)
    c = lw0.shape[2]
    npair = lw0.shape[1]

    @pl.when(ci == 0)
    def _():
        st[...] = s0_ref[:, 0]

    row = lax.broadcasted_iota(jnp.int32, (c, c), 0)
    colm = lax.broadcasted_iota(jnp.int32, (c, c), 1)
    tri_f = (colm <= row).astype(F32)
    tri_b = (colm >= row).astype(F32)
    rows2 = lax.broadcasted_iota(jnp.int32, (2 * c, 2 * c), 0)
    cols2 = lax.broadcasted_iota(jnp.int32, (2 * c, 2 * c), 1)
    r2 = rows2 & (c - 1)
    c2 = cols2 & (c - 1)
    diag = (rows2 < c) == (cols2 < c)
    m0 = lax.broadcasted_iota(jnp.int32, (1, LANES), 1) < HEAD_DIM

    fwd_refs = (lw0, kd0, b0, kkf, vf, rf)
    bwd_refs = (lw1, kd1, b1, kkb, vb, rb)
    for p0 in range(0, npair, SCAN_LOCKSTEP_PAIRS):
        gens, outs = [], []
        for p in range(p0, min(p0 + SCAN_LOCKSTEP_PAIRS, npair)):
            gens.append(_chunk_step([x.at[0, p] for x in fwd_refs], st.at[0, p], rev=False,
                                    tri=tri_f, strict=c2 < r2, incl=c2 <= r2, diag=diag, m0=m0))
            outs.append((yf_o, 0, p))
            gens.append(_chunk_step([x.at[0, p] for x in bwd_refs], st.at[1, p], rev=True,
                                    tri=tri_b, strict=c2 > r2, incl=c2 >= r2, diag=diag, m0=m0))
            outs.append((yb_o, 1, p))
        for (y_o, d, p), (y, s_new) in zip(outs, _lockstep(gens)):
            y_o[0, p] = y
            st[d, p] = s_new

    @pl.when(ci == pl.num_programs(1) - 1)
    def _():
        sfin_o[:, 0] = st[...]


def _rwkv_scan(prep, s0):
    lw0, lw1, kd0, kd1, b0, b1, kk, v, r = prep
    nb, npair, l, _ = lw0.shape
    c = SCAN_CHUNK
    assert c == HEAD_DIM and 2 * HEAD_DIM == LANES and l % c == 0
    n = l // c
    fwd = pl.BlockSpec((1, npair, c, LANES), lambda b, i: (b, 0, i, 0))
    bwd = pl.BlockSpec((1, npair, c, LANES), lambda b, i: (b, 0, n - 1 - i, 0))
    st_spec = pl.BlockSpec((2, 1, npair, LANES, LANES), lambda b, i: (0, b, 0, 0, 0))
    y_shape = jax.ShapeDtypeStruct((nb, npair, l, LANES), F32)
    return pl.pallas_call(
        _rwkv_scan_kernel,
        grid=(nb, n),
        in_specs=[fwd] * 6 + [bwd] * 6 + [st_spec],
        out_specs=[fwd, bwd, st_spec],
        out_shape=[y_shape, y_shape, jax.ShapeDtypeStruct(s0.shape, F32)],
        scratch_shapes=[pltpu.VMEM((2, npair, LANES, LANES), F32)],
        compiler_params=_params("arbitrary", "arbitrary"),
        name="rwkv_scan",
    )(lw0, kd0, b0, kk, v, r, lw1, kd1, b1, kk, v, r, s0)


def _rwkv_post_kernel(yf_ref, yb_ref, bv_ref, g_ref, lng_ref, lnb_ref, bd_ref, o_ref):
    npair = yf_ref.shape[1]
    y = jnp.concatenate([yf_ref[0, p] + yb_ref[0, p] for p in range(npair)], axis=-1)
    bd = bd_ref[...]
    inv = 1.0 / HEAD_DIM
    m = _segsum(y, bd) * inv
    dlt = y - m
    var = _segsum(dlt * dlt, bd) * inv
    yn = dlt * lax.rsqrt(var + RWKV_GN_EPS) * lng_ref[...] + lnb_ref[...]
    o_ref[0] = ((yn + bv_ref[0]) * g_ref[0]).astype(o_ref.dtype)


def _rwkv_post(yf, yb, bv, g, ln_g, ln_b, *, t_rows):
    nb, npair, l, _ = yf.shape
    w = npair * LANES
    pair_spec = pl.BlockSpec((1, npair, t_rows, LANES), lambda b, i: (b, 0, i, 0))
    wide_spec = pl.BlockSpec((1, t_rows, w), lambda b, i: (b, i, 0))
    const2 = lambda b, i: (0, 0)
    return pl.pallas_call(
        _rwkv_post_kernel,
        grid=(nb, l // t_rows),
        in_specs=[pair_spec, pair_spec, wide_spec, wide_spec,
                  pl.BlockSpec((1, w), const2), pl.BlockSpec((1, w), const2),
                  pl.BlockSpec((LANES, LANES), const2)],
        out_specs=wide_spec,
        out_shape=jax.ShapeDtypeStruct((nb, l, w), BF16),
        compiler_params=_params("arbitrary", "arbitrary"),
        name="rwkv_post",
    )(yf, yb, bv, g, ln_g.reshape(1, w), ln_b.reshape(1, w), _head_blockdiag(LANES))


def _merge_kernel(h_ref, mod_ref, g_ref, yp_ref, yr_ref, yg_ref, wzg_ref, pp_ref, pr_ref, pg_ref,
                  wo_ref, o_ref, hn_ref):
    n = pl.program_id(2)

    @pl.when(n == 0)
    def _():
        hn = _rms(h_ref[0], g_ref[...]) * (1.0 + mod_ref[0, 4:5, :]) + mod_ref[0, 3:4, :]
        hn_ref[...] = hn.astype(BF16)
        o_ref[0] = jnp.zeros(o_ref.shape[1:], F32)

    hn = hn_ref[...]
    merged = None
    for br, (y_ref, p_ref) in enumerate(((yp_ref, pp_ref), (yr_ref, pr_ref), (yg_ref, pg_ref))):
        gate = _sigmoid(jnp.dot(hn, wzg_ref[br], preferred_element_type=F32))
        term = gate * jnp.dot(y_ref[0], p_ref[...], preferred_element_type=F32)
        merged = term if merged is None else merged + term
    o_ref[0] += jnp.dot(merged.astype(BF16), wo_ref[...], preferred_element_type=F32)

    @pl.when(n == pl.num_programs(2) - 1)
    def _():
        o_ref[0] = h_ref[0] + mod_ref[0, 5:6, :] * o_ref[0]


def _merge(h, mod, g, yp, yr, yg, wzg, pp, pr, pg, wo, *, li, tm, tn):
    nb, l, d = h.shape
    per_batch = mod.shape[0] > 1
    mod_map = (lambda b, i, n: (b, 0, 0)) if per_batch else (lambda b, i, n: (0, 0, 0))
    row = lambda width: pl.BlockSpec((1, tm, width), lambda b, i, n: (b, i, 0))
    col = lambda a: pl.BlockSpec((None, a.shape[1], tn), lambda b, i, n: (li, 0, n))
    return pl.pallas_call(
        _merge_kernel,
        grid=(nb, l // tm, d // tn),
        in_specs=[row(d), pl.BlockSpec((1, N_MOD, d), mod_map),
                  pl.BlockSpec((1, d), lambda b, i, n: (0, 0)),
                  row(yp.shape[2]), row(yr.shape[2]), row(yg.shape[2]),
                  pl.BlockSpec((wzg.shape[0], d, tn), lambda b, i, n: (0, 0, n)),
                  col(pp), col(pr), col(pg),
                  pl.BlockSpec((None, tn, d), lambda b, i, n: (li, n, 0))],
        out_specs=row(d),
        out_shape=jax.ShapeDtypeStruct((nb, l, d), F32),
        scratch_shapes=[pltpu.VMEM((tm, d), BF16)],
        compiler_params=_params("arbitrary", "arbitrary", "arbitrary"),
        name="merge",
    )(h, mod, g.reshape(1, d), yp, yr, yg, wzg, pp, pr, pg, wo)


def _pick(n, pref):
    t = min(pref, n)
    while t > LANES and (n % t or t % LANES):
        t -= LANES
    return t if n % t == 0 else n


def kernel(x, c, ctx, c_ctx, ada_w, ada_b, norm_g, ffn_w_gate, ffn_w_up, ffn_w_down, w_in,
           pool_w, pool_scale, rwkv_mu, rwkv_w0, rwkv_w_up, rwkv_a0, rwkv_a_up, rwkv_g_up,
           rwkv_k_k, rwkv_k_a, rwkv_r_k, rwkv_ln_g, rwkv_ln_b, gmlp_ln_g, gmlp_ln_b, gmlp_ws,
           gmlp_bs, proj_pool, proj_rwkv, proj_gmlp, w_out, final_norm):
    depth = ada_w.shape[0]
    nb, l, d = x.shape
    lc = ctx.shape[1]
    width = rwkv_w0.shape[-1]
    pool_c = pool_scale.shape[-1]
    gmlp_c = gmlp_ln_g.shape[-1]
    rwkv_cols = rwkv_mu.shape[-1]
    off_rwkv = pool_c
    off_gmlp = off_rwkv + rwkv_cols
    off_gate = off_gmlp + 2 * gmlp_c
    npair = width // LANES

    cond8 = jnp.zeros((8, d), F32).at[:nb].set(c).at[nb].set(c_ctx)
    mod_all = _adaln(cond8, ada_w, ada_b).reshape(depth, 8, N_MOD, d)

    zero_state = jnp.zeros((2, nb, npair, LANES, LANES), F32)
    wg, wu, wd = (a.astype(BF16) for a in (ffn_w_gate, ffn_w_up, ffn_w_down))
    pp, pr, pg, wo = (a.astype(BF16) for a in (proj_pool, proj_rwkv, proj_gmlp, w_out))
    win_all = w_in.astype(BF16)
    hc = ctx
    for li in range(depth):
        last = li == depth - 1
        mod_x = mod_all[li, :nb]
        mod_c = mod_all[li, nb:nb + 1]
        win = win_all[li]
        w_za = win[:, :off_rwkv]
        w_zb = win[:, off_rwkv:off_gmlp]
        w_zuv = win[:, off_gmlp:off_gate]
        w_zg = win[:, off_gate:].reshape(d, -1, d).transpose(1, 0, 2)
        lp = {'mu': rwkv_mu[li], 'w0': rwkv_w0[li], 'w_up': rwkv_w_up[li], 'a0': rwkv_a0[li],
              'a_up': rwkv_a_up[li], 'g_up': rwkv_g_up[li], 'k_k': rwkv_k_k[li],
              'k_a': rwkv_k_a[li], 'r_k': rwkv_r_k[li]}
        tf = _pick(wg.shape[-1], 512)

        def mixer(h, mod, seq, grid_mode, s0, need_out):
            tm = _pick(seq, 1024)
            zb = _norm_mm(h, mod, norm_g[li, 1], w_zb, tm=tm, tn=_pick(rwkv_cols, 1152))
            prep = _rwkv_prep(zb, lp, t_rows=_pick(seq, 256), grid_mode=grid_mode)
            yf, yb, s_fin = _rwkv_scan(prep[:9], s0)
            if not need_out:
                return None, s_fin
            y_rwkv = _rwkv_post(yf, yb, prep[10], prep[9], rwkv_ln_g[li], rwkv_ln_b[li],
                                t_rows=_pick(seq, 256))
            za = _norm_mm(h, mod, norm_g[li, 1], w_za, tm=tm, tn=_pick(pool_c, 512))
            y_pool = _pool(za, pool_w[li], pool_scale[li], t_rows=_pick(seq, 512))
            zuv = _norm_mm(h, mod, norm_g[li, 1], w_zuv, tm=tm, tn=_pick(2 * gmlp_c, 512))
            y_gmlp = _gmlp(zuv, gmlp_ln_g[li], gmlp_ln_b[li], gmlp_ws[li], gmlp_bs[li],
                           t_rows=_pick(seq, 512))
            out = _merge(h, mod, norm_g[li, 1], y_pool, y_rwkv, y_gmlp, w_zg, pp, pr, pg, wo,
                         li=li, tm=_pick(seq, 512), tn=_pick(d, 512))
            return out, s_fin

        def ffn(h, mod, seq, k, final=False):
            return _ffn(h, mod, norm_g[li, 2 * k], wg, wu, wd, final_norm, sel=(li, k),
                        base=6 * k, final=final, tm=_pick(seq, 512), tf=tf)

        hc = ffn(hc, mod_c, lc, 0)
        hc_new, s_ctx = mixer(hc, mod_c, lc, False, zero_state, not last)
        if not last:
            hc = ffn(hc_new, mod_c, lc, 1)

        x = ffn(x, mod_x, l, 0)
        x, _ = mixer(x, mod_x, l, True, s_ctx, True)
        x = ffn(x, mod_x, l, 1, final=last)
    return x
```

```python
import functools
import math

import jax
import jax.numpy as jnp
from jax import lax
from jax.experimental import pallas as pl
from jax.experimental.pallas import tpu as pltpu

F32 = jnp.float32
BF16 = jnp.bfloat16
HIGHEST = lax.Precision.HIGHEST

NORM_EPS = 1e-6
LN_EPS = 1e-5
RWKV_GN_EPS = 64e-5
GRID_W = 64
N_MOD = 9
POOL_WINDOWS = (2, 4, 8, 16)
HEAD_DIM = 64
GMLP_CHUNK = 128
LANES = 128
SCAN_CHUNK = 64
VMEM_LIMIT = 56 * 1024 * 1024


def _params(*sem):
    return pltpu.CompilerParams(dimension_semantics=sem, vmem_limit_bytes=VMEM_LIMIT)


def _sigmoid(x):
    return 1.0 / (1.0 + jnp.exp(-x))


def _mm(a, b):
    return jnp.dot(a.astype(BF16), b.astype(BF16), preferred_element_type=F32)


def _mm_hi(a, b):
    return jnp.dot(a, b, precision=HIGHEST, preferred_element_type=F32)


def _mm_nt_hi(a, b):
    return lax.dot_general(a, b, (((1,), (1,)), ((), ())), precision=HIGHEST,
                           preferred_element_type=F32)


def _mm_tn_hi(a, b):
    return lax.dot_general(a, b, (((0,), (0,)), ((), ())), precision=HIGHEST,
                           preferred_element_type=F32)


def _rms(x, g):
    return x * lax.rsqrt(jnp.mean(x * x, axis=-1, keepdims=True) + NORM_EPS) * g


def _adaln_kernel(c_ref, w_ref, b_ref, o_ref):
    c = c_ref[...]
    o_ref[0] = _mm_hi(c * _sigmoid(c), w_ref[0]) + b_ref[0]


def _adaln(cond8, ada_w, ada_b):
    depth, d, n = ada_w.shape
    tn = 1024
    return pl.pallas_call(
        _adaln_kernel,
        grid=(depth, n // tn),
        in_specs=[pl.BlockSpec((8, d), lambda l, j: (0, 0)),
                  pl.BlockSpec((1, d, tn), lambda l, j: (l, 0, j)),
                  pl.BlockSpec((1, 1, tn), lambda l, j: (l, 0, j))],
        out_specs=pl.BlockSpec((1, 8, tn), lambda l, j: (l, 0, j)),
        out_shape=jax.ShapeDtypeStruct((depth, 8, n), F32),
        compiler_params=_params("arbitrary", "arbitrary"),
        name="adaln",
    )(cond8, ada_w, ada_b.reshape(depth, 1, n))


def _ffn_kernel(h_ref, mod_ref, g_ref, wg_ref, wu_ref, wd_ref, fn_ref, o_ref, hn_ref,
                *, base, final):
    j = pl.program_id(2)

    @pl.when(j == 0)
    def _():
        hn = _rms(h_ref[0], g_ref[...]) * (1.0 + mod_ref[0, base + 1:base + 2, :]) \
            + mod_ref[0, base:base + 1, :]
        hn_ref[...] = hn.astype(BF16)
        o_ref[0] = h_ref[0]

    hn = hn_ref[...]
    gate = jnp.dot(hn, wg_ref[...], preferred_element_type=F32)
    up = jnp.dot(hn, wu_ref[...], preferred_element_type=F32)
    act = (gate * _sigmoid(gate) * up).astype(BF16)
    o_ref[0] += (0.5 * mod_ref[0, base + 2:base + 3, :]) * jnp.dot(
        act, wd_ref[...], preferred_element_type=F32)

    if final:
        @pl.when(j == pl.num_programs(2) - 1)
        def _():
            o_ref[0] = _rms(o_ref[0], fn_ref[...])


def _ffn(h, mod, g, wg, wu, wd, fn, *, sel, base, final, tm, tf):
    nb, l, d = h.shape
    f = wg.shape[-1]
    li, k = sel
    per_batch = mod.shape[0] > 1
    mod_map = (lambda b, i, j: (b, 0, 0)) if per_batch else (lambda b, i, j: (0, 0, 0))
    return pl.pallas_call(
        functools.partial(_ffn_kernel, base=base, final=final),
        grid=(nb, l // tm, f // tf),
        in_specs=[pl.BlockSpec((1, tm, d), lambda b, i, j: (b, i, 0)),
                  pl.BlockSpec((1, N_MOD, d), mod_map),
                  pl.BlockSpec((1, d), lambda b, i, j: (0, 0)),
                  pl.BlockSpec((None, None, d, tf), lambda b, i, j: (li, k, 0, j)),
                  pl.BlockSpec((None, None, d, tf), lambda b, i, j: (li, k, 0, j)),
                  pl.BlockSpec((None, None, tf, d), lambda b, i, j: (li, k, j, 0)),
                  pl.BlockSpec((1, d), lambda b, i, j: (0, 0))],
        out_specs=pl.BlockSpec((1, tm, d), lambda b, i, j: (b, i, 0)),
        out_shape=jax.ShapeDtypeStruct((nb, l, d), F32),
        scratch_shapes=[pltpu.VMEM((tm, d), BF16)],
        compiler_params=_params("arbitrary", "arbitrary", "arbitrary"),
        name="ffn",
    )(h, mod, g.reshape(1, d), wg, wu, wd, fn.reshape(1, d))


def _norm_mm_kernel(h_ref, mod_ref, g_ref, w_ref, o_ref, hn_ref):
    @pl.when(pl.program_id(2) == 0)
    def _():
        hn = _rms(h_ref[0], g_ref[...]) * (1.0 + mod_ref[0, 4:5, :]) + mod_ref[0, 3:4, :]
        hn_ref[...] = hn.astype(BF16)

    o_ref[0] = jnp.dot(hn_ref[...], w_ref[...], preferred_element_type=F32)


def _norm_mm(h, mod, g, w, *, tm, tn):
    nb, l, d = h.shape
    n = w.shape[1]
    per_batch = mod.shape[0] > 1
    mod_map = (lambda b, i, j: (b, 0, 0)) if per_batch else (lambda b, i, j: (0, 0, 0))
    return pl.pallas_call(
        _norm_mm_kernel,
        grid=(nb, l // tm, n // tn),
        in_specs=[pl.BlockSpec((1, tm, d), lambda b, i, j: (b, i, 0)),
                  pl.BlockSpec((1, N_MOD, d), mod_map),
                  pl.BlockSpec((1, d), lambda b, i, j: (0, 0)),
                  pl.BlockSpec((d, tn), lambda b, i, j: (0, j))],
        out_specs=pl.BlockSpec((1, tm, tn), lambda b, i, j: (b, i, j)),
        out_shape=jax.ShapeDtypeStruct((nb, l, n), F32),
        scratch_shapes=[pltpu.VMEM((tm, d), BF16)],
        compiler_params=_params("arbitrary", "arbitrary", "arbitrary"),
        name="norm_mm",
    )(h, mod, g.reshape(1, d), w)


POOL_HALO = 8


def _pool_kernel(prev_ref, x_ref, next_ref, pw_ref, ps_ref, o_ref, scr, *, t_rows, seq):
    i = pl.program_id(1)
    last = pl.num_programs(1) - 1
    hal = POOL_HALO
    scr[0:hal] = jnp.where(i > 0, prev_ref[0], 0.0)
    scr[hal:hal + t_rows] = x_ref[0]
    scr[hal + t_rows:2 * hal + t_rows] = jnp.where(i < last, next_ref[0], 0.0)
    t = i * t_rows + lax.broadcasted_iota(jnp.int32, (t_rows, LANES), 0)
    for gi, w in enumerate(POOL_WINDOWS):
        c0 = gi * LANES
        s = scr[hal - w // 2:hal - w // 2 + t_rows, c0:c0 + LANES]
        for dlt in range(-(w // 2) + 1, w // 2):
            s = s + scr[hal + dlt:hal + dlt + t_rows, c0:c0 + LANES]
        lo = jnp.clip(t - w // 2, 0, seq)
        hi = jnp.clip(t - w // 2 + w, 0, seq)
        cnt = (hi - lo).astype(F32)
        p = s / cnt - scr[hal:hal + t_rows, c0:c0 + LANES]
        y = _mm(p, pw_ref[gi]) * ps_ref[:, c0:c0 + LANES]
        o_ref[0, :, c0:c0 + LANES] = y.astype(o_ref.dtype)


def _pool(z, pool_w, pool_scale, *, col_block, t_rows):
    nb, l, _ = z.shape
    c = pool_scale.shape[-1]
    hal = POOL_HALO
    r = t_rows // hal
    nh = l // hal
    cb = col_block
    return pl.pallas_call(
        functools.partial(_pool_kernel, t_rows=t_rows, seq=l),
        grid=(nb, l // t_rows),
        in_specs=[pl.BlockSpec((1, hal, c), lambda b, i: (b, jnp.maximum(i * r - 1, 0), cb)),
                  pl.BlockSpec((1, t_rows, c), lambda b, i: (b, i, cb)),
                  pl.BlockSpec((1, hal, c), lambda b, i: (b, jnp.minimum((i + 1) * r, nh - 1), cb)),
                  pl.BlockSpec(pool_w.shape, lambda b, i: (0, 0, 0)),
                  pl.BlockSpec((1, c), lambda b, i: (0, 0))],
        out_specs=pl.BlockSpec((1, t_rows, c), lambda b, i: (b, i, 0)),
        out_shape=jax.ShapeDtypeStruct((nb, l, c), BF16),
        scratch_shapes=[pltpu.VMEM((t_rows + 2 * hal, c), F32)],
        compiler_params=_params("arbitrary", "arbitrary"),
        name="pool",
    )(z, z, z, pool_w.astype(BF16), pool_scale.reshape(1, c))


def _gelu(x):
    return x * (0.5 * (1.0 + jnp.tanh(math.sqrt(2.0 / math.pi) * (x + 0.044715 * (x * x * x)))))


def _gmlp_kernel(z_ref, lng_ref, lnb_ref, ws_ref, bsb_ref, o_ref, *, t_rows):
    width = o_ref.shape[2]
    groups = ws_ref.shape[0]
    gd = width // groups
    for ci in range(t_rows // GMLP_CHUNK):
        rows = slice(ci * GMLP_CHUNK, (ci + 1) * GMLP_CHUNK)
        u = _gelu(z_ref[0, rows, 0:width])
        v = _gelu(z_ref[0, rows, width:2 * width])
        mu = jnp.mean(v, axis=-1, keepdims=True)
        var = jnp.mean(jnp.square(v - mu), axis=-1, keepdims=True)
        vn = ((v - mu) * lax.rsqrt(var + LN_EPS)) * lng_ref[...] + lnb_ref[...]
        for g in range(groups):
            cols = slice(g * gd, (g + 1) * gd)
            s = _mm(ws_ref[g], vn[:, cols]) + bsb_ref[g]
            o_ref[0, rows, cols] = (u[:, cols] * s).astype(o_ref.dtype)


def _gmlp(zuv, ln_g, ln_b, ws, bs, *, t_rows):
    nb, l, _ = zuv.shape
    c = ln_g.shape[-1]
    c2 = 2 * c
    groups = ws.shape[0]
    bsb = jnp.broadcast_to(bs[:, :, None], (groups, GMLP_CHUNK, c // groups))
    return pl.pallas_call(
        functools.partial(_gmlp_kernel, t_rows=t_rows),
        grid=(nb, l // t_rows),
        in_specs=[pl.BlockSpec((1, t_rows, c2), lambda b, i: (b, i, 0)),
                  pl.BlockSpec((1, c), lambda b, i: (0, 0)),
                  pl.BlockSpec((1, c), lambda b, i: (0, 0)),
                  pl.BlockSpec(ws.shape, lambda b, i: (0, 0, 0)),
                  pl.BlockSpec(bsb.shape, lambda b, i: (0, 0, 0))],
        out_specs=pl.BlockSpec((1, t_rows, c), lambda b, i: (b, i, 0)),
        out_shape=jax.ShapeDtypeStruct((nb, l, c), BF16),
        compiler_params=_params("arbitrary", "arbitrary"),
        name="gmlp",
    )(zuv, ln_g.reshape(1, c), ln_b.reshape(1, c), ws.astype(BF16), bsb)


def _segsum(x, bd):
    hi = x.astype(BF16)
    lo = (x - hi.astype(F32)).astype(BF16)
    out = []
    for p in range(x.shape[-1] // LANES):
        cols = slice(p * LANES, (p + 1) * LANES)
        out.append(jnp.dot(hi[:, cols], bd, preferred_element_type=F32)
                   + jnp.dot(lo[:, cols], bd, preferred_element_type=F32))
    return jnp.concatenate(out, axis=-1)


def _rwkv_prep_kernel(prev_ref, x_ref, next_ref, mu_ref, w0_ref, wup_ref, a0_ref, aup_ref,
                      gup_ref, kk_ref, ka_ref, rk_ref, bd_ref,
                      lw0_o, lw1_o, kd0_o, kd1_o, b0_o, b1_o, kk_o, v_o, r_o, g_o, bv_o,
                      scr, *, t_rows, grid_mode, width):
    i = pl.program_id(1)
    last = pl.num_programs(1) - 1
    hal = GRID_W
    scr[0:hal] = jnp.where(i > 0, prev_ref[0], 0.0)
    scr[hal:hal + t_rows] = x_ref[0]
    scr[hal + t_rows:2 * hal + t_rows] = jnp.where(i < last, next_ref[0], 0.0)
    cols = scr.shape[1]
    if grid_mode:
        q = cols // 4
        bounds = (0, q, 2 * q, 3 * q, cols)
        offs = (-1, 1, -GRID_W, GRID_W)
    else:
        bounds = (0, cols // 2, cols)
        offs = (-1, 1)

    def zs_cols(c0, c1):
        n = c1 - c0
        x = scr[hal:hal + t_rows, c0:c1]
        ch = c0 + lax.broadcasted_iota(jnp.int32, (t_rows, n), 1)
        col = lax.broadcasted_iota(jnp.int32, (t_rows, n), 0) & (GRID_W - 1)
        shifted = None
        for qi, off in enumerate(offs):
            lo, hi = bounds[qi], bounds[qi + 1]
            if hi <= c0 or lo >= c1:
                continue
            src = scr[hal + off:hal + off + t_rows, c0:c1]
            if grid_mode and off == -1:
                src = jnp.where(col == 0, 0.0, src)
            if grid_mode and off == 1:
                src = jnp.where(col == GRID_W - 1, 0.0, src)
            shifted = src if shifted is None else jnp.where(ch >= lo, src, shifted)
        return x + (shifted - x) * mu_ref[:, c0:c1]

    w = width
    r = zs_cols(0, w)
    k = zs_cols(w, 2 * w)
    v = zs_cols(2 * w, 3 * w)
    rest = zs_cols(3 * w, cols)
    wd = jnp.tanh(rest[:, 0:LANES])
    ad = rest[:, LANES:2 * LANES]
    gd = _sigmoid(rest[:, 2 * LANES:3 * LANES])

    bd = bd_ref[...]
    kk0 = k * kk_ref[...]
    ss = _segsum(kk0 * kk0, bd)
    kk = kk0 / jnp.maximum(jnp.sqrt(ss), 1e-12)
    npair = w // LANES

    def put(o_ref, val):
        for p in range(npair):
            o_ref[0, p] = val[:, p * LANES:(p + 1) * LANES]

    put(kk_o, kk)
    put(v_o, v)
    put(r_o, r)
    kd_sum = None
    for d, (lw_o, kd_o, b_o) in enumerate(((lw0_o, kd0_o, b0_o), (lw1_o, kd1_o, b1_o))):
        w_pre = w0_ref[d] + _mm(wd, wup_ref[d])
        put(lw_o, -math.exp(-0.5) * _sigmoid(w_pre))
        a = _sigmoid(a0_ref[d] + _mm(ad, aup_ref[d]))
        kd = k * (1.0 + (a - 1.0) * ka_ref[...])
        put(kd_o, kd)
        put(b_o, kk * a)
        kd_sum = kd if kd_sum is None else kd_sum + kd
    g_o[0] = _mm(gd, gup_ref[...])
    bv_o[0] = _segsum(r * kd_sum * rk_ref[...], bd) * v


def _head_blockdiag(width):
    idx = jnp.arange(width) // HEAD_DIM
    return (idx[:, None] == idx[None, :]).astype(BF16)


def _rwkv_prep(zb, lp, *, t_rows, grid_mode):
    nb, l, cols = zb.shape
    w = lp['w0'].shape[-1]
    npair = w // LANES
    hal = GRID_W
    r = t_rows // hal
    nh = l // hal
    lora = lp['w_up'].shape[1]

    def pad_dir(up):
        z = jnp.zeros_like(up[0])
        return jnp.stack([jnp.concatenate([up[0], z], 0), jnp.concatenate([z, up[1]], 0)], 0)

    const2 = lambda b, i: (0, 0)
    const3 = lambda b, i: (0, 0, 0)
    pair_spec = pl.BlockSpec((1, npair, t_rows, LANES), lambda b, i: (b, 0, i, 0))
    wide_spec = pl.BlockSpec((1, t_rows, w), lambda b, i: (b, i, 0))
    pair_shape = jax.ShapeDtypeStruct((nb, npair, l, LANES), F32)
    wide_shape = jax.ShapeDtypeStruct((nb, l, w), F32)
    return pl.pallas_call(
        functools.partial(_rwkv_prep_kernel, t_rows=t_rows, grid_mode=grid_mode, width=w),
        grid=(nb, l // t_rows),
        in_specs=[pl.BlockSpec((1, hal, cols), lambda b, i: (b, jnp.maximum(i * r - 1, 0), 0)),
                  pl.BlockSpec((1, t_rows, cols), lambda b, i: (b, i, 0)),
                  pl.BlockSpec((1, hal, cols), lambda b, i: (b, jnp.minimum((i + 1) * r, nh - 1), 0)),
                  pl.BlockSpec((1, cols), const2),
                  pl.BlockSpec((2, 1, w), const3),
                  pl.BlockSpec((2, 2 * lora, w), const3),
                  pl.BlockSpec((2, 1, w), const3),
                  pl.BlockSpec((2, 2 * lora, w), const3),
                  pl.BlockSpec(lp['g_up'].shape, const2),
                  pl.BlockSpec((1, w), const2),
                  pl.BlockSpec((1, w), const2),
                  pl.BlockSpec((1, w), const2),
                  pl.BlockSpec((LANES, LANES), const2)],
        out_specs=[pair_spec] * 9 + [wide_spec] * 2,
        out_shape=[pair_shape] * 9 + [wide_shape] * 2,
        scratch_shapes=[pltpu.VMEM((t_rows + 2 * hal, cols), F32)],
        compiler_params=_params("arbitrary", "arbitrary"),
        name="rwkv_prep",
    )(zb, zb, zb, lp['mu'].reshape(1, cols), lp['w0'].reshape(2, 1, w),
      pad_dir(lp['w_up']).astype(BF16), lp['a0'].reshape(2, 1, w),
      pad_dir(lp['a_up']).astype(BF16), lp['g_up'].astype(BF16), lp['k_k'].reshape(1, w),
      lp['k_a'].reshape(1, w), lp['r_k'].reshape(1, w), _head_blockdiag(LANES))


def _stack2(x, m0):
    return jnp.concatenate([jnp.where(m0, x, 0.0), jnp.where(m0, 0.0, x)], axis=0)


_NN = (((1,), (0,)), ((), ()))
_NT = (((1,), (1,)), ((), ()))
_TN = (((0,), (0,)), ((), ()))

SCAN_PREC = {"cum": "rhs2", "gram": "bf16", "init": "bf16", "apply": "bf16", "out": "bf16",
             "state": "bf16"}
SCAN_LOCKSTEP_PAIRS = 8

def _split(a):
    hi = a.astype(BF16)
    return hi, (a - hi.astype(F32)).astype(BF16)


def _dg(a, b, dims, site):
    mode = SCAN_PREC[site]
    if mode == "f32":
        return lax.dot_general(a, b, dims, precision=HIGHEST, preferred_element_type=F32)
    if mode == "bf16":
        return lax.dot_general(a.astype(BF16), b.astype(BF16), dims, preferred_element_type=F32)
    b_hi, b_lo = _split(b)
    if mode == "rhs2":
        a16 = a.astype(BF16)
        return (lax.dot_general(a16, b_hi, dims, preferred_element_type=F32)
                + lax.dot_general(a16, b_lo, dims, preferred_element_type=F32))
    a_hi, a_lo = _split(a)
    return (lax.dot_general(a_hi, b_hi, dims, preferred_element_type=F32)
            + lax.dot_general(a_hi, b_lo, dims, preferred_element_type=F32)
            + lax.dot_general(a_lo, b_hi, dims, preferred_element_type=F32))


def _chunk_step(refs, s_ref, *, rev, tri, strict, incl, diag, m0):
    lw, kd, b, kk, v, r = (x[...] for x in refs)
    s = s_ref[...]
    c = lw.shape[0]
    cum = _dg(tri, lw, _NN, "cum")
    yield
    cum_prev = cum - lw
    end = 0 if rev else c - 1
    tot = cum[end:end + 1, :]
    mid = cum[c // 2:c // 2 + 1, :]
    e_inv = jnp.exp(mid - cum)
    nkk = -kk
    left = jnp.concatenate([_stack2(nkk * jnp.exp(cum_prev - mid), m0),
                            _stack2(r * jnp.exp(cum - mid), m0)], axis=0)
    right = jnp.concatenate([_stack2(b * e_inv, m0), _stack2(kd * e_inv, m0)], axis=0)
    gram = _dg(left, right, _NT, "gram")
    yield
    c2 = 2 * c
    a_ab = jnp.where(strict, gram[0:c2, 0:c2], 0.0)
    a_ak = jnp.where(strict, gram[0:c2, c2:2 * c2], 0.0)
    a_rb = jnp.where(incl, gram[c2:2 * c2, 0:c2], 0.0)
    a_rk = jnp.where(incl, gram[c2:2 * c2, c2:2 * c2], 0.0)
    vsw = pltpu.roll(v, HEAD_DIM, 1)
    vbd = jnp.concatenate([jnp.where(m0, 0.0, vsw), jnp.where(m0, vsw, 0.0)], axis=0)
    e_end = jnp.exp(tot - cum)
    x = (_dg(_stack2(nkk * jnp.exp(cum_prev), m0), s, _NT, "init")
         + _dg(a_ak, vbd, _NN, "init"))
    ybd = _dg(_stack2(r * jnp.exp(cum), m0), s, _NT, "out") + _dg(a_rk, vbd, _NN, "out")
    s_new = s * jnp.exp(tot) + _dg(vbd, _stack2(kd * e_end, m0), _TN, "state")
    steps = max(1, (c - 1).bit_length())
    pw = a_ab
    for it in range(steps):
        yield
        if it < steps - 1:
            both = _dg(pw, pw + x, _NN, "apply")
            x = x + jnp.where(diag, 0.0, both)
            pw = jnp.where(diag, both, 0.0)
        else:
            x = x + _dg(pw, x, _NN, "apply")
    yield
    u = x
    ybd = ybd + _dg(a_rb, u, _NN, "out")
    y = pltpu.roll(ybd[0:c] + ybd[c:c2], HEAD_DIM, 1)
    s_new = s_new + _dg(u, _stack2(b * e_end, m0), _TN, "state")
    return y, s_new


def _lockstep(gens):
    results = [None] * len(gens)
    live = list(range(len(gens)))
    while live:
        still = []
        for i in live:
            try:
                next(gens[i])
                still.append(i)
            except StopIteration as stop:
                results[i] = stop.value
        live = still
    return results


def _rwkv_scan_kernel(lw0, kd0, b0, kkf, vf, rf, lw1, kd1, b1, kkb, vb, rb, s0_ref,
                      yf_o, yb_o, sfin_o, st):
    ci = pl.program_id(1)
    c = lw0.shape[2]
    npair = lw0.shape[1]

    @pl.when(ci == 0)
    def _():
        st[...] = s0_ref[:, 0]

    row = lax.broadcasted_iota(jnp.int32, (c, c), 0)
    colm = lax.broadcasted_iota(jnp.int32, (c, c), 1)
    tri_f = (colm <= row).astype(F32)
    tri_b = (colm >= row).astype(F32)
    rows2 = lax.broadcasted_iota(jnp.int32, (2 * c, 2 * c), 0)
    cols2 = lax.broadcasted_iota(jnp.int32, (2 * c, 2 * c), 1)
    r2 = rows2 & (c - 1)
    c2 = cols2 & (c - 1)
    diag = (rows2 < c) == (cols2 < c)
    m0 = lax.broadcasted_iota(jnp.int32, (1, LANES), 1) < HEAD_DIM

    fwd_refs = (lw0, kd0, b0, kkf, vf, rf)
    bwd_refs = (lw1, kd1, b1, kkb, vb, rb)
    for p0 in range(0, npair, SCAN_LOCKSTEP_PAIRS):
        gens, outs = [], []
        for p in range(p0, min(p0 + SCAN_LOCKSTEP_PAIRS, npair)):
            gens.append(_chunk_step([x.at[0, p] for x in fwd_refs], st.at[0, p], rev=False,
                                    tri=tri_f, strict=c2 < r2, incl=c2 <= r2, diag=diag, m0=m0))
            outs.append((yf_o, 0, p))
            gens.append(_chunk_step([x.at[0, p] for x in bwd_refs], st.at[1, p], rev=True,
                                    tri=tri_b, strict=c2 > r2, incl=c2 >= r2, diag=diag, m0=m0))
            outs.append((yb_o, 1, p))
        for (y_o, d, p), (y, s_new) in zip(outs, _lockstep(gens)):
            y_o[0, p] = y
            st[d, p] = s_new

    @pl.when(ci == pl.num_programs(1) - 1)
    def _():
        sfin_o[:, 0] = st[...]


def _rwkv_scan(prep, s0):
    lw0, lw1, kd0, kd1, b0, b1, kk, v, r = prep
    nb, npair, l, _ = lw0.shape
    c = SCAN_CHUNK
    assert c == HEAD_DIM and 2 * HEAD_DIM == LANES and l % c == 0
    n = l // c
    fwd = pl.BlockSpec((1, npair, c, LANES), lambda b, i: (b, 0, i, 0))
    bwd = pl.BlockSpec((1, npair, c, LANES), lambda b, i: (b, 0, n - 1 - i, 0))
    st_spec = pl.BlockSpec((2, 1, npair, LANES, LANES), lambda b, i: (0, b, 0, 0, 0))
    y_shape = jax.ShapeDtypeStruct((nb, npair, l, LANES), F32)
    return pl.pallas_call(
        _rwkv_scan_kernel,
        grid=(nb, n),
        in_specs=[fwd] * 6 + [bwd] * 6 + [st_spec],
        out_specs=[fwd, bwd, st_spec],
        out_shape=[y_shape, y_shape, jax.ShapeDtypeStruct(s0.shape, F32)],
        scratch_shapes=[pltpu.VMEM((2, npair, LANES, LANES), F32)],
        compiler_params=_params("arbitrary", "arbitrary"),
        name="rwkv_scan",
    )(lw0, kd0, b0, kk, v, r, lw1, kd1, b1, kk, v, r, s0)


def _rwkv_post_kernel(yf_ref, yb_ref, bv_ref, g_ref, lng_ref, lnb_ref, bd_ref, o_ref):
    npair = yf_ref.shape[1]
    y = jnp.concatenate([yf_ref[0, p] + yb_ref[0, p] for p in range(npair)], axis=-1)
    bd = bd_ref[...]
    inv = 1.0 / HEAD_DIM
    m = _segsum(y, bd) * inv
    dlt = y - m
    var = _segsum(dlt * dlt, bd) * inv
    yn = dlt * lax.rsqrt(var + RWKV_GN_EPS) * lng_ref[...] + lnb_ref[...]
    o_ref[0] = ((yn + bv_ref[0]) * g_ref[0]).astype(o_ref.dtype)


def _rwkv_post(yf, yb, bv, g, ln_g, ln_b, *, t_rows):
    nb, npair, l, _ = yf.shape
    w = npair * LANES
    pair_spec = pl.BlockSpec((1, npair, t_rows, LANES), lambda b, i: (b, 0, i, 0))
    wide_spec = pl.BlockSpec((1, t_rows, w), lambda b, i: (b, i, 0))
    const2 = lambda b, i: (0, 0)
    return pl.pallas_call(
        _rwkv_post_kernel,
        grid=(nb, l // t_rows),
        in_specs=[pair_spec, pair_spec, wide_spec, wide_spec,
                  pl.BlockSpec((1, w), const2), pl.BlockSpec((1, w), const2),
                  pl.BlockSpec((LANES, LANES), const2)],
        out_specs=wide_spec,
        out_shape=jax.ShapeDtypeStruct((nb, l, w), BF16),
        compiler_params=_params("arbitrary", "arbitrary"),
        name="rwkv_post",
    )(yf, yb, bv, g, ln_g.reshape(1, w), ln_b.reshape(1, w), _head_blockdiag(LANES))


def _merge_kernel(h_ref, mod_ref, g_ref, yp_ref, yr_ref, yg_ref, wzg_ref, pp_ref, pr_ref, pg_ref,
                  wo_ref, o_ref, hn_ref):
    n = pl.program_id(2)

    @pl.when(n == 0)
    def _():
        hn = _rms(h_ref[0], g_ref[...]) * (1.0 + mod_ref[0, 4:5, :]) + mod_ref[0, 3:4, :]
        hn_ref[...] = hn.astype(BF16)
        o_ref[0] = h_ref[0]

    hn = hn_ref[...]
    merged = None
    for br, (y_ref, p_ref) in enumerate(((yp_ref, pp_ref), (yr_ref, pr_ref), (yg_ref, pg_ref))):
        gate = _sigmoid(jnp.dot(hn, wzg_ref[br], preferred_element_type=F32))
        term = gate * jnp.dot(y_ref[0], p_ref[...], preferred_element_type=F32)
        merged = term if merged is None else merged + term
    o_ref[0] += mod_ref[0, 5:6, :] * jnp.dot(merged.astype(BF16), wo_ref[...],
                                             preferred_element_type=F32)


def _merge(h, mod, g, yp, yr, yg, wzg, pp, pr, pg, wo, *, li, tm, tn):
    nb, l, d = h.shape
    per_batch = mod.shape[0] > 1
    mod_map = (lambda b, i, n: (b, 0, 0)) if per_batch else (lambda b, i, n: (0, 0, 0))
    row = lambda width: pl.BlockSpec((1, tm, width), lambda b, i, n: (b, i, 0))
    col = lambda a: pl.BlockSpec((None, a.shape[1], tn), lambda b, i, n: (li, 0, n))
    return pl.pallas_call(
        _merge_kernel,
        grid=(nb, l // tm, d // tn),
        in_specs=[row(d), pl.BlockSpec((1, N_MOD, d), mod_map),
                  pl.BlockSpec((1, d), lambda b, i, n: (0, 0)),
                  row(yp.shape[2]), row(yr.shape[2]), row(yg.shape[2]),
                  pl.BlockSpec((wzg.shape[0], d, tn), lambda b, i, n: (0, 0, n)),
                  col(pp), col(pr), col(pg),
                  pl.BlockSpec((None, tn, d), lambda b, i, n: (li, n, 0))],
        out_specs=row(d),
        out_shape=jax.ShapeDtypeStruct((nb, l, d), F32),
        scratch_shapes=[pltpu.VMEM((tm, d), BF16)],
        compiler_params=_params("arbitrary", "arbitrary", "arbitrary"),
        name="merge",
    )(h, mod, g.reshape(1, d), yp, yr, yg, wzg, pp, pr, pg, wo)


def _pick(n, pref):
    t = min(pref, n)
    while t > LANES and (n % t or t % LANES):
        t -= LANES
    return t if n % t == 0 else n


def kernel(x, c, ctx, c_ctx, ada_w, ada_b, norm_g, ffn_w_gate, ffn_w_up, ffn_w_down, w_in,
           pool_w, pool_scale, rwkv_mu, rwkv_w0, rwkv_w_up, rwkv_a0, rwkv_a_up, rwkv_g_up,
           rwkv_k_k, rwkv_k_a, rwkv_r_k, rwkv_ln_g, rwkv_ln_b, gmlp_ln_g, gmlp_ln_b, gmlp_ws,
           gmlp_bs, proj_pool, proj_rwkv, proj_gmlp, w_out, final_norm):
    depth = ada_w.shape[0]
    nb, l, d = x.shape
    lc = ctx.shape[1]
    width = rwkv_w0.shape[-1]
    pool_c = pool_scale.shape[-1]
    gmlp_c = gmlp_ln_g.shape[-1]
    rwkv_cols = rwkv_mu.shape[-1]
    off_rwkv = pool_c
    off_gmlp = off_rwkv + rwkv_cols
    off_gate = off_gmlp + 2 * gmlp_c
    npair = width // LANES

    cond8 = jnp.zeros((8, d), F32).at[:nb].set(c).at[nb].set(c_ctx)
    mod_all = _adaln(cond8, ada_w, ada_b).reshape(depth, 8, N_MOD, d)

    zero_state = jnp.zeros((2, nb, npair, LANES, LANES), F32)
    wg, wu, wd = (a.astype(BF16) for a in (ffn_w_gate, ffn_w_up, ffn_w_down))
    pp, pr, pg, wo = (a.astype(BF16) for a in (proj_pool, proj_rwkv, proj_gmlp, w_out))
    win_all = w_in.astype(BF16)
    assert (2 * gmlp_c) % pool_c == 0
    hc = ctx.reshape(1, nb * lc, d)
    for li in range(depth):
        last = li == depth - 1
        mod_x = mod_all[li, :nb]
        mod_c = mod_all[li, nb:nb + 1]
        win = win_all[li]
        w_zb = win[:, off_rwkv:off_gmlp]
        w_uva = jnp.concatenate([win[:, off_gmlp:off_gate], win[:, :off_rwkv]], axis=1)
        w_zg = win[:, off_gate:].reshape(d, -1, d).transpose(1, 0, 2)
        lp = {'mu': rwkv_mu[li], 'w0': rwkv_w0[li], 'w_up': rwkv_w_up[li], 'a0': rwkv_a0[li],
              'a_up': rwkv_a_up[li], 'g_up': rwkv_g_up[li], 'k_k': rwkv_k_k[li],
              'k_a': rwkv_k_a[li], 'r_k': rwkv_r_k[li]}
        tf = _pick(wg.shape[-1], 512)

        def mixer(h, mod, nseq, seq, grid_mode, s0, need_out):
            flat = h.shape[:2]
            tm = _pick(flat[1], 1024)
            zb = _norm_mm(h, mod, norm_g[li, 1], w_zb, tm=tm, tn=_pick(rwkv_cols, 1152))
            prep = _rwkv_prep(zb.reshape(nseq, seq, -1), lp, t_rows=_pick(seq, 256),
                              grid_mode=grid_mode)
            yf, yb, s_fin = _rwkv_scan(prep[:9], s0)
            if not need_out:
                return None, s_fin
            y_rwkv = _rwkv_post(yf, yb, prep[10], prep[9], rwkv_ln_g[li], rwkv_ln_b[li],
                                t_rows=_pick(seq, 256))
            z_uva = _norm_mm(h, mod, norm_g[li, 1], w_uva, tm=tm,
                             tn=_pick(w_uva.shape[1], 768)).reshape(nseq, seq, -1)
            y_pool = _pool(z_uva, pool_w[li], pool_scale[li], col_block=2 * gmlp_c // pool_c,
                           t_rows=_pick(seq, 512))
            y_gmlp = _gmlp(z_uva, gmlp_ln_g[li], gmlp_ln_b[li], gmlp_ws[li], gmlp_bs[li],
                           t_rows=_pick(seq, 512))
            ys = [y.reshape(flat + (-1,)) for y in (y_pool, y_rwkv, y_gmlp)]
            out = _merge(h, mod, norm_g[li, 1], *ys, w_zg, pp, pr, pg, wo,
                         li=li, tm=_pick(flat[1], 512), tn=_pick(d, 512))
            return out, s_fin

        def ffn(h, mod, k, final=False):
            return _ffn(h, mod, norm_g[li, 2 * k], wg, wu, wd, final_norm, sel=(li, k),
                        base=6 * k, final=final, tm=_pick(h.shape[1], 512), tf=tf)

        hc = ffn(hc, mod_c, 0)
        hc_new, s_ctx = mixer(hc, mod_c, nb, lc, False, zero_state, not last)
        if not last:
            hc = ffn(hc_new, mod_c, 1)

        x = ffn(x, mod_x, 0)
        x, _ = mixer(x, mod_x, nb, l, True, s_ctx, True)
        x = ffn(x, mod_x, 1, final=last)
    return x
```

```python
import functools
import math

import jax
import jax.numpy as jnp
from jax import lax
from jax.experimental import pallas as pl
from jax.experimental.pallas import tpu as pltpu

F32 = jnp.float32
BF16 = jnp.bfloat16
HIGHEST = lax.Precision.HIGHEST

NORM_EPS = 1e-6
LN_EPS = 1e-5
RWKV_GN_EPS = 64e-5
GRID_W = 64
N_MOD = 9
POOL_WINDOWS = (2, 4, 8, 16)
HEAD_DIM = 64
GMLP_CHUNK = 128
LANES = 128
SCAN_CHUNK = 64
VMEM_LIMIT = 56 * 1024 * 1024


def _params(*sem):
    return pltpu.CompilerParams(dimension_semantics=sem, vmem_limit_bytes=VMEM_LIMIT)


def _sigmoid(x):
    return 0.5 * jnp.tanh(0.5 * x) + 0.5


def _mm(a, b):
    return jnp.dot(a.astype(BF16), b.astype(BF16), preferred_element_type=F32)


def _mm_hi(a, b):
    return jnp.dot(a, b, precision=HIGHEST, preferred_element_type=F32)


def _mm_nt_hi(a, b):
    return lax.dot_general(a, b, (((1,), (1,)), ((), ())), precision=HIGHEST,
                           preferred_element_type=F32)


def _mm_tn_hi(a, b):
    return lax.dot_general(a, b, (((0,), (0,)), ((), ())), precision=HIGHEST,
                           preferred_element_type=F32)


def _rms(x, g):
    return x * lax.rsqrt(jnp.mean(x * x, axis=-1, keepdims=True) + NORM_EPS) * g


def _adaln_kernel(c_ref, w_ref, b_ref, o_ref):
    c = c_ref[...]
    o_ref[0] = _mm_hi(c * _sigmoid(c), w_ref[0]) + b_ref[0]


def _adaln(cond8, ada_w, ada_b):
    depth, d, n = ada_w.shape
    tn = 1024
    return pl.pallas_call(
        _adaln_kernel,
        grid=(depth, n // tn),
        in_specs=[pl.BlockSpec((8, d), lambda l, j: (0, 0)),
                  pl.BlockSpec((1, d, tn), lambda l, j: (l, 0, j)),
                  pl.BlockSpec((1, 1, tn), lambda l, j: (l, 0, j))],
        out_specs=pl.BlockSpec((1, 8, tn), lambda l, j: (l, 0, j)),
        out_shape=jax.ShapeDtypeStruct((depth, 8, n), F32),
        compiler_params=_params("arbitrary", "arbitrary"),
        name="adaln",
    )(cond8, ada_w, ada_b.reshape(depth, 1, n))


def _ffn_kernel(h_ref, mod_ref, g_ref, wg_ref, wu_ref, wd_ref, fn_ref, o_ref, hn_ref,
                *, base, final):
    j = pl.program_id(2)

    @pl.when(j == 0)
    def _():
        hn = _rms(h_ref[0], g_ref[...]) * (1.0 + mod_ref[0, base + 1:base + 2, :]) \
            + mod_ref[0, base:base + 1, :]
        hn_ref[...] = hn.astype(BF16)
        o_ref[0] = h_ref[0]

    hn = hn_ref[...]
    gate = jnp.dot(hn, wg_ref[...], preferred_element_type=F32)
    up = jnp.dot(hn, wu_ref[...], preferred_element_type=F32)
    act = (gate * _sigmoid(gate) * up).astype(BF16)
    o_ref[0] += (0.5 * mod_ref[0, base + 2:base + 3, :]) * jnp.dot(
        act, wd_ref[...], preferred_element_type=F32)

    if final:
        @pl.when(j == pl.num_programs(2) - 1)
        def _():
            o_ref[0] = _rms(o_ref[0], fn_ref[...])


def _ffn(h, mod, g, wg, wu, wd, fn, *, sel, base, final, tm):
    nb, l, d = h.shape
    tf = wg.shape[-1]
    f = wd.shape[-2]
    li, k = sel
    per_batch = mod.shape[0] > 1
    mod_map = (lambda b, i, j: (b, 0, 0)) if per_batch else (lambda b, i, j: (0, 0, 0))
    return pl.pallas_call(
        functools.partial(_ffn_kernel, base=base, final=final),
        grid=(nb, l // tm, f // tf),
        in_specs=[pl.BlockSpec((1, tm, d), lambda b, i, j: (b, i, 0)),
                  pl.BlockSpec((1, N_MOD, d), mod_map),
                  pl.BlockSpec((1, d), lambda b, i, j: (0, 0)),
                  pl.BlockSpec((None, None, None, d, tf), lambda b, i, j: (li, k, j, 0, 0)),
                  pl.BlockSpec((None, None, None, d, tf), lambda b, i, j: (li, k, j, 0, 0)),
                  pl.BlockSpec((None, None, tf, d), lambda b, i, j: (li, k, j, 0)),
                  pl.BlockSpec((1, d), lambda b, i, j: (0, 0))],
        out_specs=pl.BlockSpec((1, tm, d), lambda b, i, j: (b, i, 0)),
        out_shape=jax.ShapeDtypeStruct((nb, l, d), F32),
        scratch_shapes=[pltpu.VMEM((tm, d), BF16)],
        compiler_params=_params("arbitrary", "arbitrary", "arbitrary"),
        name="ffn",
    )(h, mod, g.reshape(1, d), wg, wu, wd, fn.reshape(1, d))


def _norm_mm_kernel(h_ref, mod_ref, g_ref, w_ref, o_ref, hn_ref):
    @pl.when(pl.program_id(2) == 0)
    def _():
        hn = _rms(h_ref[0], g_ref[...]) * (1.0 + mod_ref[0, 4:5, :]) + mod_ref[0, 3:4, :]
        hn_ref[...] = hn.astype(BF16)

    o_ref[0] = jnp.dot(hn_ref[...], w_ref[...], preferred_element_type=F32)


def _norm_mm(h, mod, g, w, *, tm):
    nb, l, d = h.shape
    tn = w.shape[-1]
    n = w.shape[0] * tn
    per_batch = mod.shape[0] > 1
    mod_map = (lambda b, i, j: (b, 0, 0)) if per_batch else (lambda b, i, j: (0, 0, 0))
    return pl.pallas_call(
        _norm_mm_kernel,
        grid=(nb, l // tm, n // tn),
        in_specs=[pl.BlockSpec((1, tm, d), lambda b, i, j: (b, i, 0)),
                  pl.BlockSpec((1, N_MOD, d), mod_map),
                  pl.BlockSpec((1, d), lambda b, i, j: (0, 0)),
                  pl.BlockSpec((None, d, tn), lambda b, i, j: (j, 0, 0))],
        out_specs=pl.BlockSpec((1, tm, tn), lambda b, i, j: (b, i, j)),
        out_shape=jax.ShapeDtypeStruct((nb, l, n), F32),
        scratch_shapes=[pltpu.VMEM((tm, d), BF16)],
        compiler_params=_params("arbitrary", "arbitrary", "arbitrary"),
        name="norm_mm",
    )(h, mod, g.reshape(1, d), w)


POOL_HALO = 8


def _pool_kernel(prev_ref, x_ref, next_ref, pw_ref, ps_ref, o_ref, scr, *, t_rows, seq):
    i = pl.program_id(1)
    last = pl.num_programs(1) - 1
    hal = POOL_HALO
    scr[0:hal] = jnp.where(i > 0, prev_ref[0], 0.0)
    scr[hal:hal + t_rows] = x_ref[0]
    scr[hal + t_rows:2 * hal + t_rows] = jnp.where(i < last, next_ref[0], 0.0)
    t = i * t_rows + lax.broadcasted_iota(jnp.int32, (t_rows, LANES), 0)
    for gi, w in enumerate(POOL_WINDOWS):
        c0 = gi * LANES
        s = scr[hal - w // 2:hal - w // 2 + t_rows, c0:c0 + LANES]
        for dlt in range(-(w // 2) + 1, w // 2):
            s = s + scr[hal + dlt:hal + dlt + t_rows, c0:c0 + LANES]
        lo = jnp.clip(t - w // 2, 0, seq)
        hi = jnp.clip(t - w // 2 + w, 0, seq)
        cnt = (hi - lo).astype(F32)
        p = s / cnt - scr[hal:hal + t_rows, c0:c0 + LANES]
        y = _mm(p, pw_ref[gi]) * ps_ref[:, c0:c0 + LANES]
        o_ref[0, :, c0:c0 + LANES] = y.astype(o_ref.dtype)


def _pool(z, pool_w, pool_scale, *, col_block, t_rows):
    nb, l, _ = z.shape
    c = pool_scale.shape[-1]
    hal = POOL_HALO
    r = t_rows // hal
    nh = l // hal
    cb = col_block
    return pl.pallas_call(
        functools.partial(_pool_kernel, t_rows=t_rows, seq=l),
        grid=(nb, l // t_rows),
        in_specs=[pl.BlockSpec((1, hal, c), lambda b, i: (b, jnp.maximum(i * r - 1, 0), cb)),
                  pl.BlockSpec((1, t_rows, c), lambda b, i: (b, i, cb)),
                  pl.BlockSpec((1, hal, c), lambda b, i: (b, jnp.minimum((i + 1) * r, nh - 1), cb)),
                  pl.BlockSpec(pool_w.shape, lambda b, i: (0, 0, 0)),
                  pl.BlockSpec((1, c), lambda b, i: (0, 0))],
        out_specs=pl.BlockSpec((1, t_rows, c), lambda b, i: (b, i, 0)),
        out_shape=jax.ShapeDtypeStruct((nb, l, c), BF16),
        scratch_shapes=[pltpu.VMEM((t_rows + 2 * hal, c), F32)],
        compiler_params=_params("arbitrary", "arbitrary"),
        name="pool",
    )(z, z, z, pool_w.astype(BF16), pool_scale.reshape(1, c))


def _gelu(x):
    return x * (0.5 * (1.0 + jnp.tanh(math.sqrt(2.0 / math.pi) * (x + 0.044715 * (x * x * x)))))


def _gmlp_kernel(z_ref, lng_ref, lnb_ref, ws_ref, bsb_ref, o_ref, *, t_rows):
    width = o_ref.shape[2]
    groups = ws_ref.shape[0]
    gd = width // groups
    for ci in range(t_rows // GMLP_CHUNK):
        rows = slice(ci * GMLP_CHUNK, (ci + 1) * GMLP_CHUNK)
        u = _gelu(z_ref[0, rows, 0:width])
        v = _gelu(z_ref[0, rows, width:2 * width])
        mu = jnp.mean(v, axis=-1, keepdims=True)
        var = jnp.mean(jnp.square(v - mu), axis=-1, keepdims=True)
        vn = ((v - mu) * lax.rsqrt(var + LN_EPS)) * lng_ref[...] + lnb_ref[...]
        for g in range(groups):
            cols = slice(g * gd, (g + 1) * gd)
            s = _mm(ws_ref[g], vn[:, cols]) + bsb_ref[g]
            o_ref[0, rows, cols] = (u[:, cols] * s).astype(o_ref.dtype)


def _gmlp(zuv, ln_g, ln_b, ws, bs, *, t_rows):
    nb, l, _ = zuv.shape
    c = ln_g.shape[-1]
    c2 = 2 * c
    groups = ws.shape[0]
    bsb = jnp.broadcast_to(bs[:, :, None], (groups, GMLP_CHUNK, c // groups))
    return pl.pallas_call(
        functools.partial(_gmlp_kernel, t_rows=t_rows),
        grid=(nb, l // t_rows),
        in_specs=[pl.BlockSpec((1, t_rows, c2), lambda b, i: (b, i, 0)),
                  pl.BlockSpec((1, c), lambda b, i: (0, 0)),
                  pl.BlockSpec((1, c), lambda b, i: (0, 0)),
                  pl.BlockSpec(ws.shape, lambda b, i: (0, 0, 0)),
                  pl.BlockSpec(bsb.shape, lambda b, i: (0, 0, 0))],
        out_specs=pl.BlockSpec((1, t_rows, c), lambda b, i: (b, i, 0)),
        out_shape=jax.ShapeDtypeStruct((nb, l, c), BF16),
        compiler_params=_params("arbitrary", "arbitrary"),
        name="gmlp",
    )(zuv, ln_g.reshape(1, c), ln_b.reshape(1, c), ws.astype(BF16), bsb)


def _segsum(x, bd):
    hi = x.astype(BF16)
    lo = (x - hi.astype(F32)).astype(BF16)
    out = []
    for p in range(x.shape[-1] // LANES):
        cols = slice(p * LANES, (p + 1) * LANES)
        out.append(jnp.dot(hi[:, cols], bd, preferred_element_type=F32)
                   + jnp.dot(lo[:, cols], bd, preferred_element_type=F32))
    return jnp.concatenate(out, axis=-1)


def _rwkv_prep_kernel(prev_ref, x_ref, next_ref, mu_ref, w0_ref, wup_ref, a0_ref, aup_ref,
                      gup_ref, kk_ref, ka_ref, rk_ref, bd_ref,
                      lw0_o, lw1_o, kd0_o, kd1_o, b0_o, b1_o, kk_o, v_o, r_o, g_o, bv_o,
                      scr, *, t_rows, grid_mode, width):
    i = pl.program_id(1)
    last = pl.num_programs(1) - 1
    hal = GRID_W
    scr[0:hal] = jnp.where(i > 0, prev_ref[0], 0.0)
    scr[hal:hal + t_rows] = x_ref[0]
    scr[hal + t_rows:2 * hal + t_rows] = jnp.where(i < last, next_ref[0], 0.0)
    cols = scr.shape[1]
    if grid_mode:
        q = cols // 4
        bounds = (0, q, 2 * q, 3 * q, cols)
        offs = (-1, 1, -GRID_W, GRID_W)
    else:
        bounds = (0, cols // 2, cols)
        offs = (-1, 1)

    def zs_cols(c0, c1):
        n = c1 - c0
        x = scr[hal:hal + t_rows, c0:c1]
        ch = c0 + lax.broadcasted_iota(jnp.int32, (t_rows, n), 1)
        col = lax.broadcasted_iota(jnp.int32, (t_rows, n), 0) & (GRID_W - 1)
        shifted = None
        for qi, off in enumerate(offs):
            lo, hi = bounds[qi], bounds[qi + 1]
            if hi <= c0 or lo >= c1:
                continue
            src = scr[hal + off:hal + off + t_rows, c0:c1]
            if grid_mode and off == -1:
                src = jnp.where(col == 0, 0.0, src)
            if grid_mode and off == 1:
                src = jnp.where(col == GRID_W - 1, 0.0, src)
            shifted = src if shifted is None else jnp.where(ch >= lo, src, shifted)
        return x + (shifted - x) * mu_ref[:, c0:c1]

    w = width
    r = zs_cols(0, w)
    k = zs_cols(w, 2 * w)
    v = zs_cols(2 * w, 3 * w)
    rest = zs_cols(3 * w, cols)
    wd = jnp.tanh(rest[:, 0:LANES])
    ad = rest[:, LANES:2 * LANES]
    gd = _sigmoid(rest[:, 2 * LANES:3 * LANES])

    bd = bd_ref[...]
    kk0 = k * kk_ref[...]
    ss = _segsum(kk0 * kk0, bd)
    kk = kk0 / jnp.maximum(jnp.sqrt(ss), 1e-12)
    npair = w // LANES

    def put(o_ref, val):
        for p in range(npair):
            o_ref[0, p] = val[:, p * LANES:(p + 1) * LANES]

    put(kk_o, kk)
    put(v_o, v)
    put(r_o, r)
    kd_sum = None
    for d, (lw_o, kd_o, b_o) in enumerate(((lw0_o, kd0_o, b0_o), (lw1_o, kd1_o, b1_o))):
        w_pre = w0_ref[d] + _mm(wd, wup_ref[d])
        put(lw_o, -math.exp(-0.5) * _sigmoid(w_pre))
        a = _sigmoid(a0_ref[d] + _mm(ad, aup_ref[d]))
        kd = k * (1.0 + (a - 1.0) * ka_ref[...])
        put(kd_o, kd)
        put(b_o, kk * a)
        kd_sum = kd if kd_sum is None else kd_sum + kd
    g_o[0] = _mm(gd, gup_ref[...])
    bv_o[0] = _segsum(r * kd_sum * rk_ref[...], bd) * v


def _head_blockdiag(width):
    idx = jnp.arange(width) // HEAD_DIM
    return (idx[:, None] == idx[None, :]).astype(BF16)


def _rwkv_prep(zb, lp, *, t_rows, grid_mode):
    nb, l, cols = zb.shape
    w = lp['w0'].shape[-1]
    npair = w // LANES
    hal = GRID_W
    r = t_rows // hal
    nh = l // hal
    lora = lp['w_up'].shape[1]

    def pad_dir(up):
        z = jnp.zeros_like(up[0])
        return jnp.stack([jnp.concatenate([up[0], z], 0), jnp.concatenate([z, up[1]], 0)], 0)

    const2 = lambda b, i: (0, 0)
    const3 = lambda b, i: (0, 0, 0)
    pair_spec = pl.BlockSpec((1, npair, t_rows, LANES), lambda b, i: (b, 0, i, 0))
    wide_spec = pl.BlockSpec((1, t_rows, w), lambda b, i: (b, i, 0))
    pair_shape = jax.ShapeDtypeStruct((nb, npair, l, LANES), F32)
    wide_shape = jax.ShapeDtypeStruct((nb, l, w), F32)
    return pl.pallas_call(
        functools.partial(_rwkv_prep_kernel, t_rows=t_rows, grid_mode=grid_mode, width=w),
        grid=(nb, l // t_rows),
        in_specs=[pl.BlockSpec((1, hal, cols), lambda b, i: (b, jnp.maximum(i * r - 1, 0), 0)),
                  pl.BlockSpec((1, t_rows, cols), lambda b, i: (b, i, 0)),
                  pl.BlockSpec((1, hal, cols), lambda b, i: (b, jnp.minimum((i + 1) * r, nh - 1), 0)),
                  pl.BlockSpec((1, cols), const2),
                  pl.BlockSpec((2, 1, w), const3),
                  pl.BlockSpec((2, 2 * lora, w), const3),
                  pl.BlockSpec((2, 1, w), const3),
                  pl.BlockSpec((2, 2 * lora, w), const3),
                  pl.BlockSpec(lp['g_up'].shape, const2),
                  pl.BlockSpec((1, w), const2),
                  pl.BlockSpec((1, w), const2),
                  pl.BlockSpec((1, w), const2),
                  pl.BlockSpec((LANES, LANES), const2)],
        out_specs=[pair_spec] * 9 + [wide_spec] * 2,
        out_shape=[pair_shape] * 9 + [wide_shape] * 2,
        scratch_shapes=[pltpu.VMEM((t_rows + 2 * hal, cols), F32)],
        compiler_params=_params("arbitrary", "arbitrary"),
        name="rwkv_prep",
    )(zb, zb, zb, lp['mu'].reshape(1, cols), lp['w0'].reshape(2, 1, w),
      pad_dir(lp['w_up']).astype(BF16), lp['a0'].reshape(2, 1, w),
      pad_dir(lp['a_up']).astype(BF16), lp['g_up'].astype(BF16), lp['k_k'].reshape(1, w),
      lp['k_a'].reshape(1, w), lp['r_k'].reshape(1, w), _head_blockdiag(LANES))


def _stack2(x, m0):
    return jnp.concatenate([jnp.where(m0, x, 0.0), jnp.where(m0, 0.0, x)], axis=0)


_NN = (((1,), (0,)), ((), ()))
_NT = (((1,), (1,)), ((), ()))
_TN = (((0,), (0,)), ((), ()))

SCAN_PREC = {"cum": "rhs2", "gram": "bf16", "init": "bf16", "apply": "bf16", "out": "bf16",
             "state": "bf16"}
SCAN_LOCKSTEP_PAIRS = 8

def _split(a):
    hi = a.astype(BF16)
    return hi, (a - hi.astype(F32)).astype(BF16)


def _dg(a, b, dims, site):
    mode = SCAN_PREC[site]
    if mode == "f32":
        return lax.dot_general(a, b, dims, precision=HIGHEST, preferred_element_type=F32)
    if mode == "bf16":
        return lax.dot_general(a.astype(BF16), b.astype(BF16), dims, preferred_element_type=F32)
    b_hi, b_lo = _split(b)
    if mode == "rhs2":
        a16 = a.astype(BF16)
        return (lax.dot_general(a16, b_hi, dims, preferred_element_type=F32)
                + lax.dot_general(a16, b_lo, dims, preferred_element_type=F32))
    a_hi, a_lo = _split(a)
    return (lax.dot_general(a_hi, b_hi, dims, preferred_element_type=F32)
            + lax.dot_general(a_hi, b_lo, dims, preferred_element_type=F32)
            + lax.dot_general(a_lo, b_hi, dims, preferred_element_type=F32))


def _chunk_step(refs, s_ref, *, rev, tri, strict, incl, diag, m0):
    lw, kd, b, kk, v, r = (x[...] for x in refs)
    s = s_ref[...]
    c = lw.shape[0]
    cum = _dg(tri, lw, _NN, "cum")
    yield
    cum_prev = cum - lw
    end = 0 if rev else c - 1
    tot = cum[end:end + 1, :]
    mid = cum[c // 2:c // 2 + 1, :]
    e_inv = jnp.exp(mid - cum)
    nkk = -kk
    left = jnp.concatenate([_stack2(nkk * jnp.exp(cum_prev - mid), m0),
                            _stack2(r * jnp.exp(cum - mid), m0)], axis=0)
    right = jnp.concatenate([_stack2(b * e_inv, m0), _stack2(kd * e_inv, m0)], axis=0)
    gram = _dg(left, right, _NT, "gram")
    yield
    c2 = 2 * c
    a_ab = jnp.where(strict, gram[0:c2, 0:c2], 0.0)
    a_ak = jnp.where(strict, gram[0:c2, c2:2 * c2], 0.0)
    a_rb = jnp.where(incl, gram[c2:2 * c2, 0:c2], 0.0)
    a_rk = jnp.where(incl, gram[c2:2 * c2, c2:2 * c2], 0.0)
    vsw = pltpu.roll(v, HEAD_DIM, 1)
    vbd = jnp.concatenate([jnp.where(m0, 0.0, vsw), jnp.where(m0, vsw, 0.0)], axis=0)
    e_end = jnp.exp(tot - cum)
    x = (_dg(_stack2(nkk * jnp.exp(cum_prev), m0), s, _NT, "init")
         + _dg(a_ak, vbd, _NN, "init"))
    ybd = _dg(_stack2(r * jnp.exp(cum), m0), s, _NT, "out") + _dg(a_rk, vbd, _NN, "out")
    s_new = s * jnp.exp(tot) + _dg(vbd, _stack2(kd * e_end, m0), _TN, "state")
    steps = max(1, (c - 1).bit_length())
    pw = a_ab
    for it in range(steps):
        yield
        if it < steps - 1:
            both = _dg(pw, pw + x, _NN, "apply")
            x = x + jnp.where(diag, 0.0, both)
            pw = jnp.where(diag, both, 0.0)
        else:
            x = x + _dg(pw, x, _NN, "apply")
    yield
    u = x
    ybd = ybd + _dg(a_rb, u, _NN, "out")
    y = pltpu.roll(ybd[0:c] + ybd[c:c2], HEAD_DIM, 1)
    s_new = s_new + _dg(u, _stack2(b * e_end, m0), _TN, "state")
    return y, s_new


def _lockstep(gens):
    results = [None] * len(gens)
    live = list(range(len(gens)))
    while live:
        still = []
        for i in live:
            try:
                next(gens[i])
                still.append(i)
            except StopIteration as stop:
                results[i] = stop.value
        live = still
    return results


def _rwkv_scan_kernel(lw0, kd0, b0, kkf, vf, rf, lw1, kd1, b1, kkb, vb, rb, s0_ref,
                      yf_o, yb_o, sfin_o, st):
    ci = pl.program_id(1)
    c = lw0.shape[2]
    npair = lw0.shape[1]

    @pl.when(ci == 0)
    def _():
        st[...] = s0_ref[:, 0]

    row = lax.broadcasted_iota(jnp.int32, (c, c), 0)
    colm = lax.broadcasted_iota(jnp.int32, (c, c), 1)
    tri_f = (colm <= row).astype(F32)
    tri_b = (colm >= row).astype(F32)
    rows2 = lax.broadcasted_iota(jnp.int32, (2 * c, 2 * c), 0)
    cols2 = lax.broadcasted_iota(jnp.int32, (2 * c, 2 * c), 1)
    r2 = rows2 & (c - 1)
    c2 = cols2 & (c - 1)
    diag = (rows2 < c) == (cols2 < c)
    m0 = lax.broadcasted_iota(jnp.int32, (1, LANES), 1) < HEAD_DIM

    fwd_refs = (lw0, kd0, b0, kkf, vf, rf)
    bwd_refs = (lw1, kd1, b1, kkb, vb, rb)
    for p0 in range(0, npair, SCAN_LOCKSTEP_PAIRS):
        gens, outs = [], []
        for p in range(p0, min(p0 + SCAN_LOCKSTEP_PAIRS, npair)):
            gens.append(_chunk_step([x.at[0, p] for x in fwd_refs], st.at[0, p], rev=False,
                                    tri=tri_f, strict=c2 < r2, incl=c2 <= r2, diag=diag, m0=m0))
            outs.append((yf_o, 0, p))
            gens.append(_chunk_step([x.at[0, p] for x in bwd_refs], st.at[1, p], rev=True,
                                    tri=tri_b, strict=c2 > r2, incl=c2 >= r2, diag=diag, m0=m0))
            outs.append((yb_o, 1, p))
        for (y_o, d, p), (y, s_new) in zip(outs, _lockstep(gens)):
            y_o[0, p] = y
            st[d, p] = s_new

    @pl.when(ci == pl.num_programs(1) - 1)
    def _():
        sfin_o[:, 0] = st[...]


def _rwkv_scan(prep, s0):
    lw0, lw1, kd0, kd1, b0, b1, kk, v, r = prep
    nb, npair, l, _ = lw0.shape
    c = SCAN_CHUNK
    assert c == HEAD_DIM and 2 * HEAD_DIM == LANES and l % c == 0
    n = l // c
    fwd = pl.BlockSpec((1, npair, c, LANES), lambda b, i: (b, 0, i, 0))
    bwd = pl.BlockSpec((1, npair, c, LANES), lambda b, i: (b, 0, n - 1 - i, 0))
    st_spec = pl.BlockSpec((2, 1, npair, LANES, LANES), lambda b, i: (0, b, 0, 0, 0))
    y_shape = jax.ShapeDtypeStruct((nb, npair, l, LANES), F32)
    return pl.pallas_call(
        _rwkv_scan_kernel,
        grid=(nb, n),
        in_specs=[fwd] * 6 + [bwd] * 6 + [st_spec],
        out_specs=[fwd, bwd, st_spec],
        out_shape=[y_shape, y_shape, jax.ShapeDtypeStruct(s0.shape, F32)],
        scratch_shapes=[pltpu.VMEM((2, npair, LANES, LANES), F32)],
        compiler_params=_params("arbitrary", "arbitrary"),
        name="rwkv_scan",
    )(lw0, kd0, b0, kk, v, r, lw1, kd1, b1, kk, v, r, s0)


def _rwkv_post_kernel(yf_ref, yb_ref, bv_ref, g_ref, lng_ref, lnb_ref, bd_ref, o_ref):
    npair = yf_ref.shape[1]
    y = jnp.concatenate([yf_ref[0, p] + yb_ref[0, p] for p in range(npair)], axis=-1)
    bd = bd_ref[...]
    inv = 1.0 / HEAD_DIM
    m = _segsum(y, bd) * inv
    dlt = y - m
    var = _segsum(dlt * dlt, bd) * inv
    yn = dlt * lax.rsqrt(var + RWKV_GN_EPS) * lng_ref[...] + lnb_ref[...]
    o_ref[0] = ((yn + bv_ref[0]) * g_ref[0]).astype(o_ref.dtype)


def _rwkv_post(yf, yb, bv, g, ln_g, ln_b, *, t_rows):
    nb, npair, l, _ = yf.shape
    w = npair * LANES
    pair_spec = pl.BlockSpec((1, npair, t_rows, LANES), lambda b, i: (b, 0, i, 0))
    wide_spec = pl.BlockSpec((1, t_rows, w), lambda b, i: (b, i, 0))
    const2 = lambda b, i: (0, 0)
    return pl.pallas_call(
        _rwkv_post_kernel,
        grid=(nb, l // t_rows),
        in_specs=[pair_spec, pair_spec, wide_spec, wide_spec,
                  pl.BlockSpec((1, w), const2), pl.BlockSpec((1, w), const2),
                  pl.BlockSpec((LANES, LANES), const2)],
        out_specs=wide_spec,
        out_shape=jax.ShapeDtypeStruct((nb, l, w), BF16),
        compiler_params=_params("arbitrary", "arbitrary"),
        name="rwkv_post",
    )(yf, yb, bv, g, ln_g.reshape(1, w), ln_b.reshape(1, w), _head_blockdiag(LANES))


def _merge_kernel(h_ref, mod_ref, g_ref, yp_ref, yr_ref, yg_ref, wzg_ref, pp_ref, pr_ref, pg_ref,
                  wo_ref, o_ref, hn_ref):
    n = pl.program_id(2)

    @pl.when(n == 0)
    def _():
        hn = _rms(h_ref[0], g_ref[...]) * (1.0 + mod_ref[0, 4:5, :]) + mod_ref[0, 3:4, :]
        hn_ref[...] = hn.astype(BF16)
        o_ref[0] = h_ref[0]

    hn = hn_ref[...]
    merged = None
    for br, (y_ref, p_ref) in enumerate(((yp_ref, pp_ref), (yr_ref, pr_ref), (yg_ref, pg_ref))):
        gate = _sigmoid(jnp.dot(hn, wzg_ref[br], preferred_element_type=F32))
        term = gate * jnp.dot(y_ref[0], p_ref[...], preferred_element_type=F32)
        merged = term if merged is None else merged + term
    o_ref[0] += mod_ref[0, 5:6, :] * jnp.dot(merged.astype(BF16), wo_ref[...],
                                             preferred_element_type=F32)


def _merge(h, mod, g, yp, yr, yg, wzg, pp, pr, pg, wo, *, li, tm):
    nb, l, d = h.shape
    tn = wzg.shape[-1]
    per_batch = mod.shape[0] > 1
    mod_map = (lambda b, i, n: (b, 0, 0)) if per_batch else (lambda b, i, n: (0, 0, 0))
    row = lambda width: pl.BlockSpec((1, tm, width), lambda b, i, n: (b, i, 0))
    col = lambda a: pl.BlockSpec((None, None, a.shape[2], tn), lambda b, i, n: (li, n, 0, 0))
    return pl.pallas_call(
        _merge_kernel,
        grid=(nb, l // tm, d // tn),
        in_specs=[row(d), pl.BlockSpec((1, N_MOD, d), mod_map),
                  pl.BlockSpec((1, d), lambda b, i, n: (0, 0)),
                  row(yp.shape[2]), row(yr.shape[2]), row(yg.shape[2]),
                  pl.BlockSpec((None, wzg.shape[1], d, tn), lambda b, i, n: (n, 0, 0, 0)),
                  col(pp), col(pr), col(pg),
                  pl.BlockSpec((None, tn, d), lambda b, i, n: (li, n, 0))],
        out_specs=row(d),
        out_shape=jax.ShapeDtypeStruct((nb, l, d), F32),
        scratch_shapes=[pltpu.VMEM((tm, d), BF16)],
        compiler_params=_params("arbitrary", "arbitrary", "arbitrary"),
        name="merge",
    )(h, mod, g.reshape(1, d), yp, yr, yg, wzg, pp, pr, pg, wo)


def _col_tiles(w, tn):
    lead, (k, n) = w.shape[:-2], w.shape[-2:]
    return jnp.swapaxes(w.reshape(lead + (k, n // tn, tn)), -3, -2)


def _pick(n, pref):
    t = min(pref, n)
    while t > LANES and (n % t or t % LANES):
        t -= LANES
    return t if n % t == 0 else n


def kernel(x, c, ctx, c_ctx, ada_w, ada_b, norm_g, ffn_w_gate, ffn_w_up, ffn_w_down, w_in,
           pool_w, pool_scale, rwkv_mu, rwkv_w0, rwkv_w_up, rwkv_a0, rwkv_a_up, rwkv_g_up,
           rwkv_k_k, rwkv_k_a, rwkv_r_k, rwkv_ln_g, rwkv_ln_b, gmlp_ln_g, gmlp_ln_b, gmlp_ws,
           gmlp_bs, proj_pool, proj_rwkv, proj_gmlp, w_out, final_norm):
    depth = ada_w.shape[0]
    nb, l, d = x.shape
    lc = ctx.shape[1]
    width = rwkv_w0.shape[-1]
    pool_c = pool_scale.shape[-1]
    gmlp_c = gmlp_ln_g.shape[-1]
    rwkv_cols = rwkv_mu.shape[-1]
    off_rwkv = pool_c
    off_gmlp = off_rwkv + rwkv_cols
    off_gate = off_gmlp + 2 * gmlp_c
    npair = width // LANES

    cond8 = jnp.zeros((8, d), F32).at[:nb].set(c).at[nb].set(c_ctx)
    mod_all = _adaln(cond8, ada_w, ada_b).reshape(depth, 8, N_MOD, d)

    zero_state = jnp.zeros((2, nb, npair, LANES, LANES), F32)
    tf = _pick(ffn_w_gate.shape[-1], 512)
    tn_merge = _pick(d, 512)
    wg, wu = (_col_tiles(a.astype(BF16), tf) for a in (ffn_w_gate, ffn_w_up))
    wd = ffn_w_down.astype(BF16)
    pp, pr, pg = (_col_tiles(a.astype(BF16), tn_merge)
                  for a in (proj_pool, proj_rwkv, proj_gmlp))
    wo = w_out.astype(BF16)
    win_all = w_in.astype(BF16)
    assert (2 * gmlp_c) % pool_c == 0
    hc = ctx.reshape(1, nb * lc, d)
    for li in range(depth):
        last = li == depth - 1
        mod_x = mod_all[li, :nb]
        mod_c = mod_all[li, nb:nb + 1]
        win = win_all[li]
        w_zb = _col_tiles(win[:, off_rwkv:off_gmlp], _pick(rwkv_cols, 1152))
        w_uva = jnp.concatenate([win[:, off_gmlp:off_gate], win[:, :off_rwkv]], axis=1)
        w_uva = _col_tiles(w_uva, _pick(w_uva.shape[1], 768))
        w_zg = win[:, off_gate:].reshape(d, -1, d // tn_merge, tn_merge).transpose(2, 1, 0, 3)
        lp = {'mu': rwkv_mu[li], 'w0': rwkv_w0[li], 'w_up': rwkv_w_up[li], 'a0': rwkv_a0[li],
              'a_up': rwkv_a_up[li], 'g_up': rwkv_g_up[li], 'k_k': rwkv_k_k[li],
              'k_a': rwkv_k_a[li], 'r_k': rwkv_r_k[li]}

        def mixer(h, mod, nseq, seq, grid_mode, s0, need_out):
            flat = h.shape[:2]
            tm = _pick(flat[1], 1024)
            zb = _norm_mm(h, mod, norm_g[li, 1], w_zb, tm=tm)
            prep = _rwkv_prep(zb.reshape(nseq, seq, -1), lp, t_rows=_pick(seq, 256),
                              grid_mode=grid_mode)
            yf, yb, s_fin = _rwkv_scan(prep[:9], s0)
            if not need_out:
                return None, s_fin
            y_rwkv = _rwkv_post(yf, yb, prep[10], prep[9], rwkv_ln_g[li], rwkv_ln_b[li],
                                t_rows=_pick(seq, 256))
            z_uva = _norm_mm(h, mod, norm_g[li, 1], w_uva, tm=tm).reshape(nseq, seq, -1)
            y_pool = _pool(z_uva, pool_w[li], pool_scale[li], col_block=2 * gmlp_c // pool_c,
                           t_rows=_pick(seq, 512))
            y_gmlp = _gmlp(z_uva, gmlp_ln_g[li], gmlp_ln_b[li], gmlp_ws[li], gmlp_bs[li],
                           t_rows=_pick(seq, 512))
            ys = [y.reshape(flat + (-1,)) for y in (y_pool, y_rwkv, y_gmlp)]
            out = _merge(h, mod, norm_g[li, 1], *ys, w_zg, pp, pr, pg, wo,
                         li=li, tm=_pick(flat[1], 512))
            return out, s_fin

        def ffn(h, mod, k, final=False):
            return _ffn(h, mod, norm_g[li, 2 * k], wg, wu, wd, final_norm, sel=(li, k),
                        base=6 * k, final=final, tm=_pick(h.shape[1], 512))

        hc = ffn(hc, mod_c, 0)
        hc_new, s_ctx = mixer(hc, mod_c, nb, lc, False, zero_state, not last)
        if not last:
            hc = ffn(hc_new, mod_c, 1)

        x = ffn(x, mod_x, 0)
        x, _ = mixer(x, mod_x, nb, l, True, s_ctx, True)
        x = ffn(x, mod_x, 1, final=last)
    return x
```

```python
import functools
import math

import jax
import jax.numpy as jnp
from jax import lax
from jax.experimental import pallas as pl
from jax.experimental.pallas import tpu as pltpu

F32 = jnp.float32
BF16 = jnp.bfloat16
HIGHEST = lax.Precision.HIGHEST

NORM_EPS = 1e-6
LN_EPS = 1e-5
RWKV_GN_EPS = 64e-5
GRID_W = 64
N_MOD = 9
POOL_WINDOWS = (2, 4, 8, 16)
HEAD_DIM = 64
GMLP_CHUNK = 128
LANES = 128
SCAN_CHUNK = 64
VMEM_LIMIT = 56 * 1024 * 1024


def _params(*sem):
    return pltpu.CompilerParams(dimension_semantics=sem, vmem_limit_bytes=VMEM_LIMIT)


def _sigmoid(x):
    return 0.5 * jnp.tanh(0.5 * x) + 0.5


def _mm(a, b):
    return jnp.dot(a.astype(BF16), b.astype(BF16), preferred_element_type=F32)


def _mm_hi(a, b):
    return jnp.dot(a, b, precision=HIGHEST, preferred_element_type=F32)


def _mm_nt_hi(a, b):
    return lax.dot_general(a, b, (((1,), (1,)), ((), ())), precision=HIGHEST,
                           preferred_element_type=F32)


def _mm_tn_hi(a, b):
    return lax.dot_general(a, b, (((0,), (0,)), ((), ())), precision=HIGHEST,
                           preferred_element_type=F32)


def _rms(x, g):
    return x * lax.rsqrt(jnp.mean(x * x, axis=-1, keepdims=True) + NORM_EPS) * g


def _adaln_kernel(c_ref, w_ref, b_ref, o_ref):
    c = c_ref[...]
    o_ref[0] = _mm_hi(c * _sigmoid(c), w_ref[0]) + b_ref[0]


def _adaln(cond8, ada_w, ada_b):
    depth, d, n = ada_w.shape
    tn = 1024
    return pl.pallas_call(
        _adaln_kernel,
        grid=(depth, n // tn),
        in_specs=[pl.BlockSpec((8, d), lambda l, j: (0, 0)),
                  pl.BlockSpec((1, d, tn), lambda l, j: (l, 0, j)),
                  pl.BlockSpec((1, 1, tn), lambda l, j: (l, 0, j))],
        out_specs=pl.BlockSpec((1, 8, tn), lambda l, j: (l, 0, j)),
        out_shape=jax.ShapeDtypeStruct((depth, 8, n), F32),
        compiler_params=_params("arbitrary", "arbitrary"),
        name="adaln",
    )(cond8, ada_w, ada_b.reshape(depth, 1, n))


def _ffn_kernel(h_ref, mod_ref, g_ref, wg_ref, wu_ref, wd_ref, tail_ref, *rest, base, tail):
    if tail == "mixer_norm":
        o_ref, hn_o_ref, hn_ref = rest
    else:
        o_ref, hn_ref = rest
    j = pl.program_id(2)

    @pl.when(j == 0)
    def _():
        hn = _rms(h_ref[0], g_ref[...]) * (1.0 + mod_ref[0, base + 1:base + 2, :]) \
            + mod_ref[0, base:base + 1, :]
        hn_ref[...] = hn.astype(BF16)
        o_ref[0] = h_ref[0]

    hn = hn_ref[...]
    gate = jnp.dot(hn, wg_ref[...], preferred_element_type=F32)
    up = jnp.dot(hn, wu_ref[...], preferred_element_type=F32)
    act = (gate * _sigmoid(gate) * up).astype(BF16)
    o_ref[0] += (0.5 * mod_ref[0, base + 2:base + 3, :]) * jnp.dot(
        act, wd_ref[...], preferred_element_type=F32)

    if tail is not None:
        @pl.when(j == pl.num_programs(2) - 1)
        def _():
            normed = _rms(o_ref[0], tail_ref[...])
            if tail == "final":
                o_ref[0] = normed
            else:
                hn_o_ref[0] = (normed * (1.0 + mod_ref[0, 4:5, :])
                               + mod_ref[0, 3:4, :]).astype(BF16)


def _ffn(h, mod, g, wg, wu, wd, tail_g, *, sel, base, tail, tm, tf):
    nb, l, d = h.shape
    f = wg.shape[-1]
    li, k = sel
    per_batch = mod.shape[0] > 1
    mod_map = (lambda b, i, j: (b, 0, 0)) if per_batch else (lambda b, i, j: (0, 0, 0))
    row = pl.BlockSpec((1, tm, d), lambda b, i, j: (b, i, 0))
    out_specs, out_shape = row, jax.ShapeDtypeStruct((nb, l, d), F32)
    if tail == "mixer_norm":
        out_specs, out_shape = [row, row], [out_shape, jax.ShapeDtypeStruct((nb, l, d), BF16)]
    return pl.pallas_call(
        functools.partial(_ffn_kernel, base=base, tail=tail),
        grid=(nb, l // tm, f // tf),
        in_specs=[row,
                  pl.BlockSpec((1, N_MOD, d), mod_map),
                  pl.BlockSpec((1, d), lambda b, i, j: (0, 0)),
                  pl.BlockSpec((None, None, d, tf), lambda b, i, j: (li, k, 0, j)),
                  pl.BlockSpec((None, None, d, tf), lambda b, i, j: (li, k, 0, j)),
                  pl.BlockSpec((None, None, tf, d), lambda b, i, j: (li, k, j, 0)),
                  pl.BlockSpec((1, d), lambda b, i, j: (0, 0))],
        out_specs=out_specs,
        out_shape=out_shape,
        scratch_shapes=[pltpu.VMEM((tm, d), BF16)],
        compiler_params=_params("arbitrary", "arbitrary", "arbitrary"),
        name="ffn",
    )(h, mod, g.reshape(1, d), wg, wu, wd, tail_g.reshape(1, d))


def _proj_kernel(hn_ref, w_ref, o_ref):
    o_ref[0] = jnp.dot(hn_ref[0], w_ref[...], preferred_element_type=F32)


def _proj(hn, w, *, tm, tn):
    nb, l, d = hn.shape
    n = w.shape[1]
    return pl.pallas_call(
        _proj_kernel,
        grid=(nb, l // tm, n // tn),
        in_specs=[pl.BlockSpec((1, tm, d), lambda b, i, j: (b, i, 0)),
                  pl.BlockSpec((d, tn), lambda b, i, j: (0, j))],
        out_specs=pl.BlockSpec((1, tm, tn), lambda b, i, j: (b, i, j)),
        out_shape=jax.ShapeDtypeStruct((nb, l, n), F32),
        compiler_params=_params("arbitrary", "arbitrary", "arbitrary"),
        name="proj",
    )(hn, w)


POOL_HALO = 8


def _pool_kernel(prev_ref, x_ref, next_ref, pw_ref, ps_ref, o_ref, scr, *, t_rows, seq):
    i = pl.program_id(1)
    last = pl.num_programs(1) - 1
    hal = POOL_HALO
    scr[0:hal] = jnp.where(i > 0, prev_ref[0], 0.0)
    scr[hal:hal + t_rows] = x_ref[0]
    scr[hal + t_rows:2 * hal + t_rows] = jnp.where(i < last, next_ref[0], 0.0)
    t = i * t_rows + lax.broadcasted_iota(jnp.int32, (t_rows, LANES), 0)
    for gi, w in enumerate(POOL_WINDOWS):
        c0 = gi * LANES
        s = scr[hal - w // 2:hal - w // 2 + t_rows, c0:c0 + LANES]
        for dlt in range(-(w // 2) + 1, w // 2):
            s = s + scr[hal + dlt:hal + dlt + t_rows, c0:c0 + LANES]
        lo = jnp.clip(t - w // 2, 0, seq)
        hi = jnp.clip(t - w // 2 + w, 0, seq)
        cnt = (hi - lo).astype(F32)
        p = s / cnt - scr[hal:hal + t_rows, c0:c0 + LANES]
        y = _mm(p, pw_ref[gi]) * ps_ref[:, c0:c0 + LANES]
        o_ref[0, :, c0:c0 + LANES] = y.astype(o_ref.dtype)


def _pool(z, pool_w, pool_scale, *, col_block, t_rows):
    nb, l, _ = z.shape
    c = pool_scale.shape[-1]
    hal = POOL_HALO
    r = t_rows // hal
    nh = l // hal
    cb = col_block
    return pl.pallas_call(
        functools.partial(_pool_kernel, t_rows=t_rows, seq=l),
        grid=(nb, l // t_rows),
        in_specs=[pl.BlockSpec((1, hal, c), lambda b, i: (b, jnp.maximum(i * r - 1, 0), cb)),
                  pl.BlockSpec((1, t_rows, c), lambda b, i: (b, i, cb)),
                  pl.BlockSpec((1, hal, c), lambda b, i: (b, jnp.minimum((i + 1) * r, nh - 1), cb)),
                  pl.BlockSpec(pool_w.shape, lambda b, i: (0, 0, 0)),
                  pl.BlockSpec((1, c), lambda b, i: (0, 0))],
        out_specs=pl.BlockSpec((1, t_rows, c), lambda b, i: (b, i, 0)),
        out_shape=jax.ShapeDtypeStruct((nb, l, c), BF16),
        scratch_shapes=[pltpu.VMEM((t_rows + 2 * hal, c), F32)],
        compiler_params=_params("arbitrary", "arbitrary"),
        name="pool",
    )(z, z, z, pool_w.astype(BF16), pool_scale.reshape(1, c))


def _gelu(x):
    return x * (0.5 * (1.0 + jnp.tanh(math.sqrt(2.0 / math.pi) * (x + 0.044715 * (x * x * x)))))


def _gmlp_kernel(z_ref, lng_ref, lnb_ref, ws_ref, bsb_ref, o_ref, *, t_rows):
    width = o_ref.shape[2]
    groups = ws_ref.shape[0]
    gd = width // groups
    for ci in range(t_rows // GMLP_CHUNK):
        rows = slice(ci * GMLP_CHUNK, (ci + 1) * GMLP_CHUNK)
        u = _gelu(z_ref[0, rows, 0:width])
        v = _gelu(z_ref[0, rows, width:2 * width])
        mu = jnp.mean(v, axis=-1, keepdims=True)
        var = jnp.mean(jnp.square(v - mu), axis=-1, keepdims=True)
        vn = ((v - mu) * lax.rsqrt(var + LN_EPS)) * lng_ref[...] + lnb_ref[...]
        for g in range(groups):
            cols = slice(g * gd, (g + 1) * gd)
            s = _mm(ws_ref[g], vn[:, cols]) + bsb_ref[g]
            o_ref[0, rows, cols] = (u[:, cols] * s).astype(o_ref.dtype)


def _gmlp(zuv, ln_g, ln_b, ws, bs, *, t_rows):
    nb, l, _ = zuv.shape
    c = ln_g.shape[-1]
    c2 = 2 * c
    groups = ws.shape[0]
    bsb = jnp.broadcast_to(bs[:, :, None], (groups, GMLP_CHUNK, c // groups))
    return pl.pallas_call(
        functools.partial(_gmlp_kernel, t_rows=t_rows),
        grid=(nb, l // t_rows),
        in_specs=[pl.BlockSpec((1, t_rows, c2), lambda b, i: (b, i, 0)),
                  pl.BlockSpec((1, c), lambda b, i: (0, 0)),
                  pl.BlockSpec((1, c), lambda b, i: (0, 0)),
                  pl.BlockSpec(ws.shape, lambda b, i: (0, 0, 0)),
                  pl.BlockSpec(bsb.shape, lambda b, i: (0, 0, 0))],
        out_specs=pl.BlockSpec((1, t_rows, c), lambda b, i: (b, i, 0)),
        out_shape=jax.ShapeDtypeStruct((nb, l, c), BF16),
        compiler_params=_params("arbitrary", "arbitrary"),
        name="gmlp",
    )(zuv, ln_g.reshape(1, c), ln_b.reshape(1, c), ws.astype(BF16), bsb)


def _segsum(x, bd):
    hi = x.astype(BF16)
    lo = (x - hi.astype(F32)).astype(BF16)
    out = []
    for p in range(x.shape[-1] // LANES):
        cols = slice(p * LANES, (p + 1) * LANES)
        out.append(jnp.dot(hi[:, cols], bd, preferred_element_type=F32)
                   + jnp.dot(lo[:, cols], bd, preferred_element_type=F32))
    return jnp.concatenate(out, axis=-1)


def _rwkv_prep_kernel(prev_ref, x_ref, next_ref, mu_ref, w0_ref, wup_ref, a0_ref, aup_ref,
                      gup_ref, kk_ref, ka_ref, rk_ref, bd_ref,
                      lw0_o, lw1_o, kd0_o, kd1_o, b0_o, b1_o, kk_o, v_o, r_o, g_o, bv_o,
                      scr, *, t_rows, grid_mode, width):
    i = pl.program_id(1)
    last = pl.num_programs(1) - 1
    hal = GRID_W
    scr[0:hal] = jnp.where(i > 0, prev_ref[0], 0.0)
    scr[hal:hal + t_rows] = x_ref[0]
    scr[hal + t_rows:2 * hal + t_rows] = jnp.where(i < last, next_ref[0], 0.0)
    cols = scr.shape[1]
    if grid_mode:
        q = cols // 4
        bounds = (0, q, 2 * q, 3 * q, cols)
        offs = (-1, 1, -GRID_W, GRID_W)
    else:
        bounds = (0, cols // 2, cols)
        offs = (-1, 1)

    def zs_cols(c0, c1):
        n = c1 - c0
        x = scr[hal:hal + t_rows, c0:c1]
        ch = c0 + lax.broadcasted_iota(jnp.int32, (t_rows, n), 1)
        col = lax.broadcasted_iota(jnp.int32, (t_rows, n), 0) & (GRID_W - 1)
        shifted = None
        for qi, off in enumerate(offs):
            lo, hi = bounds[qi], bounds[qi + 1]
            if hi <= c0 or lo >= c1:
                continue
            src = scr[hal + off:hal + off + t_rows, c0:c1]
            if grid_mode and off == -1:
                src = jnp.where(col == 0, 0.0, src)
            if grid_mode and off == 1:
                src = jnp.where(col == GRID_W - 1, 0.0, src)
            shifted = src if shifted is None else jnp.where(ch >= lo, src, shifted)
        return x + (shifted - x) * mu_ref[:, c0:c1]

    w = width
    r = zs_cols(0, w)
    k = zs_cols(w, 2 * w)
    v = zs_cols(2 * w, 3 * w)
    rest = zs_cols(3 * w, cols)
    wd = jnp.tanh(rest[:, 0:LANES])
    ad = rest[:, LANES:2 * LANES]
    gd = _sigmoid(rest[:, 2 * LANES:3 * LANES])

    bd = bd_ref[...]
    kk0 = k * kk_ref[...]
    ss = _segsum(kk0 * kk0, bd)
    kk = kk0 / jnp.maximum(jnp.sqrt(ss), 1e-12)
    npair = w // LANES

    def put(o_ref, val):
        for p in range(npair):
            o_ref[0, p] = val[:, p * LANES:(p + 1) * LANES].astype(o_ref.dtype)

    put(kk_o, kk)
    put(v_o, v)
    put(r_o, r)
    kd_sum = None
    for d, (lw_o, kd_o, b_o) in enumerate(((lw0_o, kd0_o, b0_o), (lw1_o, kd1_o, b1_o))):
        w_pre = w0_ref[d] + _mm(wd, wup_ref[d])
        put(lw_o, -math.exp(-0.5) * _sigmoid(w_pre))
        a = _sigmoid(a0_ref[d] + _mm(ad, aup_ref[d]))
        kd = k * (1.0 + (a - 1.0) * ka_ref[...])
        put(kd_o, kd)
        put(b_o, kk * a)
        kd_sum = kd if kd_sum is None else kd_sum + kd
    g_o[0] = _mm(gd, gup_ref[...]).astype(g_o.dtype)
    bv_o[0] = (_segsum(r * kd_sum * rk_ref[...], bd) * v).astype(bv_o.dtype)


def _head_blockdiag(width):
    idx = jnp.arange(width) // HEAD_DIM
    return (idx[:, None] == idx[None, :]).astype(BF16)


def _rwkv_prep(zb, lp, *, t_rows, grid_mode):
    nb, l, cols = zb.shape
    w = lp['w0'].shape[-1]
    npair = w // LANES
    hal = GRID_W
    r = t_rows // hal
    nh = l // hal
    lora = lp['w_up'].shape[1]

    def pad_dir(up):
        z = jnp.zeros_like(up[0])
        return jnp.stack([jnp.concatenate([up[0], z], 0), jnp.concatenate([z, up[1]], 0)], 0)

    const2 = lambda b, i: (0, 0)
    const3 = lambda b, i: (0, 0, 0)
    pair_spec = pl.BlockSpec((1, npair, t_rows, LANES), lambda b, i: (b, 0, i, 0))
    wide_spec = pl.BlockSpec((1, t_rows, w), lambda b, i: (b, i, 0))
    pair_shape = lambda dt: jax.ShapeDtypeStruct((nb, npair, l, LANES), dt)
    wide_shape = jax.ShapeDtypeStruct((nb, l, w), BF16)
    return pl.pallas_call(
        functools.partial(_rwkv_prep_kernel, t_rows=t_rows, grid_mode=grid_mode, width=w),
        grid=(nb, l // t_rows),
        in_specs=[pl.BlockSpec((1, hal, cols), lambda b, i: (b, jnp.maximum(i * r - 1, 0), 0)),
                  pl.BlockSpec((1, t_rows, cols), lambda b, i: (b, i, 0)),
                  pl.BlockSpec((1, hal, cols), lambda b, i: (b, jnp.minimum((i + 1) * r, nh - 1), 0)),
                  pl.BlockSpec((1, cols), const2),
                  pl.BlockSpec((2, 1, w), const3),
                  pl.BlockSpec((2, 2 * lora, w), const3),
                  pl.BlockSpec((2, 1, w), const3),
                  pl.BlockSpec((2, 2 * lora, w), const3),
                  pl.BlockSpec(lp['g_up'].shape, const2),
                  pl.BlockSpec((1, w), const2),
                  pl.BlockSpec((1, w), const2),
                  pl.BlockSpec((1, w), const2),
                  pl.BlockSpec((LANES, LANES), const2)],
        out_specs=[pair_spec] * 9 + [wide_spec] * 2,
        out_shape=[pair_shape(F32)] * 2 + [pair_shape(BF16)] * 7 + [wide_shape] * 2,
        scratch_shapes=[pltpu.VMEM((t_rows + 2 * hal, cols), F32)],
        compiler_params=_params("arbitrary", "arbitrary"),
        name="rwkv_prep",
    )(zb, zb, zb, lp['mu'].reshape(1, cols), lp['w0'].reshape(2, 1, w),
      pad_dir(lp['w_up']).astype(BF16), lp['a0'].reshape(2, 1, w),
      pad_dir(lp['a_up']).astype(BF16), lp['g_up'].astype(BF16), lp['k_k'].reshape(1, w),
      lp['k_a'].reshape(1, w), lp['r_k'].reshape(1, w), _head_blockdiag(LANES))


def _stack2(x, m0):
    return jnp.concatenate([jnp.where(m0, x, 0.0), jnp.where(m0, 0.0, x)], axis=0)


_NN = (((1,), (0,)), ((), ()))
_NT = (((1,), (1,)), ((), ()))
_TN = (((0,), (0,)), ((), ()))

SCAN_PREC = {"cum": "rhs2", "gram": "bf16", "init": "bf16", "apply": "bf16", "out": "bf16",
             "state": "bf16"}
SCAN_LOCKSTEP_PAIRS = 8

def _split(a):
    hi = a.astype(BF16)
    return hi, (a - hi.astype(F32)).astype(BF16)


def _dg(a, b, dims, site):
    mode = SCAN_PREC[site]
    if mode == "f32":
        return lax.dot_general(a, b, dims, precision=HIGHEST, preferred_element_type=F32)
    if mode == "bf16":
        return lax.dot_general(a.astype(BF16), b.astype(BF16), dims, preferred_element_type=F32)
    b_hi, b_lo = _split(b)
    if mode == "rhs2":
        a16 = a.astype(BF16)
        return (lax.dot_general(a16, b_hi, dims, preferred_element_type=F32)
                + lax.dot_general(a16, b_lo, dims, preferred_element_type=F32))
    a_hi, a_lo = _split(a)
    return (lax.dot_general(a_hi, b_hi, dims, preferred_element_type=F32)
            + lax.dot_general(a_hi, b_lo, dims, preferred_element_type=F32)
            + lax.dot_general(a_lo, b_hi, dims, preferred_element_type=F32))


def _chunk_step(refs, s_ref, *, rev, tri, strict, incl, diag, m0):
    lw, kd, b, kk, v, r = (x[...].astype(F32) for x in refs)
    s = s_ref[...]
    c = lw.shape[0]
    cum = _dg(tri, lw, _NN, "cum")
    yield
    cum_prev = cum - lw
    end = 0 if rev else c - 1
    tot = cum[end:end + 1, :]
    mid = cum[c // 2:c // 2 + 1, :]
    e_inv = jnp.exp(mid - cum)
    nkk = -kk
    left = jnp.concatenate([_stack2(nkk * jnp.exp(cum_prev - mid), m0),
                            _stack2(r * jnp.exp(cum - mid), m0)], axis=0)
    right = jnp.concatenate([_stack2(b * e_inv, m0), _stack2(kd * e_inv, m0)], axis=0)
    gram = _dg(left, right, _NT, "gram")
    yield
    c2 = 2 * c
    a_ab = jnp.where(strict, gram[0:c2, 0:c2], 0.0)
    a_ak = jnp.where(strict, gram[0:c2, c2:2 * c2], 0.0)
    a_rb = jnp.where(incl, gram[c2:2 * c2, 0:c2], 0.0)
    a_rk = jnp.where(incl, gram[c2:2 * c2, c2:2 * c2], 0.0)
    vsw = pltpu.roll(v, HEAD_DIM, 1)
    vbd = jnp.concatenate([jnp.where(m0, 0.0, vsw), jnp.where(m0, vsw, 0.0)], axis=0)
    e_end = jnp.exp(tot - cum)
    x = (_dg(_stack2(nkk * jnp.exp(cum_prev), m0), s, _NT, "init")
         + _dg(a_ak, vbd, _NN, "init"))
    ybd = _dg(_stack2(r * jnp.exp(cum), m0), s, _NT, "out") + _dg(a_rk, vbd, _NN, "out")
    s_new = s * jnp.exp(tot) + _dg(vbd, _stack2(kd * e_end, m0), _TN, "state")
    steps = max(1, (c - 1).bit_length())
    pw = a_ab
    for it in range(steps):
        yield
        if it < steps - 1:
            both = _dg(pw, pw + x, _NN, "apply")
            x = x + jnp.where(diag, 0.0, both)
            pw = jnp.where(diag, both, 0.0)
        else:
            x = x + _dg(pw, x, _NN, "apply")
    yield
    u = x
    ybd = ybd + _dg(a_rb, u, _NN, "out")
    y = pltpu.roll(ybd[0:c] + ybd[c:c2], HEAD_DIM, 1)
    s_new = s_new + _dg(u, _stack2(b * e_end, m0), _TN, "state")
    return y, s_new


def _lockstep(gens):
    results = [None] * len(gens)
    live = list(range(len(gens)))
    while live:
        still = []
        for i in live:
            try:
                next(gens[i])
                still.append(i)
            except StopIteration as stop:
                results[i] = stop.value
        live = still
    return results


def _rwkv_scan_kernel(lw0, kd0, b0, kkf, vf, rf, lw1, kd1, b1, kkb, vb, rb, s0_ref,
                      yf_o, yb_o, sfin_o, st):
    ci = pl.program_id(1)
    c = lw0.shape[2]
    npair = lw0.shape[1]

    @pl.when(ci == 0)
    def _():
        st[...] = s0_ref[:, 0]

    row = lax.broadcasted_iota(jnp.int32, (c, c), 0)
    colm = lax.broadcasted_iota(jnp.int32, (c, c), 1)
    tri_f = (colm <= row).astype(F32)
    tri_b = (colm >= row).astype(F32)
    rows2 = lax.broadcasted_iota(jnp.int32, (2 * c, 2 * c), 0)
    cols2 = lax.broadcasted_iota(jnp.int32, (2 * c, 2 * c), 1)
    r2 = rows2 & (c - 1)
    c2 = cols2 & (c - 1)
    diag = (rows2 < c) == (cols2 < c)
    m0 = lax.broadcasted_iota(jnp.int32, (1, LANES), 1) < HEAD_DIM

    fwd_refs = (lw0, kd0, b0, kkf, vf, rf)
    bwd_refs = (lw1, kd1, b1, kkb, vb, rb)
    for p0 in range(0, npair, SCAN_LOCKSTEP_PAIRS):
        gens, outs = [], []
        for p in range(p0, min(p0 + SCAN_LOCKSTEP_PAIRS, npair)):
            gens.append(_chunk_step([x.at[0, p] for x in fwd_refs], st.at[0, p], rev=False,
                                    tri=tri_f, strict=c2 < r2, incl=c2 <= r2, diag=diag, m0=m0))
            outs.append((yf_o, 0, p))
            gens.append(_chunk_step([x.at[0, p] for x in bwd_refs], st.at[1, p], rev=True,
                                    tri=tri_b, strict=c2 > r2, incl=c2 >= r2, diag=diag, m0=m0))
            outs.append((yb_o, 1, p))
        for (y_o, d, p), (y, s_new) in zip(outs, _lockstep(gens)):
            y_o[0, p] = y.astype(y_o.dtype)
            st[d, p] = s_new

    @pl.when(ci == pl.num_programs(1) - 1)
    def _():
        sfin_o[:, 0] = st[...]


def _rwkv_scan(prep, s0):
    lw0, lw1, kd0, kd1, b0, b1, kk, v, r = prep
    nb, npair, l, _ = lw0.shape
    c = SCAN_CHUNK
    assert c == HEAD_DIM and 2 * HEAD_DIM == LANES and l % c == 0
    n = l // c
    fwd = pl.BlockSpec((1, npair, c, LANES), lambda b, i: (b, 0, i, 0))
    bwd = pl.BlockSpec((1, npair, c, LANES), lambda b, i: (b, 0, n - 1 - i, 0))
    st_spec = pl.BlockSpec((2, 1, npair, LANES, LANES), lambda b, i: (0, b, 0, 0, 0))
    y_shape = jax.ShapeDtypeStruct((nb, npair, l, LANES), BF16)
    return pl.pallas_call(
        _rwkv_scan_kernel,
        grid=(nb, n),
        in_specs=[fwd] * 6 + [bwd] * 6 + [st_spec],
        out_specs=[fwd, bwd, st_spec],
        out_shape=[y_shape, y_shape, jax.ShapeDtypeStruct(s0.shape, F32)],
        scratch_shapes=[pltpu.VMEM((2, npair, LANES, LANES), F32)],
        compiler_params=_params("arbitrary", "arbitrary"),
        name="rwkv_scan",
    )(lw0, kd0, b0, kk, v, r, lw1, kd1, b1, kk, v, r, s0)


def _rwkv_post_kernel(yf_ref, yb_ref, bv_ref, g_ref, lng_ref, lnb_ref, bd_ref, o_ref):
    npair = yf_ref.shape[1]
    y = jnp.concatenate([yf_ref[0, p].astype(F32) + yb_ref[0, p].astype(F32)
                         for p in range(npair)], axis=-1)
    bd = bd_ref[...]
    inv = 1.0 / HEAD_DIM
    m = _segsum(y, bd) * inv
    dlt = y - m
    var = _segsum(dlt * dlt, bd) * inv
    yn = dlt * lax.rsqrt(var + RWKV_GN_EPS) * lng_ref[...] + lnb_ref[...]
    o_ref[0] = ((yn + bv_ref[0].astype(F32)) * g_ref[0].astype(F32)).astype(o_ref.dtype)


def _rwkv_post(yf, yb, bv, g, ln_g, ln_b, *, t_rows):
    nb, npair, l, _ = yf.shape
    w = npair * LANES
    pair_spec = pl.BlockSpec((1, npair, t_rows, LANES), lambda b, i: (b, 0, i, 0))
    wide_spec = pl.BlockSpec((1, t_rows, w), lambda b, i: (b, i, 0))
    const2 = lambda b, i: (0, 0)
    return pl.pallas_call(
        _rwkv_post_kernel,
        grid=(nb, l // t_rows),
        in_specs=[pair_spec, pair_spec, wide_spec, wide_spec,
                  pl.BlockSpec((1, w), const2), pl.BlockSpec((1, w), const2),
                  pl.BlockSpec((LANES, LANES), const2)],
        out_specs=wide_spec,
        out_shape=jax.ShapeDtypeStruct((nb, l, w), BF16),
        compiler_params=_params("arbitrary", "arbitrary"),
        name="rwkv_post",
    )(yf, yb, bv, g, ln_g.reshape(1, w), ln_b.reshape(1, w), _head_blockdiag(LANES))


def _merge_kernel(h_ref, hn_ref, mod_ref, yp_ref, yr_ref, yg_ref, wzg_ref, pp_ref, pr_ref, pg_ref,
                  wo_ref, o_ref):
    @pl.when(pl.program_id(2) == 0)
    def _():
        o_ref[0] = h_ref[0]

    hn = hn_ref[0]
    merged = None
    for br, (y_ref, p_ref) in enumerate(((yp_ref, pp_ref), (yr_ref, pr_ref), (yg_ref, pg_ref))):
        gate = _sigmoid(jnp.dot(hn, wzg_ref[br], preferred_element_type=F32))
        term = gate * jnp.dot(y_ref[0], p_ref[...], preferred_element_type=F32)
        merged = term if merged is None else merged + term
    o_ref[0] += mod_ref[0, 5:6, :] * jnp.dot(merged.astype(BF16), wo_ref[...],
                                             preferred_element_type=F32)


def _merge(h, hn, mod, yp, yr, yg, wzg, pp, pr, pg, wo, *, li, tm, tn):
    nb, l, d = h.shape
    per_batch = mod.shape[0] > 1
    mod_map = (lambda b, i, n: (b, 0, 0)) if per_batch else (lambda b, i, n: (0, 0, 0))
    row = lambda width: pl.BlockSpec((1, tm, width), lambda b, i, n: (b, i, 0))
    col = lambda a: pl.BlockSpec((None, a.shape[1], tn), lambda b, i, n: (li, 0, n))
    return pl.pallas_call(
        _merge_kernel,
        grid=(nb, l // tm, d // tn),
        in_specs=[row(d), row(d), pl.BlockSpec((1, N_MOD, d), mod_map),
                  row(yp.shape[2]), row(yr.shape[2]), row(yg.shape[2]),
                  pl.BlockSpec((wzg.shape[0], d, tn), lambda b, i, n: (0, 0, n)),
                  col(pp), col(pr), col(pg),
                  pl.BlockSpec((None, tn, d), lambda b, i, n: (li, n, 0))],
        out_specs=row(d),
        out_shape=jax.ShapeDtypeStruct((nb, l, d), F32),
        compiler_params=_params("arbitrary", "arbitrary", "arbitrary"),
        name="merge",
    )(h, hn, mod, yp, yr, yg, wzg, pp, pr, pg, wo)


def _pick(n, pref):
    t = min(pref, n)
    while t > LANES and (n % t or t % LANES):
        t -= LANES
    return t if n % t == 0 else n


def kernel(x, c, ctx, c_ctx, ada_w, ada_b, norm_g, ffn_w_gate, ffn_w_up, ffn_w_down, w_in,
           pool_w, pool_scale, rwkv_mu, rwkv_w0, rwkv_w_up, rwkv_a0, rwkv_a_up, rwkv_g_up,
           rwkv_k_k, rwkv_k_a, rwkv_r_k, rwkv_ln_g, rwkv_ln_b, gmlp_ln_g, gmlp_ln_b, gmlp_ws,
           gmlp_bs, proj_pool, proj_rwkv, proj_gmlp, w_out, final_norm):
    depth = ada_w.shape[0]
    nb, l, d = x.shape
    lc = ctx.shape[1]
    width = rwkv_w0.shape[-1]
    pool_c = pool_scale.shape[-1]
    gmlp_c = gmlp_ln_g.shape[-1]
    rwkv_cols = rwkv_mu.shape[-1]
    off_rwkv = pool_c
    off_gmlp = off_rwkv + rwkv_cols
    off_gate = off_gmlp + 2 * gmlp_c
    npair = width // LANES

    cond8 = jnp.zeros((8, d), F32).at[:nb].set(c).at[nb].set(c_ctx)
    mod_all = _adaln(cond8, ada_w, ada_b).reshape(depth, 8, N_MOD, d)

    zero_state = jnp.zeros((2, nb, npair, LANES, LANES), F32)
    wg, wu, wd = (a.astype(BF16) for a in (ffn_w_gate, ffn_w_up, ffn_w_down))
    pp, pr, pg, wo = (a.astype(BF16) for a in (proj_pool, proj_rwkv, proj_gmlp, w_out))
    win_all = w_in.astype(BF16)
    tf = _pick(wg.shape[-1], 512)
    assert (2 * gmlp_c) % pool_c == 0
    hc = ctx.reshape(1, nb * lc, d)
    for li in range(depth):
        last = li == depth - 1
        mod_x = mod_all[li, :nb]
        mod_c = mod_all[li, nb:nb + 1]
        win = win_all[li]
        w_zb = win[:, off_rwkv:off_gmlp]
        w_uva = jnp.concatenate([win[:, off_gmlp:off_gate], win[:, :off_rwkv]], axis=1)
        w_zg = win[:, off_gate:].reshape(d, -1, d).transpose(1, 0, 2)
        lp = {'mu': rwkv_mu[li], 'w0': rwkv_w0[li], 'w_up': rwkv_w_up[li], 'a0': rwkv_a0[li],
              'a_up': rwkv_a_up[li], 'g_up': rwkv_g_up[li], 'k_k': rwkv_k_k[li],
              'k_a': rwkv_k_a[li], 'r_k': rwkv_r_k[li]}

        def mixer(h, hn, mod, nseq, seq, grid_mode, s0, need_out):
            flat = h.shape[:2]
            tm = _pick(flat[1], 1024)
            zb = _proj(hn, w_zb, tm=tm, tn=_pick(rwkv_cols, 1152))
            prep = _rwkv_prep(zb.reshape(nseq, seq, -1), lp, t_rows=_pick(seq, 256),
                              grid_mode=grid_mode)
            yf, yb, s_fin = _rwkv_scan(prep[:9], s0)
            if not need_out:
                return None, s_fin
            y_rwkv = _rwkv_post(yf, yb, prep[10], prep[9], rwkv_ln_g[li], rwkv_ln_b[li],
                                t_rows=_pick(seq, 256))
            z_uva = _proj(hn, w_uva, tm=tm,
                          tn=_pick(w_uva.shape[1], 768)).reshape(nseq, seq, -1)
            y_pool = _pool(z_uva, pool_w[li], pool_scale[li], col_block=2 * gmlp_c // pool_c,
                           t_rows=_pick(seq, 512))
            y_gmlp = _gmlp(z_uva, gmlp_ln_g[li], gmlp_ln_b[li], gmlp_ws[li], gmlp_bs[li],
                           t_rows=_pick(seq, 512))
            ys = [y.reshape(flat + (-1,)) for y in (y_pool, y_rwkv, y_gmlp)]
            out = _merge(h, hn, mod, *ys, w_zg, pp, pr, pg, wo,
                         li=li, tm=_pick(flat[1], 512), tn=_pick(d, 512))
            return out, s_fin

        def ffn(h, mod, k, tail, tail_g):
            return _ffn(h, mod, norm_g[li, 2 * k], wg, wu, wd, tail_g, sel=(li, k),
                        base=6 * k, tail=tail, tm=_pick(h.shape[1], 512), tf=tf)

        hc, hnc = ffn(hc, mod_c, 0, "mixer_norm", norm_g[li, 1])
        hc_new, s_ctx = mixer(hc, hnc, mod_c, nb, lc, False, zero_state, not last)
        if not last:
            hc = ffn(hc_new, mod_c, 1, None, final_norm)

        x, hn = ffn(x, mod_x, 0, "mixer_norm", norm_g[li, 1])
        x, _ = mixer(x, hn, mod_x, nb, l, True, s_ctx, True)
        x = ffn(x, mod_x, 1, "final" if last else None, final_norm)
    return x
```

```python
import functools
import math

import jax
import jax.numpy as jnp
from jax import lax
from jax.experimental import pallas as pl
from jax.experimental.pallas import tpu as pltpu

F32 = jnp.float32
BF16 = jnp.bfloat16
HIGHEST = lax.Precision.HIGHEST

NORM_EPS = 1e-6
LN_EPS = 1e-5
RWKV_GN_EPS = 64e-5
GRID_W = 64
N_MOD = 9
POOL_WINDOWS = (2, 4, 8, 16)
HEAD_DIM = 64
GMLP_CHUNK = 128
LANES = 128
SCAN_CHUNK = 64
VMEM_LIMIT = 56 * 1024 * 1024


def _params(*sem):
    return pltpu.CompilerParams(dimension_semantics=sem, vmem_limit_bytes=VMEM_LIMIT)


def _sigmoid(x):
    return 0.5 * jnp.tanh(0.5 * x) + 0.5


def _mm(a, b):
    return jnp.dot(a.astype(BF16), b.astype(BF16), preferred_element_type=F32)


def _mm_hi(a, b):
    return jnp.dot(a, b, precision=HIGHEST, preferred_element_type=F32)


def _mm_nt_hi(a, b):
    return lax.dot_general(a, b, (((1,), (1,)), ((), ())), precision=HIGHEST,
                           preferred_element_type=F32)


def _mm_tn_hi(a, b):
    return lax.dot_general(a, b, (((0,), (0,)), ((), ())), precision=HIGHEST,
                           preferred_element_type=F32)


def _rms(x, g):
    return x * lax.rsqrt(jnp.mean(x * x, axis=-1, keepdims=True) + NORM_EPS) * g


def _adaln_kernel(c_ref, w_ref, b_ref, o_ref):
    c = c_ref[...]
    o_ref[0] = _mm_hi(c * _sigmoid(c), w_ref[0]) + b_ref[0]


def _adaln(cond8, ada_w, ada_b):
    depth, d, n = ada_w.shape
    tn = 1024
    return pl.pallas_call(
        _adaln_kernel,
        grid=(depth, n // tn),
        in_specs=[pl.BlockSpec((8, d), lambda l, j: (0, 0)),
                  pl.BlockSpec((1, d, tn), lambda l, j: (l, 0, j)),
                  pl.BlockSpec((1, 1, tn), lambda l, j: (l, 0, j))],
        out_specs=pl.BlockSpec((1, 8, tn), lambda l, j: (l, 0, j)),
        out_shape=jax.ShapeDtypeStruct((depth, 8, n), F32),
        compiler_params=_params("arbitrary", "arbitrary"),
        name="adaln",
    )(cond8, ada_w, ada_b.reshape(depth, 1, n))


FFN_WEIGHT_SLOTS = 2


def _ffn_kernel(h_ref, mod_ref, g_ref, wg_hbm, wu_hbm, wd_hbm, tail_ref, *rest,
                base, tail, sel, tf):
    if tail == "mixer_norm":
        o_ref, hn_o_ref, hn_ref, wg_buf, wu_buf, wd_buf, sem = rest
    else:
        o_ref, hn_ref, wg_buf, wu_buf, wd_buf, sem = rest
    li, k = sel
    nf = wd_hbm.shape[2] // tf
    step = pl.program_id(0) * pl.num_programs(1) + pl.program_id(1)
    nsteps = pl.num_programs(0) * pl.num_programs(1)
    first_tile = step * nf

    def copies(j, slot):
        cols = pl.ds(j * tf, tf)
        return (pltpu.make_async_copy(wg_hbm.at[li, k, :, cols], wg_buf.at[slot], sem.at[slot, 0]),
                pltpu.make_async_copy(wu_hbm.at[li, k, :, cols], wu_buf.at[slot], sem.at[slot, 1]),
                pltpu.make_async_copy(wd_hbm.at[li, k, cols, :], wd_buf.at[slot], sem.at[slot, 2]))

    @pl.when(step == 0)
    def _():
        for cp in copies(0, 0):
            cp.start()

    hn = _rms(h_ref[0], g_ref[...]) * (1.0 + mod_ref[0, base + 1:base + 2, :]) \
        + mod_ref[0, base:base + 1, :]
    hn_ref[...] = hn.astype(BF16)
    o_ref[0] = h_ref[0]
    half_gate = 0.5 * mod_ref[0, base + 2:base + 3, :]

    for j in range(nf):
        slot = (first_tile + j) % FFN_WEIGHT_SLOTS
        nxt = (first_tile + j + 1) % FFN_WEIGHT_SLOTS
        if j + 1 < nf:
            for cp in copies(j + 1, nxt):
                cp.start()
        else:
            @pl.when(step + 1 < nsteps)
            def _():
                for cp in copies(0, nxt):
                    cp.start()
        for cp in copies(j, slot):
            cp.wait()
        hn = hn_ref[...]
        gate = jnp.dot(hn, wg_buf[slot], preferred_element_type=F32)
        up = jnp.dot(hn, wu_buf[slot], preferred_element_type=F32)
        act = (gate * _sigmoid(gate) * up).astype(BF16)
        o_ref[0] += half_gate * jnp.dot(act, wd_buf[slot], preferred_element_type=F32)

    if tail is not None:
        normed = _rms(o_ref[0], tail_ref[...])
        if tail == "final":
            o_ref[0] = normed
        else:
            hn_o_ref[0] = (normed * (1.0 + mod_ref[0, 4:5, :])
                           + mod_ref[0, 3:4, :]).astype(BF16)


def _ffn(h, mod, g, wg, wu, wd, tail_g, *, sel, base, tail, tm, tf):
    nb, l, d = h.shape
    assert wg.shape[-1] % tf == 0 and l % tm == 0
    per_batch = mod.shape[0] > 1
    mod_map = (lambda b, i: (b, 0, 0)) if per_batch else (lambda b, i: (0, 0, 0))
    row = pl.BlockSpec((1, tm, d), lambda b, i: (b, i, 0))
    hbm = pl.BlockSpec(memory_space=pl.ANY)
    out_specs, out_shape = row, jax.ShapeDtypeStruct((nb, l, d), F32)
    if tail == "mixer_norm":
        out_specs, out_shape = [row, row], [out_shape, jax.ShapeDtypeStruct((nb, l, d), BF16)]
    slots = FFN_WEIGHT_SLOTS
    return pl.pallas_call(
        functools.partial(_ffn_kernel, base=base, tail=tail, sel=sel, tf=tf),
        grid=(nb, l // tm),
        in_specs=[row,
                  pl.BlockSpec((1, N_MOD, d), mod_map),
                  pl.BlockSpec((1, d), lambda b, i: (0, 0)),
                  hbm, hbm, hbm,
                  pl.BlockSpec((1, d), lambda b, i: (0, 0))],
        out_specs=out_specs,
        out_shape=out_shape,
        scratch_shapes=[pltpu.VMEM((tm, d), BF16),
                        pltpu.VMEM((slots, d, tf), BF16),
                        pltpu.VMEM((slots, d, tf), BF16),
                        pltpu.VMEM((slots, tf, d), BF16),
                        pltpu.SemaphoreType.DMA((slots, 3))],
        compiler_params=_params("arbitrary", "arbitrary"),
        name="ffn",
    )(h, mod, g.reshape(1, d), wg, wu, wd, tail_g.reshape(1, d))


def _proj_kernel(hn_ref, w_ref, o_ref):
    o_ref[0] = jnp.dot(hn_ref[0], w_ref[...], preferred_element_type=F32)


def _proj(hn, w, *, tm, tn):
    nb, l, d = hn.shape
    n = w.shape[1]
    return pl.pallas_call(
        _proj_kernel,
        grid=(nb, l // tm, n // tn),
        in_specs=[pl.BlockSpec((1, tm, d), lambda b, i, j: (b, i, 0)),
                  pl.BlockSpec((d, tn), lambda b, i, j: (0, j))],
        out_specs=pl.BlockSpec((1, tm, tn), lambda b, i, j: (b, i, j)),
        out_shape=jax.ShapeDtypeStruct((nb, l, n), F32),
        compiler_params=_params("arbitrary", "arbitrary", "arbitrary"),
        name="proj",
    )(hn, w)


POOL_HALO = 8


def _pool_kernel(prev_ref, x_ref, next_ref, pw_ref, ps_ref, o_ref, scr, *, t_rows, seq):
    i = pl.program_id(1)
    last = pl.num_programs(1) - 1
    hal = POOL_HALO
    scr[0:hal] = jnp.where(i > 0, prev_ref[0], 0.0)
    scr[hal:hal + t_rows] = x_ref[0]
    scr[hal + t_rows:2 * hal + t_rows] = jnp.where(i < last, next_ref[0], 0.0)
    t = i * t_rows + lax.broadcasted_iota(jnp.int32, (t_rows, LANES), 0)
    for gi, w in enumerate(POOL_WINDOWS):
        c0 = gi * LANES
        s = scr[hal - w // 2:hal - w // 2 + t_rows, c0:c0 + LANES]
        for dlt in range(-(w // 2) + 1, w // 2):
            s = s + scr[hal + dlt:hal + dlt + t_rows, c0:c0 + LANES]
        lo = jnp.clip(t - w // 2, 0, seq)
        hi = jnp.clip(t - w // 2 + w, 0, seq)
        cnt = (hi - lo).astype(F32)
        p = s / cnt - scr[hal:hal + t_rows, c0:c0 + LANES]
        y = _mm(p, pw_ref[gi]) * ps_ref[:, c0:c0 + LANES]
        o_ref[0, :, c0:c0 + LANES] = y.astype(o_ref.dtype)


def _pool(z, pool_w, pool_scale, *, col_block, t_rows):
    nb, l, _ = z.shape
    c = pool_scale.shape[-1]
    hal = POOL_HALO
    r = t_rows // hal
    nh = l // hal
    cb = col_block
    return pl.pallas_call(
        functools.partial(_pool_kernel, t_rows=t_rows, seq=l),
        grid=(nb, l // t_rows),
        in_specs=[pl.BlockSpec((1, hal, c), lambda b, i: (b, jnp.maximum(i * r - 1, 0), cb)),
                  pl.BlockSpec((1, t_rows, c), lambda b, i: (b, i, cb)),
                  pl.BlockSpec((1, hal, c), lambda b, i: (b, jnp.minimum((i + 1) * r, nh - 1), cb)),
                  pl.BlockSpec(pool_w.shape, lambda b, i: (0, 0, 0)),
                  pl.BlockSpec((1, c), lambda b, i: (0, 0))],
        out_specs=pl.BlockSpec((1, t_rows, c), lambda b, i: (b, i, 0)),
        out_shape=jax.ShapeDtypeStruct((nb, l, c), BF16),
        scratch_shapes=[pltpu.VMEM((t_rows + 2 * hal, c), F32)],
        compiler_params=_params("arbitrary", "arbitrary"),
        name="pool",
    )(z, z, z, pool_w.astype(BF16), pool_scale.reshape(1, c))


def _gelu(x):
    return x * (0.5 * (1.0 + jnp.tanh(math.sqrt(2.0 / math.pi) * (x + 0.044715 * (x * x * x)))))


def _gmlp_kernel(z_ref, lng_ref, lnb_ref, ws_ref, bsb_ref, o_ref, *, t_rows):
    width = o_ref.shape[2]
    groups = ws_ref.shape[0]
    gd = width // groups
    for ci in range(t_rows // GMLP_CHUNK):
        rows = slice(ci * GMLP_CHUNK, (ci + 1) * GMLP_CHUNK)
        u = _gelu(z_ref[0, rows, 0:width])
        v = _gelu(z_ref[0, rows, width:2 * width])
        mu = jnp.mean(v, axis=-1, keepdims=True)
        var = jnp.mean(jnp.square(v - mu), axis=-1, keepdims=True)
        vn = ((v - mu) * lax.rsqrt(var + LN_EPS)) * lng_ref[...] + lnb_ref[...]
        for g in range(groups):
            cols = slice(g * gd, (g + 1) * gd)
            s = _mm(ws_ref[g], vn[:, cols]) + bsb_ref[g]
            o_ref[0, rows, cols] = (u[:, cols] * s).astype(o_ref.dtype)


def _gmlp(zuv, ln_g, ln_b, ws, bs, *, t_rows):
    nb, l, _ = zuv.shape
    c = ln_g.shape[-1]
    c2 = 2 * c
    groups = ws.shape[0]
    bsb = jnp.broadcast_to(bs[:, :, None], (groups, GMLP_CHUNK, c // groups))
    return pl.pallas_call(
        functools.partial(_gmlp_kernel, t_rows=t_rows),
        grid=(nb, l // t_rows),
        in_specs=[pl.BlockSpec((1, t_rows, c2), lambda b, i: (b, i, 0)),
                  pl.BlockSpec((1, c), lambda b, i: (0, 0)),
                  pl.BlockSpec((1, c), lambda b, i: (0, 0)),
                  pl.BlockSpec(ws.shape, lambda b, i: (0, 0, 0)),
                  pl.BlockSpec(bsb.shape, lambda b, i: (0, 0, 0))],
        out_specs=pl.BlockSpec((1, t_rows, c), lambda b, i: (b, i, 0)),
        out_shape=jax.ShapeDtypeStruct((nb, l, c), BF16),
        compiler_params=_params("arbitrary", "arbitrary"),
        name="gmlp",
    )(zuv, ln_g.reshape(1, c), ln_b.reshape(1, c), ws.astype(BF16), bsb)


def _segsum(x, bd):
    hi = x.astype(BF16)
    lo = (x - hi.astype(F32)).astype(BF16)
    out = []
    for p in range(x.shape[-1] // LANES):
        cols = slice(p * LANES, (p + 1) * LANES)
        out.append(jnp.dot(hi[:, cols], bd, preferred_element_type=F32)
                   + jnp.dot(lo[:, cols], bd, preferred_element_type=F32))
    return jnp.concatenate(out, axis=-1)


def _rwkv_prep_kernel(prev_ref, x_ref, next_ref, mu_ref, w0_ref, wup_ref, a0_ref, aup_ref,
                      gup_ref, kk_ref, ka_ref, rk_ref, bd_ref,
                      lw0_o, lw1_o, kd0_o, kd1_o, b0_o, b1_o, kk_o, v_o, r_o, g_o, bv_o,
                      scr, *, t_rows, grid_mode, width):
    i = pl.program_id(1)
    last = pl.num_programs(1) - 1
    hal = GRID_W
    scr[0:hal] = jnp.where(i > 0, prev_ref[0], 0.0)
    scr[hal:hal + t_rows] = x_ref[0]
    scr[hal + t_rows:2 * hal + t_rows] = jnp.where(i < last, next_ref[0], 0.0)
    cols = scr.shape[1]
    if grid_mode:
        q = cols // 4
        bounds = (0, q, 2 * q, 3 * q, cols)
        offs = (-1, 1, -GRID_W, GRID_W)
    else:
        bounds = (0, cols // 2, cols)
        offs = (-1, 1)

    def zs_cols(c0, c1):
        n = c1 - c0
        x = scr[hal:hal + t_rows, c0:c1]
        ch = c0 + lax.broadcasted_iota(jnp.int32, (t_rows, n), 1)
        col = lax.broadcasted_iota(jnp.int32, (t_rows, n), 0) & (GRID_W - 1)
        shifted = None
        for qi, off in enumerate(offs):
            lo, hi = bounds[qi], bounds[qi + 1]
            if hi <= c0 or lo >= c1:
                continue
            src = scr[hal + off:hal + off + t_rows, c0:c1]
            if grid_mode and off == -1:
                src = jnp.where(col == 0, 0.0, src)
            if grid_mode and off == 1:
                src = jnp.where(col == GRID_W - 1, 0.0, src)
            shifted = src if shifted is None else jnp.where(ch >= lo, src, shifted)
        return x + (shifted - x) * mu_ref[:, c0:c1]

    w = width
    r = zs_cols(0, w)
    k = zs_cols(w, 2 * w)
    v = zs_cols(2 * w, 3 * w)
    rest = zs_cols(3 * w, cols)
    wd = jnp.tanh(rest[:, 0:LANES])
    ad = rest[:, LANES:2 * LANES]
    gd = _sigmoid(rest[:, 2 * LANES:3 * LANES])

    bd = bd_ref[...]
    kk0 = k * kk_ref[...]
    ss = _segsum(kk0 * kk0, bd)
    kk = kk0 / jnp.maximum(jnp.sqrt(ss), 1e-12)
    npair = w // LANES

    def put(o_ref, val):
        for p in range(npair):
            o_ref[0, p] = val[:, p * LANES:(p + 1) * LANES].astype(o_ref.dtype)

    put(kk_o, kk)
    put(v_o, v)
    put(r_o, r)
    kd_sum = None
    for d, (lw_o, kd_o, b_o) in enumerate(((lw0_o, kd0_o, b0_o), (lw1_o, kd1_o, b1_o))):
        w_pre = w0_ref[d] + _mm(wd, wup_ref[d])
        put(lw_o, -math.exp(-0.5) * _sigmoid(w_pre))
        a = _sigmoid(a0_ref[d] + _mm(ad, aup_ref[d]))
        kd = k * (1.0 + (a - 1.0) * ka_ref[...])
        put(kd_o, kd)
        put(b_o, kk * a)
        kd_sum = kd if kd_sum is None else kd_sum + kd
    g_o[0] = _mm(gd, gup_ref[...]).astype(g_o.dtype)
    bv_o[0] = (_segsum(r * kd_sum * rk_ref[...], bd) * v).astype(bv_o.dtype)


def _head_blockdiag(width):
    idx = jnp.arange(width) // HEAD_DIM
    return (idx[:, None] == idx[None, :]).astype(BF16)


def _rwkv_prep(zb, lp, *, t_rows, grid_mode):
    nb, l, cols = zb.shape
    w = lp['w0'].shape[-1]
    npair = w // LANES
    hal = GRID_W
    r = t_rows // hal
    nh = l // hal
    lora = lp['w_up'].shape[1]

    def pad_dir(up):
        z = jnp.zeros_like(up[0])
        return jnp.stack([jnp.concatenate([up[0], z], 0), jnp.concatenate([z, up[1]], 0)], 0)

    const2 = lambda b, i: (0, 0)
    const3 = lambda b, i: (0, 0, 0)
    pair_spec = pl.BlockSpec((1, npair, t_rows, LANES), lambda b, i: (b, 0, i, 0))
    wide_spec = pl.BlockSpec((1, t_rows, w), lambda b, i: (b, i, 0))
    pair_shape = lambda dt: jax.ShapeDtypeStruct((nb, npair, l, LANES), dt)
    wide_shape = jax.ShapeDtypeStruct((nb, l, w), BF16)
    return pl.pallas_call(
        functools.partial(_rwkv_prep_kernel, t_rows=t_rows, grid_mode=grid_mode, width=w),
        grid=(nb, l // t_rows),
        in_specs=[pl.BlockSpec((1, hal, cols), lambda b, i: (b, jnp.maximum(i * r - 1, 0), 0)),
                  pl.BlockSpec((1, t_rows, cols), lambda b, i: (b, i, 0)),
                  pl.BlockSpec((1, hal, cols), lambda b, i: (b, jnp.minimum((i + 1) * r, nh - 1), 0)),
                  pl.BlockSpec((1, cols), const2),
                  pl.BlockSpec((2, 1, w), const3),
                  pl.BlockSpec((2, 2 * lora, w), const3),
                  pl.BlockSpec((2, 1, w), const3),
                  pl.BlockSpec((2, 2 * lora, w), const3),
                  pl.BlockSpec(lp['g_up'].shape, const2),
                  pl.BlockSpec((1, w), const2),
                  pl.BlockSpec((1, w), const2),
                  pl.BlockSpec((1, w), const2),
                  pl.BlockSpec((LANES, LANES), const2)],
        out_specs=[pair_spec] * 9 + [wide_spec] * 2,
        out_shape=[pair_shape(F32)] * 2 + [pair_shape(BF16)] * 7 + [wide_shape] * 2,
        scratch_shapes=[pltpu.VMEM((t_rows + 2 * hal, cols), F32)],
        compiler_params=_params("arbitrary", "arbitrary"),
        name="rwkv_prep",
    )(zb, zb, zb, lp['mu'].reshape(1, cols), lp['w0'].reshape(2, 1, w),
      pad_dir(lp['w_up']).astype(BF16), lp['a0'].reshape(2, 1, w),
      pad_dir(lp['a_up']).astype(BF16), lp['g_up'].astype(BF16), lp['k_k'].reshape(1, w),
      lp['k_a'].reshape(1, w), lp['r_k'].reshape(1, w), _head_blockdiag(LANES))


def _stack2(x, m0):
    return jnp.concatenate([jnp.where(m0, x, 0.0), jnp.where(m0, 0.0, x)], axis=0)


_NN = (((1,), (0,)), ((), ()))
_NT = (((1,), (1,)), ((), ()))
_TN = (((0,), (0,)), ((), ()))

SCAN_PREC = {"cum": "rhs2", "gram": "bf16", "init": "bf16", "apply": "bf16", "out": "bf16",
             "state": "bf16"}
SCAN_LOCKSTEP_PAIRS = 8

def _split(a):
    hi = a.astype(BF16)
    return hi, (a - hi.astype(F32)).astype(BF16)


def _dg(a, b, dims, site):
    mode = SCAN_PREC[site]
    if mode == "f32":
        return lax.dot_general(a, b, dims, precision=HIGHEST, preferred_element_type=F32)
    if mode == "bf16":
        return lax.dot_general(a.astype(BF16), b.astype(BF16), dims, preferred_element_type=F32)
    b_hi, b_lo = _split(b)
    if mode == "rhs2":
        a16 = a.astype(BF16)
        return (lax.dot_general(a16, b_hi, dims, preferred_element_type=F32)
                + lax.dot_general(a16, b_lo, dims, preferred_element_type=F32))
    a_hi, a_lo = _split(a)
    return (lax.dot_general(a_hi, b_hi, dims, preferred_element_type=F32)
            + lax.dot_general(a_hi, b_lo, dims, preferred_element_type=F32)
            + lax.dot_general(a_lo, b_hi, dims, preferred_element_type=F32))


def _chunk_step(refs, s_ref, *, rev, tri, strict, incl, diag, m0):
    lw, kd, b, kk, v, r = (x[...].astype(F32) for x in refs)
    s = s_ref[...]
    c = lw.shape[0]
    cum = _dg(tri, lw, _NN, "cum")
    yield
    cum_prev = cum - lw
    end = 0 if rev else c - 1
    tot = cum[end:end + 1, :]
    mid = cum[c // 2:c // 2 + 1, :]
    e_inv = jnp.exp(mid - cum)
    nkk = -kk
    left = jnp.concatenate([_stack2(nkk * jnp.exp(cum_prev - mid), m0),
                            _stack2(r * jnp.exp(cum - mid), m0)], axis=0)
    right = jnp.concatenate([_stack2(b * e_inv, m0), _stack2(kd * e_inv, m0)], axis=0)
    gram = _dg(left, right, _NT, "gram")
    yield
    c2 = 2 * c
    a_ab = jnp.where(strict, gram[0:c2, 0:c2], 0.0)
    a_ak = jnp.where(strict, gram[0:c2, c2:2 * c2], 0.0)
    a_rb = jnp.where(incl, gram[c2:2 * c2, 0:c2], 0.0)
    a_rk = jnp.where(incl, gram[c2:2 * c2, c2:2 * c2], 0.0)
    vsw = pltpu.roll(v, HEAD_DIM, 1)
    vbd = jnp.concatenate([jnp.where(m0, 0.0, vsw), jnp.where(m0, vsw, 0.0)], axis=0)
    e_end = jnp.exp(tot - cum)
    x = (_dg(_stack2(nkk * jnp.exp(cum_prev), m0), s, _NT, "init")
         + _dg(a_ak, vbd, _NN, "init"))
    ybd = _dg(_stack2(r * jnp.exp(cum), m0), s, _NT, "out") + _dg(a_rk, vbd, _NN, "out")
    s_new = s * jnp.exp(tot) + _dg(vbd, _stack2(kd * e_end, m0), _TN, "state")
    steps = max(1, (c - 1).bit_length())
    pw = a_ab
    for it in range(steps):
        yield
        if it < steps - 1:
            both = _dg(pw, pw + x, _NN, "apply")
            x = x + jnp.where(diag, 0.0, both)
            pw = jnp.where(diag, both, 0.0)
        else:
            x = x + _dg(pw, x, _NN, "apply")
    yield
    u = x
    ybd = ybd + _dg(a_rb, u, _NN, "out")
    y = pltpu.roll(ybd[0:c] + ybd[c:c2], HEAD_DIM, 1)
    s_new = s_new + _dg(u, _stack2(b * e_end, m0), _TN, "state")
    return y, s_new


def _lockstep(gens):
    results = [None] * len(gens)
    live = list(range(len(gens)))
    while live:
        still = []
        for i in live:
            try:
                next(gens[i])
                still.append(i)
            except StopIteration as stop:
                results[i] = stop.value
        live = still
    return results


def _rwkv_scan_kernel(lw0, kd0, b0, kkf, vf, rf, lw1, kd1, b1, kkb, vb, rb, s0_ref,
                      yf_o, yb_o, sfin_o, st):
    ci = pl.program_id(1)
    c = lw0.shape[2]
    npair = lw0.shape[1]

    @pl.when(ci == 0)
    def _():
        st[...] = s0_ref[:, 0]

    row = lax.broadcasted_iota(jnp.int32, (c, c), 0)
    colm = lax.broadcasted_iota(jnp.int32, (c, c), 1)
    tri_f = (colm <= row).astype(F32)
    tri_b = (colm >= row).astype(F32)
    rows2 = lax.broadcasted_iota(jnp.int32, (2 * c, 2 * c), 0)
    cols2 = lax.broadcasted_iota(jnp.int32, (2 * c, 2 * c), 1)
    r2 = rows2 & (c - 1)
    c2 = cols2 & (c - 1)
    diag = (rows2 < c) == (cols2 < c)
    m0 = lax.broadcasted_iota(jnp.int32, (1, LANES), 1) < HEAD_DIM

    fwd_refs = (lw0, kd0, b0, kkf, vf, rf)
    bwd_refs = (lw1, kd1, b1, kkb, vb, rb)
    for p0 in range(0, npair, SCAN_LOCKSTEP_PAIRS):
        gens, outs = [], []
        for p in range(p0, min(p0 + SCAN_LOCKSTEP_PAIRS, npair)):
            gens.append(_chunk_step([x.at[0, p] for x in fwd_refs], st.at[0, p], rev=False,
                                    tri=tri_f, strict=c2 < r2, incl=c2 <= r2, diag=diag, m0=m0))
            outs.append((yf_o, 0, p))
            gens.append(_chunk_step([x.at[0, p] for x in bwd_refs], st.at[1, p], rev=True,
                                    tri=tri_b, strict=c2 > r2, incl=c2 >= r2, diag=diag, m0=m0))
            outs.append((yb_o, 1, p))
        for (y_o, d, p), (y, s_new) in zip(outs, _lockstep(gens)):
            y_o[0, p] = y.astype(y_o.dtype)
            st[d, p] = s_new

    @pl.when(ci == pl.num_programs(1) - 1)
    def _():
        sfin_o[:, 0] = st[...]


def _rwkv_scan(prep, s0):
    lw0, lw1, kd0, kd1, b0, b1, kk, v, r = prep
    nb, npair, l, _ = lw0.shape
    c = SCAN_CHUNK
    assert c == HEAD_DIM and 2 * HEAD_DIM == LANES and l % c == 0
    n = l // c
    fwd = pl.BlockSpec((1, npair, c, LANES), lambda b, i: (b, 0, i, 0))
    bwd = pl.BlockSpec((1, npair, c, LANES), lambda b, i: (b, 0, n - 1 - i, 0))
    st_spec = pl.BlockSpec((2, 1, npair, LANES, LANES), lambda b, i: (0, b, 0, 0, 0))
    y_shape = jax.ShapeDtypeStruct((nb, npair, l, LANES), BF16)
    return pl.pallas_call(
        _rwkv_scan_kernel,
        grid=(nb, n),
        in_specs=[fwd] * 6 + [bwd] * 6 + [st_spec],
        out_specs=[fwd, bwd, st_spec],
        out_shape=[y_shape, y_shape, jax.ShapeDtypeStruct(s0.shape, F32)],
        scratch_shapes=[pltpu.VMEM((2, npair, LANES, LANES), F32)],
        compiler_params=_params("arbitrary", "arbitrary"),
        name="rwkv_scan",
    )(lw0, kd0, b0, kk, v, r, lw1, kd1, b1, kk, v, r, s0)


def _rwkv_post_kernel(yf_ref, yb_ref, bv_ref, g_ref, lng_ref, lnb_ref, bd_ref, o_ref):
    npair = yf_ref.shape[1]
    y = jnp.concatenate([yf_ref[0, p].astype(F32) + yb_ref[0, p].astype(F32)
                         for p in range(npair)], axis=-1)
    bd = bd_ref[...]
    inv = 1.0 / HEAD_DIM
    m = _segsum(y, bd) * inv
    dlt = y - m
    var = _segsum(dlt * dlt, bd) * inv
    yn = dlt * lax.rsqrt(var + RWKV_GN_EPS) * lng_ref[...] + lnb_ref[...]
    o_ref[0] = ((yn + bv_ref[0].astype(F32)) * g_ref[0].astype(F32)).astype(o_ref.dtype)


def _rwkv_post(yf, yb, bv, g, ln_g, ln_b, *, t_rows):
    nb, npair, l, _ = yf.shape
    w = npair * LANES
    pair_spec = pl.BlockSpec((1, npair, t_rows, LANES), lambda b, i: (b, 0, i, 0))
    wide_spec = pl.BlockSpec((1, t_rows, w), lambda b, i: (b, i, 0))
    const2 = lambda b, i: (0, 0)
    return pl.pallas_call(
        _rwkv_post_kernel,
        grid=(nb, l // t_rows),
        in_specs=[pair_spec, pair_spec, wide_spec, wide_spec,
                  pl.BlockSpec((1, w), const2), pl.BlockSpec((1, w), const2),
                  pl.BlockSpec((LANES, LANES), const2)],
        out_specs=wide_spec,
        out_shape=jax.ShapeDtypeStruct((nb, l, w), BF16),
        compiler_params=_params("arbitrary", "arbitrary"),
        name="rwkv_post",
    )(yf, yb, bv, g, ln_g.reshape(1, w), ln_b.reshape(1, w), _head_blockdiag(LANES))


def _merge_kernel(h_ref, hn_ref, mod_ref, yp_ref, yr_ref, yg_ref, wzg_ref, pp_ref, pr_ref, pg_ref,
                  wo_ref, o_ref):
    @pl.when(pl.program_id(2) == 0)
    def _():
        o_ref[0] = h_ref[0]

    hn = hn_ref[0]
    merged = None
    for br, (y_ref, p_ref) in enumerate(((yp_ref, pp_ref), (yr_ref, pr_ref), (yg_ref, pg_ref))):
        gate = _sigmoid(jnp.dot(hn, wzg_ref[br], preferred_element_type=F32))
        term = gate * jnp.dot(y_ref[0], p_ref[...], preferred_element_type=F32)
        merged = term if merged is None else merged + term
    o_ref[0] += mod_ref[0, 5:6, :] * jnp.dot(merged.astype(BF16), wo_ref[...],
                                             preferred_element_type=F32)


def _merge(h, hn, mod, yp, yr, yg, wzg, pp, pr, pg, wo, *, li, tm, tn):
    nb, l, d = h.shape
    per_batch = mod.shape[0] > 1
    mod_map = (lambda b, i, n: (b, 0, 0)) if per_batch else (lambda b, i, n: (0, 0, 0))
    row = lambda width: pl.BlockSpec((1, tm, width), lambda b, i, n: (b, i, 0))
    col = lambda a: pl.BlockSpec((None, a.shape[1], tn), lambda b, i, n: (li, 0, n))
    return pl.pallas_call(
        _merge_kernel,
        grid=(nb, l // tm, d // tn),
        in_specs=[row(d), row(d), pl.BlockSpec((1, N_MOD, d), mod_map),
                  row(yp.shape[2]), row(yr.shape[2]), row(yg.shape[2]),
                  pl.BlockSpec((wzg.shape[0], d, tn), lambda b, i, n: (0, 0, n)),
                  col(pp), col(pr), col(pg),
                  pl.BlockSpec((None, tn, d), lambda b, i, n: (li, n, 0))],
        out_specs=row(d),
        out_shape=jax.ShapeDtypeStruct((nb, l, d), F32),
        compiler_params=_params("arbitrary", "arbitrary", "arbitrary"),
        name="merge",
    )(h, hn, mod, yp, yr, yg, wzg, pp, pr, pg, wo)


def _pick(n, pref):
    t = min(pref, n)
    while t > LANES and (n % t or t % LANES):
        t -= LANES
    return t if n % t == 0 else n


def kernel(x, c, ctx, c_ctx, ada_w, ada_b, norm_g, ffn_w_gate, ffn_w_up, ffn_w_down, w_in,
           pool_w, pool_scale, rwkv_mu, rwkv_w0, rwkv_w_up, rwkv_a0, rwkv_a_up, rwkv_g_up,
           rwkv_k_k, rwkv_k_a, rwkv_r_k, rwkv_ln_g, rwkv_ln_b, gmlp_ln_g, gmlp_ln_b, gmlp_ws,
           gmlp_bs, proj_pool, proj_rwkv, proj_gmlp, w_out, final_norm):
    depth = ada_w.shape[0]
    nb, l, d = x.shape
    lc = ctx.shape[1]
    width = rwkv_w0.shape[-1]
    pool_c = pool_scale.shape[-1]
    gmlp_c = gmlp_ln_g.shape[-1]
    rwkv_cols = rwkv_mu.shape[-1]
    off_rwkv = pool_c
    off_gmlp = off_rwkv + rwkv_cols
    off_gate = off_gmlp + 2 * gmlp_c
    npair = width // LANES

    cond8 = jnp.zeros((8, d), F32).at[:nb].set(c).at[nb].set(c_ctx)
    mod_all = _adaln(cond8, ada_w, ada_b).reshape(depth, 8, N_MOD, d)

    zero_state = jnp.zeros((2, nb, npair, LANES, LANES), F32)
    wg, wu, wd = (a.astype(BF16) for a in (ffn_w_gate, ffn_w_up, ffn_w_down))
    pp, pr, pg, wo = (a.astype(BF16) for a in (proj_pool, proj_rwkv, proj_gmlp, w_out))
    win_all = w_in.astype(BF16)
    tf = _pick(wg.shape[-1], 512)
    assert (2 * gmlp_c) % pool_c == 0
    hc = ctx.reshape(1, nb * lc, d)
    for li in range(depth):
        last = li == depth - 1
        mod_x = mod_all[li, :nb]
        mod_c = mod_all[li, nb:nb + 1]
        win = win_all[li]
        w_zb = win[:, off_rwkv:off_gmlp]
        w_uva = jnp.concatenate([win[:, off_gmlp:off_gate], win[:, :off_rwkv]], axis=1)
        w_zg = win[:, off_gate:].reshape(d, -1, d).transpose(1, 0, 2)
        lp = {'mu': rwkv_mu[li], 'w0': rwkv_w0[li], 'w_up': rwkv_w_up[li], 'a0': rwkv_a0[li],
              'a_up': rwkv_a_up[li], 'g_up': rwkv_g_up[li], 'k_k': rwkv_k_k[li],
              'k_a': rwkv_k_a[li], 'r_k': rwkv_r_k[li]}

        def mixer(h, hn, mod, nseq, seq, grid_mode, s0, need_out):
            flat = h.shape[:2]
            tm = _pick(flat[1], 1024)
            zb = _proj(hn, w_zb, tm=tm, tn=_pick(rwkv_cols, 1152))
            prep = _rwkv_prep(zb.reshape(nseq, seq, -1), lp, t_rows=_pick(seq, 256),
                              grid_mode=grid_mode)
            yf, yb, s_fin = _rwkv_scan(prep[:9], s0)
            if not need_out:
                return None, s_fin
            y_rwkv = _rwkv_post(yf, yb, prep[10], prep[9], rwkv_ln_g[li], rwkv_ln_b[li],
                                t_rows=_pick(seq, 256))
            z_uva = _proj(hn, w_uva, tm=tm,
                          tn=_pick(w_uva.shape[1], 768)).reshape(nseq, seq, -1)
            y_pool = _pool(z_uva, pool_w[li], pool_scale[li], col_block=2 * gmlp_c // pool_c,
                           t_rows=_pick(seq, 512))
            y_gmlp = _gmlp(z_uva, gmlp_ln_g[li], gmlp_ln_b[li], gmlp_ws[li], gmlp_bs[li],
                           t_rows=_pick(seq, 512))
            ys = [y.reshape(flat + (-1,)) for y in (y_pool, y_rwkv, y_gmlp)]
            out = _merge(h, hn, mod, *ys, w_zg, pp, pr, pg, wo,
                         li=li, tm=_pick(flat[1], 512), tn=_pick(d, 512))
            return out, s_fin

        def ffn(h, mod, k, tail, tail_g):
            return _ffn(h, mod, norm_g[li, 2 * k], wg, wu, wd, tail_g, sel=(li, k),
                        base=6 * k, tail=tail, tm=_pick(h.shape[1], 512), tf=tf)

        hc, hnc = ffn(hc, mod_c, 0, "mixer_norm", norm_g[li, 1])
        hc_new, s_ctx = mixer(hc, hnc, mod_c, nb, lc, False, zero_state, not last)
        if not last:
            hc = ffn(hc_new, mod_c, 1, None, final_norm)

        x, hn = ffn(x, mod_x, 0, "mixer_norm", norm_g[li, 1])
        x, _ = mixer(x, hn, mod_x, nb, l, True, s_ctx, True)
        x = ffn(x, mod_x, 1, "final" if last else None, final_norm)
    return x
```

```python
import functools
import math

import jax
import jax.numpy as jnp
from jax import lax
from jax.experimental import pallas as pl
from jax.experimental.pallas import tpu as pltpu

F32 = jnp.float32
BF16 = jnp.bfloat16
HIGHEST = lax.Precision.HIGHEST

NORM_EPS = 1e-6
LN_EPS = 1e-5
RWKV_GN_EPS = 64e-5
GRID_W = 64
N_MOD = 9
POOL_WINDOWS = (2, 4, 8, 16)
HEAD_DIM = 64
GMLP_CHUNK = 128
LANES = 128
SCAN_CHUNK = 64
VMEM_LIMIT = 56 * 1024 * 1024


def _params(*sem):
    return pltpu.CompilerParams(dimension_semantics=sem, vmem_limit_bytes=VMEM_LIMIT)


def _sigmoid(x):
    return 0.5 * jnp.tanh(0.5 * x) + 0.5


def _mm(a, b):
    return jnp.dot(a.astype(BF16), b.astype(BF16), preferred_element_type=F32)


def _mm_hi(a, b):
    return jnp.dot(a, b, precision=HIGHEST, preferred_element_type=F32)


def _mm_nt_hi(a, b):
    return lax.dot_general(a, b, (((1,), (1,)), ((), ())), precision=HIGHEST,
                           preferred_element_type=F32)


def _mm_tn_hi(a, b):
    return lax.dot_general(a, b, (((0,), (0,)), ((), ())), precision=HIGHEST,
                           preferred_element_type=F32)


def _rms(x, g):
    return x * lax.rsqrt(jnp.mean(x * x, axis=-1, keepdims=True) + NORM_EPS) * g


def _adaln_kernel(c_ref, w_ref, b_ref, o_ref):
    c = c_ref[...]
    o_ref[0] = _mm_hi(c * _sigmoid(c), w_ref[0]) + b_ref[0]


def _adaln(cond8, ada_w, ada_b):
    depth, d, n = ada_w.shape
    tn = 1024
    return pl.pallas_call(
        _adaln_kernel,
        grid=(depth, n // tn),
        in_specs=[pl.BlockSpec((8, d), lambda l, j: (0, 0)),
                  pl.BlockSpec((1, d, tn), lambda l, j: (l, 0, j)),
                  pl.BlockSpec((1, 1, tn), lambda l, j: (l, 0, j))],
        out_specs=pl.BlockSpec((1, 8, tn), lambda l, j: (l, 0, j)),
        out_shape=jax.ShapeDtypeStruct((depth, 8, n), F32),
        compiler_params=_params("arbitrary", "arbitrary"),
        name="adaln",
    )(cond8, ada_w, ada_b.reshape(depth, 1, n))


FFN_WEIGHT_SLOTS = 2


def _ffn_kernel(h_ref, mod_ref, hnext_ref, modnext_ref, g_ref, wg_hbm, wu_hbm, wd_hbm,
                tail_ref, *rest, base, tail, sel, tf):
    if tail == "mixer_norm":
        o_ref, hn_o_ref, hn_ref, hn_nxt, wg_buf, wu_buf, wd_buf, sem = rest
    else:
        o_ref, hn_ref, hn_nxt, wg_buf, wu_buf, wd_buf, sem = rest
    li, k = sel
    nf = wd_hbm.shape[2] // tf
    step = pl.program_id(0) * pl.num_programs(1) + pl.program_id(1)
    nsteps = pl.num_programs(0) * pl.num_programs(1)
    first_tile = step * nf

    def copies(j, slot):
        cols = pl.ds(j * tf, tf)
        return (pltpu.make_async_copy(wg_hbm.at[li, k, :, cols], wg_buf.at[slot], sem.at[slot, 0]),
                pltpu.make_async_copy(wu_hbm.at[li, k, :, cols], wu_buf.at[slot], sem.at[slot, 1]),
                pltpu.make_async_copy(wd_hbm.at[li, k, cols, :], wd_buf.at[slot], sem.at[slot, 2]))

    def normed_input(x_ref, m_ref, rows):
        hn = _rms(x_ref[0, rows], g_ref[...]) * (1.0 + m_ref[0, base + 1:base + 2, :]) \
            + m_ref[0, base:base + 1, :]
        return hn.astype(BF16)

    tm = h_ref.shape[1]

    @pl.when(step == 0)
    def _():
        for cp in copies(0, 0):
            cp.start()
        hn_ref[...] = normed_input(h_ref, mod_ref, slice(0, tm))

    @pl.when(step > 0)
    def _():
        hn_ref[...] = hn_nxt[...]

    half_gate = 0.5 * mod_ref[0, base + 2:base + 3, :]
    piece = -(-tm // nf)
    piece = -(-piece // 16) * 16

    for j in range(nf):
        slot = (first_tile + j) % FFN_WEIGHT_SLOTS
        nxt = (first_tile + j + 1) % FFN_WEIGHT_SLOTS
        if j + 1 < nf:
            for cp in copies(j + 1, nxt):
                cp.start()
        else:
            @pl.when(step + 1 < nsteps)
            def _():
                for cp in copies(0, nxt):
                    cp.start()
        for cp in copies(j, slot):
            cp.wait()
        rows = slice(min(j * piece, tm), min((j + 1) * piece, tm))
        if rows.start < rows.stop:
            hn_nxt[rows] = normed_input(hnext_ref, modnext_ref, rows)
        hn = hn_ref[...]
        gate = jnp.dot(hn, wg_buf[slot], preferred_element_type=F32)
        up = jnp.dot(hn, wu_buf[slot], preferred_element_type=F32)
        act = (gate * _sigmoid(gate) * up).astype(BF16)
        acc = h_ref[0] if j == 0 else o_ref[0]
        o_ref[0] = acc + half_gate * jnp.dot(act, wd_buf[slot], preferred_element_type=F32)

    if tail is not None:
        normed = _rms(o_ref[0], tail_ref[...])
        if tail == "final":
            o_ref[0] = normed
        else:
            hn_o_ref[0] = (normed * (1.0 + mod_ref[0, 4:5, :])
                           + mod_ref[0, 3:4, :]).astype(BF16)


def _ffn(h, mod, g, wg, wu, wd, tail_g, *, sel, base, tail, tm, tf):
    nb, l, d = h.shape
    assert wg.shape[-1] % tf == 0 and l % tm == 0
    per_batch = mod.shape[0] > 1
    mod_map = (lambda b, i: (b, 0, 0)) if per_batch else (lambda b, i: (0, 0, 0))
    ni = l // tm

    def nxt(b, i):
        lin = jnp.minimum(b * ni + i + 1, nb * ni - 1)
        return lin // ni, lin % ni

    row = pl.BlockSpec((1, tm, d), lambda b, i: (b, i, 0))
    row_next = pl.BlockSpec((1, tm, d), lambda b, i: nxt(b, i) + (0,))
    mod_next = pl.BlockSpec((1, N_MOD, d), (lambda b, i: (nxt(b, i)[0], 0, 0)) if per_batch
                            else (lambda b, i: (0, 0, 0)))
    hbm = pl.BlockSpec(memory_space=pl.ANY)
    out_specs, out_shape = row, jax.ShapeDtypeStruct((nb, l, d), F32)
    if tail == "mixer_norm":
        out_specs, out_shape = [row, row], [out_shape, jax.ShapeDtypeStruct((nb, l, d), BF16)]
    slots = FFN_WEIGHT_SLOTS
    return pl.pallas_call(
        functools.partial(_ffn_kernel, base=base, tail=tail, sel=sel, tf=tf),
        grid=(nb, ni),
        in_specs=[row,
                  pl.BlockSpec((1, N_MOD, d), mod_map),
                  row_next, mod_next,
                  pl.BlockSpec((1, d), lambda b, i: (0, 0)),
                  hbm, hbm, hbm,
                  pl.BlockSpec((1, d), lambda b, i: (0, 0))],
        out_specs=out_specs,
        out_shape=out_shape,
        scratch_shapes=[pltpu.VMEM((tm, d), BF16),
                        pltpu.VMEM((tm, d), BF16),
                        pltpu.VMEM((slots, d, tf), BF16),
                        pltpu.VMEM((slots, d, tf), BF16),
                        pltpu.VMEM((slots, tf, d), BF16),
                        pltpu.SemaphoreType.DMA((slots, 3))],
        compiler_params=_params("arbitrary", "arbitrary"),
        name="ffn",
    )(h, mod, h, mod, g.reshape(1, d), wg, wu, wd, tail_g.reshape(1, d))


def _proj_kernel(hn_ref, w_ref, o_ref):
    o_ref[0] = jnp.dot(hn_ref[0], w_ref[...], preferred_element_type=F32)


def _proj(hn, w, *, tm, tn):
    nb, l, d = hn.shape
    n = w.shape[1]
    return pl.pallas_call(
        _proj_kernel,
        grid=(nb, l // tm, n // tn),
        in_specs=[pl.BlockSpec((1, tm, d), lambda b, i, j: (b, i, 0)),
                  pl.BlockSpec((d, tn), lambda b, i, j: (0, j))],
        out_specs=pl.BlockSpec((1, tm, tn), lambda b, i, j: (b, i, j)),
        out_shape=jax.ShapeDtypeStruct((nb, l, n), F32),
        compiler_params=_params("arbitrary", "arbitrary", "arbitrary"),
        name="proj",
    )(hn, w)


POOL_HALO = 8


def _pool_kernel(prev_ref, x_ref, next_ref, pw_ref, ps_ref, o_ref, scr, *, t_rows, seq):
    i = pl.program_id(1)
    last = pl.num_programs(1) - 1
    hal = POOL_HALO
    scr[0:hal] = jnp.where(i > 0, prev_ref[0], 0.0)
    scr[hal:hal + t_rows] = x_ref[0]
    scr[hal + t_rows:2 * hal + t_rows] = jnp.where(i < last, next_ref[0], 0.0)
    t = i * t_rows + lax.broadcasted_iota(jnp.int32, (t_rows, LANES), 0)
    for gi, w in enumerate(POOL_WINDOWS):
        c0 = gi * LANES
        s = scr[hal - w // 2:hal - w // 2 + t_rows, c0:c0 + LANES]
        for dlt in range(-(w // 2) + 1, w // 2):
            s = s + scr[hal + dlt:hal + dlt + t_rows, c0:c0 + LANES]
        lo = jnp.clip(t - w // 2, 0, seq)
        hi = jnp.clip(t - w // 2 + w, 0, seq)
        cnt = (hi - lo).astype(F32)
        p = s / cnt - scr[hal:hal + t_rows, c0:c0 + LANES]
        y = _mm(p, pw_ref[gi]) * ps_ref[:, c0:c0 + LANES]
        o_ref[0, :, c0:c0 + LANES] = y.astype(o_ref.dtype)


def _pool(z, pool_w, pool_scale, *, col_block, t_rows):
    nb, l, _ = z.shape
    c = pool_scale.shape[-1]
    hal = POOL_HALO
    r = t_rows // hal
    nh = l // hal
    cb = col_block
    return pl.pallas_call(
        functools.partial(_pool_kernel, t_rows=t_rows, seq=l),
        grid=(nb, l // t_rows),
        in_specs=[pl.BlockSpec((1, hal, c), lambda b, i: (b, jnp.maximum(i * r - 1, 0), cb)),
                  pl.BlockSpec((1, t_rows, c), lambda b, i: (b, i, cb)),
                  pl.BlockSpec((1, hal, c), lambda b, i: (b, jnp.minimum((i + 1) * r, nh - 1), cb)),
                  pl.BlockSpec(pool_w.shape, lambda b, i: (0, 0, 0)),
                  pl.BlockSpec((1, c), lambda b, i: (0, 0))],
        out_specs=pl.BlockSpec((1, t_rows, c), lambda b, i: (b, i, 0)),
        out_shape=jax.ShapeDtypeStruct((nb, l, c), BF16),
        scratch_shapes=[pltpu.VMEM((t_rows + 2 * hal, c), F32)],
        compiler_params=_params("arbitrary", "arbitrary"),
        name="pool",
    )(z, z, z, pool_w.astype(BF16), pool_scale.reshape(1, c))


def _gelu(x):
    return x * (0.5 * (1.0 + jnp.tanh(math.sqrt(2.0 / math.pi) * (x + 0.044715 * (x * x * x)))))


def _gmlp_kernel(z_ref, lng_ref, lnb_ref, ws_ref, bsb_ref, o_ref, *, t_rows):
    width = o_ref.shape[2]
    groups = ws_ref.shape[0]
    gd = width // groups
    for ci in range(t_rows // GMLP_CHUNK):
        rows = slice(ci * GMLP_CHUNK, (ci + 1) * GMLP_CHUNK)
        u = _gelu(z_ref[0, rows, 0:width])
        v = _gelu(z_ref[0, rows, width:2 * width])
        mu = jnp.mean(v, axis=-1, keepdims=True)
        var = jnp.mean(jnp.square(v - mu), axis=-1, keepdims=True)
        vn = ((v - mu) * lax.rsqrt(var + LN_EPS)) * lng_ref[...] + lnb_ref[...]
        for g in range(groups):
            cols = slice(g * gd, (g + 1) * gd)
            s = _mm(ws_ref[g], vn[:, cols]) + bsb_ref[g]
            o_ref[0, rows, cols] = (u[:, cols] * s).astype(o_ref.dtype)


def _gmlp(zuv, ln_g, ln_b, ws, bs, *, t_rows):
    nb, l, _ = zuv.shape
    c = ln_g.shape[-1]
    c2 = 2 * c
    groups = ws.shape[0]
    bsb = jnp.broadcast_to(bs[:, :, None], (groups, GMLP_CHUNK, c // groups))
    return pl.pallas_call(
        functools.partial(_gmlp_kernel, t_rows=t_rows),
        grid=(nb, l // t_rows),
        in_specs=[pl.BlockSpec((1, t_rows, c2), lambda b, i: (b, i, 0)),
                  pl.BlockSpec((1, c), lambda b, i: (0, 0)),
                  pl.BlockSpec((1, c), lambda b, i: (0, 0)),
                  pl.BlockSpec(ws.shape, lambda b, i: (0, 0, 0)),
                  pl.BlockSpec(bsb.shape, lambda b, i: (0, 0, 0))],
        out_specs=pl.BlockSpec((1, t_rows, c), lambda b, i: (b, i, 0)),
        out_shape=jax.ShapeDtypeStruct((nb, l, c), BF16),
        compiler_params=_params("arbitrary", "arbitrary"),
        name="gmlp",
    )(zuv, ln_g.reshape(1, c), ln_b.reshape(1, c), ws.astype(BF16), bsb)


def _segsum(x, bd):
    hi = x.astype(BF16)
    lo = (x - hi.astype(F32)).astype(BF16)
    out = []
    for p in range(x.shape[-1] // LANES):
        cols = slice(p * LANES, (p + 1) * LANES)
        out.append(jnp.dot(hi[:, cols], bd, preferred_element_type=F32)
                   + jnp.dot(lo[:, cols], bd, preferred_element_type=F32))
    return jnp.concatenate(out, axis=-1)


def _rwkv_prep_kernel(prev_ref, x_ref, next_ref, mu_ref, w0_ref, wup_ref, a0_ref, aup_ref,
                      gup_ref, kk_ref, ka_ref, rk_ref, bd_ref,
                      lw0_o, lw1_o, kd0_o, kd1_o, b0_o, b1_o, kk_o, v_o, r_o, g_o, bv_o,
                      scr, *, t_rows, grid_mode, width):
    i = pl.program_id(1)
    last = pl.num_programs(1) - 1
    hal = GRID_W
    scr[0:hal] = jnp.where(i > 0, prev_ref[0], 0.0)
    scr[hal:hal + t_rows] = x_ref[0]
    scr[hal + t_rows:2 * hal + t_rows] = jnp.where(i < last, next_ref[0], 0.0)
    cols = scr.shape[1]
    if grid_mode:
        q = cols // 4
        bounds = (0, q, 2 * q, 3 * q, cols)
        offs = (-1, 1, -GRID_W, GRID_W)
    else:
        bounds = (0, cols // 2, cols)
        offs = (-1, 1)

    def zs_cols(c0, c1):
        n = c1 - c0
        x = scr[hal:hal + t_rows, c0:c1]
        ch = c0 + lax.broadcasted_iota(jnp.int32, (t_rows, n), 1)
        col = lax.broadcasted_iota(jnp.int32, (t_rows, n), 0) & (GRID_W - 1)
        shifted = None
        for qi, off in enumerate(offs):
            lo, hi = bounds[qi], bounds[qi + 1]
            if hi <= c0 or lo >= c1:
                continue
            src = scr[hal + off:hal + off + t_rows, c0:c1]
            if grid_mode and off == -1:
                src = jnp.where(col == 0, 0.0, src)
            if grid_mode and off == 1:
                src = jnp.where(col == GRID_W - 1, 0.0, src)
            shifted = src if shifted is None else jnp.where(ch >= lo, src, shifted)
        return x + (shifted - x) * mu_ref[:, c0:c1]

    w = width
    r = zs_cols(0, w)
    k = zs_cols(w, 2 * w)
    v = zs_cols(2 * w, 3 * w)
    rest = zs_cols(3 * w, cols)
    wd = jnp.tanh(rest[:, 0:LANES])
    ad = rest[:, LANES:2 * LANES]
    gd = _sigmoid(rest[:, 2 * LANES:3 * LANES])

    bd = bd_ref[...]
    kk0 = k * kk_ref[...]
    ss = _segsum(kk0 * kk0, bd)
    kk = kk0 / jnp.maximum(jnp.sqrt(ss), 1e-12)
    npair = w // LANES

    def put(o_ref, val):
        for p in range(npair):
            o_ref[0, p] = val[:, p * LANES:(p + 1) * LANES].astype(o_ref.dtype)

    put(kk_o, kk)
    put(v_o, v)
    put(r_o, r)
    kd_sum = None
    for d, (lw_o, kd_o, b_o) in enumerate(((lw0_o, kd0_o, b0_o), (lw1_o, kd1_o, b1_o))):
        w_pre = w0_ref[d] + _mm(wd, wup_ref[d])
        put(lw_o, -math.exp(-0.5) * _sigmoid(w_pre))
        a = _sigmoid(a0_ref[d] + _mm(ad, aup_ref[d]))
        kd = k * (1.0 + (a - 1.0) * ka_ref[...])
        put(kd_o, kd)
        put(b_o, kk * a)
        kd_sum = kd if kd_sum is None else kd_sum + kd
    g_o[0] = _mm(gd, gup_ref[...]).astype(g_o.dtype)
    bv_o[0] = (_segsum(r * kd_sum * rk_ref[...], bd) * v).astype(bv_o.dtype)


def _head_blockdiag(width):
    idx = jnp.arange(width) // HEAD_DIM
    return (idx[:, None] == idx[None, :]).astype(BF16)


def _rwkv_prep(zb, lp, *, t_rows, grid_mode):
    nb, l, cols = zb.shape
    w = lp['w0'].shape[-1]
    npair = w // LANES
    hal = GRID_W
    r = t_rows // hal
    nh = l // hal
    lora = lp['w_up'].shape[1]

    def pad_dir(up):
        z = jnp.zeros_like(up[0])
        return jnp.stack([jnp.concatenate([up[0], z], 0), jnp.concatenate([z, up[1]], 0)], 0)

    const2 = lambda b, i: (0, 0)
    const3 = lambda b, i: (0, 0, 0)
    pair_spec = pl.BlockSpec((1, npair, t_rows, LANES), lambda b, i: (b, 0, i, 0))
    wide_spec = pl.BlockSpec((1, t_rows, w), lambda b, i: (b, i, 0))
    pair_shape = lambda dt: jax.ShapeDtypeStruct((nb, npair, l, LANES), dt)
    wide_shape = jax.ShapeDtypeStruct((nb, l, w), BF16)
    return pl.pallas_call(
        functools.partial(_rwkv_prep_kernel, t_rows=t_rows, grid_mode=grid_mode, width=w),
        grid=(nb, l // t_rows),
        in_specs=[pl.BlockSpec((1, hal, cols), lambda b, i: (b, jnp.maximum(i * r - 1, 0), 0)),
                  pl.BlockSpec((1, t_rows, cols), lambda b, i: (b, i, 0)),
                  pl.BlockSpec((1, hal, cols), lambda b, i: (b, jnp.minimum((i + 1) * r, nh - 1), 0)),
                  pl.BlockSpec((1, cols), const2),
                  pl.BlockSpec((2, 1, w), const3),
                  pl.BlockSpec((2, 2 * lora, w), const3),
                  pl.BlockSpec((2, 1, w), const3),
                  pl.BlockSpec((2, 2 * lora, w), const3),
                  pl.BlockSpec(lp['g_up'].shape, const2),
                  pl.BlockSpec((1, w), const2),
                  pl.BlockSpec((1, w), const2),
                  pl.BlockSpec((1, w), const2),
                  pl.BlockSpec((LANES, LANES), const2)],
        out_specs=[pair_spec] * 9 + [wide_spec] * 2,
        out_shape=[pair_shape(F32)] * 2 + [pair_shape(BF16)] * 7 + [wide_shape] * 2,
        scratch_shapes=[pltpu.VMEM((t_rows + 2 * hal, cols), F32)],
        compiler_params=_params("arbitrary", "arbitrary"),
        name="rwkv_prep",
    )(zb, zb, zb, lp['mu'].reshape(1, cols), lp['w0'].reshape(2, 1, w),
      pad_dir(lp['w_up']).astype(BF16), lp['a0'].reshape(2, 1, w),
      pad_dir(lp['a_up']).astype(BF16), lp['g_up'].astype(BF16), lp['k_k'].reshape(1, w),
      lp['k_a'].reshape(1, w), lp['r_k'].reshape(1, w), _head_blockdiag(LANES))


def _stack2(x, m0):
    xs = x.astype(BF16)
    zero = jnp.zeros_like(xs)
    return jnp.concatenate([jnp.where(m0, xs, zero), jnp.where(m0, zero, xs)], axis=0)


_NN = (((1,), (0,)), ((), ()))
_NT = (((1,), (1,)), ((), ()))
_TN = (((0,), (0,)), ((), ()))

SCAN_PREC = {"cum": "rhs2", "gram": "bf16", "init": "bf16", "apply": "bf16", "out": "bf16",
             "state": "bf16"}
SCAN_LOCKSTEP_PAIRS = 8

def _split(a):
    hi = a.astype(BF16)
    return hi, (a - hi.astype(F32)).astype(BF16)


def _dg(a, b, dims, site):
    mode = SCAN_PREC[site]
    if mode == "f32":
        return lax.dot_general(a, b, dims, precision=HIGHEST, preferred_element_type=F32)
    if mode == "bf16":
        return lax.dot_general(a.astype(BF16), b.astype(BF16), dims, preferred_element_type=F32)
    b_hi, b_lo = _split(b)
    if mode == "rhs2":
        a16 = a.astype(BF16)
        return (lax.dot_general(a16, b_hi, dims, preferred_element_type=F32)
                + lax.dot_general(a16, b_lo, dims, preferred_element_type=F32))
    a_hi, a_lo = _split(a)
    return (lax.dot_general(a_hi, b_hi, dims, preferred_element_type=F32)
            + lax.dot_general(a_hi, b_lo, dims, preferred_element_type=F32)
            + lax.dot_general(a_lo, b_hi, dims, preferred_element_type=F32))


def _chunk_step(refs, s_ref, *, rev, tri, strict, incl, diag, m0):
    lw, kd, b, kk, v, r = (x[...].astype(F32) for x in refs)
    s = s_ref[...]
    c = lw.shape[0]
    cum = _dg(tri, lw, _NN, "cum")
    yield
    cum_prev = cum - lw
    end = 0 if rev else c - 1
    tot = cum[end:end + 1, :]
    mid = cum[c // 2:c // 2 + 1, :]
    e_inv = jnp.exp(mid - cum)
    nkk = -kk
    left = jnp.concatenate([_stack2(nkk * jnp.exp(cum_prev - mid), m0),
                            _stack2(r * jnp.exp(cum - mid), m0)], axis=0)
    right = jnp.concatenate([_stack2(b * e_inv, m0), _stack2(kd * e_inv, m0)], axis=0)
    gram = _dg(left, right, _NT, "gram")
    yield
    c2 = 2 * c
    g16 = gram.astype(BF16)
    zero16 = jnp.zeros((c2, c2), BF16)
    a_ab = jnp.where(strict, g16[0:c2, 0:c2], zero16)
    a_ak = jnp.where(strict, g16[0:c2, c2:2 * c2], zero16)
    a_rb = jnp.where(incl, g16[c2:2 * c2, 0:c2], zero16)
    a_rk = jnp.where(incl, g16[c2:2 * c2, c2:2 * c2], zero16)
    vsw = pltpu.roll(v, HEAD_DIM, 1).astype(BF16)
    zv = jnp.zeros_like(vsw)
    vbd = jnp.concatenate([jnp.where(m0, zv, vsw), jnp.where(m0, vsw, zv)], axis=0)
    e_end = jnp.exp(tot - cum)
    x = (_dg(_stack2(nkk * jnp.exp(cum_prev), m0), s, _NT, "init")
         + _dg(a_ak, vbd, _NN, "init"))
    ybd = _dg(_stack2(r * jnp.exp(cum), m0), s, _NT, "out") + _dg(a_rk, vbd, _NN, "out")
    s_new = s * jnp.exp(tot) + _dg(vbd, _stack2(kd * e_end, m0), _TN, "state")
    steps = max(1, (c - 1).bit_length())
    pw = a_ab
    rhs = jnp.where(diag, pw, x.astype(BF16))
    for it in range(steps):
        yield
        both = _dg(pw, rhs, _NN, "apply")
        x = x + both
        if it < steps - 1:
            both16 = both.astype(BF16)
            pw = jnp.where(diag, both16, zero16)
            rhs = jnp.where(diag, both16, x.astype(BF16))
    yield
    u = jnp.where(diag, 0.0, x)
    ybd = ybd + _dg(a_rb, u, _NN, "out")
    y = pltpu.roll(ybd[0:c] + ybd[c:c2], HEAD_DIM, 1)
    s_new = s_new + _dg(u, _stack2(b * e_end, m0), _TN, "state")
    return y, s_new


def _lockstep(gens):
    results = [None] * len(gens)
    live = list(range(len(gens)))
    while live:
        still = []
        for i in live:
            try:
                next(gens[i])
                still.append(i)
            except StopIteration as stop:
                results[i] = stop.value
        live = still
    return results


def _rwkv_scan_kernel(lw0, kd0, b0, kkf, vf, rf, lw1, kd1, b1, kkb, vb, rb, s0_ref,
                      yf_o, yb_o, sfin_o, st):
    ci = pl.program_id(1)
    c = lw0.shape[2]
    npair = lw0.shape[1]

    @pl.when(ci == 0)
    def _():
        st[...] = s0_ref[:, 0]

    row = lax.broadcasted_iota(jnp.int32, (c, c), 0)
    colm = lax.broadcasted_iota(jnp.int32, (c, c), 1)
    tri_f = (colm <= row).astype(F32)
    tri_b = (colm >= row).astype(F32)
    rows2 = lax.broadcasted_iota(jnp.int32, (2 * c, 2 * c), 0)
    cols2 = lax.broadcasted_iota(jnp.int32, (2 * c, 2 * c), 1)
    r2 = rows2 & (c - 1)
    c2 = cols2 & (c - 1)
    diag = (rows2 < c) == (cols2 < c)
    m0 = lax.broadcasted_iota(jnp.int32, (1, LANES), 1) < HEAD_DIM

    fwd_refs = (lw0, kd0, b0, kkf, vf, rf)
    bwd_refs = (lw1, kd1, b1, kkb, vb, rb)
    for p0 in range(0, npair, SCAN_LOCKSTEP_PAIRS):
        gens, outs = [], []
        for p in range(p0, min(p0 + SCAN_LOCKSTEP_PAIRS, npair)):
            gens.append(_chunk_step([x.at[0, p] for x in fwd_refs], st.at[0, p], rev=False,
                                    tri=tri_f, strict=c2 < r2, incl=c2 <= r2, diag=diag, m0=m0))
            outs.append((yf_o, 0, p))
            gens.append(_chunk_step([x.at[0, p] for x in bwd_refs], st.at[1, p], rev=True,
                                    tri=tri_b, strict=c2 > r2, incl=c2 >= r2, diag=diag, m0=m0))
            outs.append((yb_o, 1, p))
        for (y_o, d, p), (y, s_new) in zip(outs, _lockstep(gens)):
            y_o[0, p] = y.astype(y_o.dtype)
            st[d, p] = s_new

    @pl.when(ci == pl.num_programs(1) - 1)
    def _():
        sfin_o[:, 0] = st[...]


def _rwkv_scan(prep, s0):
    lw0, lw1, kd0, kd1, b0, b1, kk, v, r = prep
    nb, npair, l, _ = lw0.shape
    c = SCAN_CHUNK
    assert c == HEAD_DIM and 2 * HEAD_DIM == LANES and l % c == 0
    n = l // c
    fwd = pl.BlockSpec((1, npair, c, LANES), lambda b, i: (b, 0, i, 0))
    bwd = pl.BlockSpec((1, npair, c, LANES), lambda b, i: (b, 0, n - 1 - i, 0))
    st_spec = pl.BlockSpec((2, 1, npair, LANES, LANES), lambda b, i: (0, b, 0, 0, 0))
    y_shape = jax.ShapeDtypeStruct((nb, npair, l, LANES), BF16)
    return pl.pallas_call(
        _rwkv_scan_kernel,
        grid=(nb, n),
        in_specs=[fwd] * 6 + [bwd] * 6 + [st_spec],
        out_specs=[fwd, bwd, st_spec],
        out_shape=[y_shape, y_shape, jax.ShapeDtypeStruct(s0.shape, F32)],
        scratch_shapes=[pltpu.VMEM((2, npair, LANES, LANES), F32)],
        compiler_params=_params("arbitrary", "arbitrary"),
        name="rwkv_scan",
    )(lw0, kd0, b0, kk, v, r, lw1, kd1, b1, kk, v, r, s0)


def _rwkv_post_kernel(yf_ref, yb_ref, bv_ref, g_ref, lng_ref, lnb_ref, bd_ref, o_ref):
    npair = yf_ref.shape[1]
    y = jnp.concatenate([yf_ref[0, p].astype(F32) + yb_ref[0, p].astype(F32)
                         for p in range(npair)], axis=-1)
    bd = bd_ref[...]
    inv = 1.0 / HEAD_DIM
    m = _segsum(y, bd) * inv
    dlt = y - m
    var = _segsum(dlt * dlt, bd) * inv
    yn = dlt * lax.rsqrt(var + RWKV_GN_EPS) * lng_ref[...] + lnb_ref[...]
    o_ref[0] = ((yn + bv_ref[0].astype(F32)) * g_ref[0].astype(F32)).astype(o_ref.dtype)


def _rwkv_post(yf, yb, bv, g, ln_g, ln_b, *, t_rows):
    nb, npair, l, _ = yf.shape
    w = npair * LANES
    pair_spec = pl.BlockSpec((1, npair, t_rows, LANES), lambda b, i: (b, 0, i, 0))
    wide_spec = pl.BlockSpec((1, t_rows, w), lambda b, i: (b, i, 0))
    const2 = lambda b, i: (0, 0)
    return pl.pallas_call(
        _rwkv_post_kernel,
        grid=(nb, l // t_rows),
        in_specs=[pair_spec, pair_spec, wide_spec, wide_spec,
                  pl.BlockSpec((1, w), const2), pl.BlockSpec((1, w), const2),
                  pl.BlockSpec((LANES, LANES), const2)],
        out_specs=wide_spec,
        out_shape=jax.ShapeDtypeStruct((nb, l, w), BF16),
        compiler_params=_params("arbitrary", "arbitrary"),
        name="rwkv_post",
    )(yf, yb, bv, g, ln_g.reshape(1, w), ln_b.reshape(1, w), _head_blockdiag(LANES))


def _merge_kernel(h_ref, hn_ref, mod_ref, yp_ref, yr_ref, yg_ref, wzg_ref, pp_ref, pr_ref, pg_ref,
                  wo_ref, o_ref):
    @pl.when(pl.program_id(2) == 0)
    def _():
        o_ref[0] = h_ref[0]

    hn = hn_ref[0]
    merged = None
    for br, (y_ref, p_ref) in enumerate(((yp_ref, pp_ref), (yr_ref, pr_ref), (yg_ref, pg_ref))):
        gate = _sigmoid(jnp.dot(hn, wzg_ref[br], preferred_element_type=F32))
        term = gate * jnp.dot(y_ref[0], p_ref[...], preferred_element_type=F32)
        merged = term if merged is None else merged + term
    o_ref[0] += mod_ref[0, 5:6, :] * jnp.dot(merged.astype(BF16), wo_ref[...],
                                             preferred_element_type=F32)


def _merge(h, hn, mod, yp, yr, yg, wzg, pp, pr, pg, wo, *, li, tm, tn):
    nb, l, d = h.shape
    per_batch = mod.shape[0] > 1
    mod_map = (lambda b, i, n: (b, 0, 0)) if per_batch else (lambda b, i, n: (0, 0, 0))
    row = lambda width: pl.BlockSpec((1, tm, width), lambda b, i, n: (b, i, 0))
    col = lambda a: pl.BlockSpec((None, a.shape[1], tn), lambda b, i, n: (li, 0, n))
    return pl.pallas_call(
        _merge_kernel,
        grid=(nb, l // tm, d // tn),
        in_specs=[row(d), row(d), pl.BlockSpec((1, N_MOD, d), mod_map),
                  row(yp.shape[2]), row(yr.shape[2]), row(yg.shape[2]),
                  pl.BlockSpec((wzg.shape[0], d, tn), lambda b, i, n: (0, 0, n)),
                  col(pp), col(pr), col(pg),
                  pl.BlockSpec((None, tn, d), lambda b, i, n: (li, n, 0))],
        out_specs=row(d),
        out_shape=jax.ShapeDtypeStruct((nb, l, d), F32),
        compiler_params=_params("arbitrary", "arbitrary", "arbitrary"),
        name="merge",
    )(h, hn, mod, yp, yr, yg, wzg, pp, pr, pg, wo)


def _pick(n, pref):
    t = min(pref, n)
    while t > LANES and (n % t or t % LANES):
        t -= LANES
    return t if n % t == 0 else n


def kernel(x, c, ctx, c_ctx, ada_w, ada_b, norm_g, ffn_w_gate, ffn_w_up, ffn_w_down, w_in,
           pool_w, pool_scale, rwkv_mu, rwkv_w0, rwkv_w_up, rwkv_a0, rwkv_a_up, rwkv_g_up,
           rwkv_k_k, rwkv_k_a, rwkv_r_k, rwkv_ln_g, rwkv_ln_b, gmlp_ln_g, gmlp_ln_b, gmlp_ws,
           gmlp_bs, proj_pool, proj_rwkv, proj_gmlp, w_out, final_norm):
    depth = ada_w.shape[0]
    nb, l, d = x.shape
    lc = ctx.shape[1]
    width = rwkv_w0.shape[-1]
    pool_c = pool_scale.shape[-1]
    gmlp_c = gmlp_ln_g.shape[-1]
    rwkv_cols = rwkv_mu.shape[-1]
    off_rwkv = pool_c
    off_gmlp = off_rwkv + rwkv_cols
    off_gate = off_gmlp + 2 * gmlp_c
    npair = width // LANES

    cond8 = jnp.zeros((8, d), F32).at[:nb].set(c).at[nb].set(c_ctx)
    mod_all = _adaln(cond8, ada_w, ada_b).reshape(depth, 8, N_MOD, d)

    zero_state = jnp.zeros((2, nb, npair, LANES, LANES), F32)
    wg, wu, wd = (a.astype(BF16) for a in (ffn_w_gate, ffn_w_up, ffn_w_down))
    pp, pr, pg, wo = (a.astype(BF16) for a in (proj_pool, proj_rwkv, proj_gmlp, w_out))
    win_all = w_in.astype(BF16)
    tf = _pick(wg.shape[-1], 512)
    assert (2 * gmlp_c) % pool_c == 0
    hc = ctx.reshape(1, nb * lc, d)
    for li in range(depth):
        last = li == depth - 1
        mod_x = mod_all[li, :nb]
        mod_c = mod_all[li, nb:nb + 1]
        win = win_all[li]
        w_zb = win[:, off_rwkv:off_gmlp]
        w_uva = jnp.concatenate([win[:, off_gmlp:off_gate], win[:, :off_rwkv]], axis=1)
        w_zg = win[:, off_gate:].reshape(d, -1, d).transpose(1, 0, 2)
        lp = {'mu': rwkv_mu[li], 'w0': rwkv_w0[li], 'w_up': rwkv_w_up[li], 'a0': rwkv_a0[li],
              'a_up': rwkv_a_up[li], 'g_up': rwkv_g_up[li], 'k_k': rwkv_k_k[li],
              'k_a': rwkv_k_a[li], 'r_k': rwkv_r_k[li]}

        def mixer(h, hn, mod, nseq, seq, grid_mode, s0, need_out):
            flat = h.shape[:2]
            tm = _pick(flat[1], 1024)
            zb = _proj(hn, w_zb, tm=tm, tn=_pick(rwkv_cols, 1152))
            prep = _rwkv_prep(zb.reshape(nseq, seq, -1), lp, t_rows=_pick(seq, 256),
                              grid_mode=grid_mode)
            yf, yb, s_fin = _rwkv_scan(prep[:9], s0)
            if not need_out:
                return None, s_fin
            y_rwkv = _rwkv_post(yf, yb, prep[10], prep[9], rwkv_ln_g[li], rwkv_ln_b[li],
                                t_rows=_pick(seq, 256))
            z_uva = _proj(hn, w_uva, tm=tm,
                          tn=_pick(w_uva.shape[1], 768)).reshape(nseq, seq, -1)
            y_pool = _pool(z_uva, pool_w[li], pool_scale[li], col_block=2 * gmlp_c // pool_c,
                           t_rows=_pick(seq, 512))
            y_gmlp = _gmlp(z_uva, gmlp_ln_g[li], gmlp_ln_b[li], gmlp_ws[li], gmlp_bs[li],
                           t_rows=_pick(seq, 512))
            ys = [y.reshape(flat + (-1,)) for y in (y_pool, y_rwkv, y_gmlp)]
            out = _merge(h, hn, mod, *ys, w_zg, pp, pr, pg, wo,
                         li=li, tm=_pick(flat[1], 512), tn=_pick(d, 512))
            return out, s_fin

        def ffn(h, mod, k, tail, tail_g):
            return _ffn(h, mod, norm_g[li, 2 * k], wg, wu, wd, tail_g, sel=(li, k),
                        base=6 * k, tail=tail, tm=_pick(h.shape[1], 512), tf=tf)

        hc, hnc = ffn(hc, mod_c, 0, "mixer_norm", norm_g[li, 1])
        hc_new, s_ctx = mixer(hc, hnc, mod_c, nb, lc, False, zero_state, not last)
        if not last:
            hc = ffn(hc_new, mod_c, 1, None, final_norm)

        x, hn = ffn(x, mod_x, 0, "mixer_norm", norm_g[li, 1])
        x, _ = mixer(x, hn, mod_x, nb, l, True, s_ctx, True)
        x = ffn(x, mod_x, 1, "final" if last else None, final_norm)
    return x
```

```python
import functools
import math

import jax
import jax.numpy as jnp
from jax import lax
from jax.experimental import pallas as pl
from jax.experimental.pallas import tpu as pltpu

F32 = jnp.float32
BF16 = jnp.bfloat16
HIGHEST = lax.Precision.HIGHEST

NORM_EPS = 1e-6
LN_EPS = 1e-5
RWKV_GN_EPS = 64e-5
GRID_W = 64
N_MOD = 9
POOL_WINDOWS = (2, 4, 8, 16)
HEAD_DIM = 64
GMLP_CHUNK = 128
LANES = 128
SCAN_CHUNK = 64
VMEM_LIMIT = 56 * 1024 * 1024


def _params(*sem):
    return pltpu.CompilerParams(dimension_semantics=sem, vmem_limit_bytes=VMEM_LIMIT)


def _sigmoid(x):
    return 0.5 * jnp.tanh(0.5 * x) + 0.5


def _mm(a, b):
    return jnp.dot(a.astype(BF16), b.astype(BF16), preferred_element_type=F32)


def _mm_hi(a, b):
    return jnp.dot(a, b, precision=HIGHEST, preferred_element_type=F32)


def _mm_nt_hi(a, b):
    return lax.dot_general(a, b, (((1,), (1,)), ((), ())), precision=HIGHEST,
                           preferred_element_type=F32)


def _mm_tn_hi(a, b):
    return lax.dot_general(a, b, (((0,), (0,)), ((), ())), precision=HIGHEST,
                           preferred_element_type=F32)


def _rms(x, g):
    return x * lax.rsqrt(jnp.mean(x * x, axis=-1, keepdims=True) + NORM_EPS) * g


def _adaln_kernel(c_ref, w_ref, b_ref, o_ref):
    c = c_ref[...]
    o_ref[0] = _mm_hi(c * _sigmoid(c), w_ref[0]) + b_ref[0]


def _adaln(cond8, ada_w, ada_b):
    depth, d, n = ada_w.shape
    tn = 1024
    return pl.pallas_call(
        _adaln_kernel,
        grid=(depth, n // tn),
        in_specs=[pl.BlockSpec((8, d), lambda l, j: (0, 0)),
                  pl.BlockSpec((1, d, tn), lambda l, j: (l, 0, j)),
                  pl.BlockSpec((1, 1, tn), lambda l, j: (l, 0, j))],
        out_specs=pl.BlockSpec((1, 8, tn), lambda l, j: (l, 0, j)),
        out_shape=jax.ShapeDtypeStruct((depth, 8, n), F32),
        compiler_params=_params("arbitrary", "arbitrary"),
        name="adaln",
    )(cond8, ada_w, ada_b.reshape(depth, 1, n))


FFN_WEIGHT_SLOTS = 2


def _ffn_kernel(h_ref, mod_ref, g_ref, wg_hbm, wu_hbm, wd_hbm, tail_ref, *rest,
                base, tail, sel, tf):
    if tail == "mixer_norm":
        o_ref, hn_o_ref, hn_ref, wg_buf, wu_buf, wd_buf, sem = rest
    else:
        o_ref, hn_ref, wg_buf, wu_buf, wd_buf, sem = rest
    li, k = sel
    nf = wd_hbm.shape[2] // tf
    step = pl.program_id(0) * pl.num_programs(1) + pl.program_id(1)
    nsteps = pl.num_programs(0) * pl.num_programs(1)
    first_tile = step * nf

    def copies(j, slot):
        cols = pl.ds(j * tf, tf)
        return (pltpu.make_async_copy(wg_hbm.at[li, k, :, cols], wg_buf.at[slot], sem.at[slot, 0]),
                pltpu.make_async_copy(wu_hbm.at[li, k, :, cols], wu_buf.at[slot], sem.at[slot, 1]),
                pltpu.make_async_copy(wd_hbm.at[li, k, cols, :], wd_buf.at[slot], sem.at[slot, 2]))

    @pl.when(step == 0)
    def _():
        for cp in copies(0, 0):
            cp.start()

    hn = _rms(h_ref[0], g_ref[...]) * (1.0 + mod_ref[0, base + 1:base + 2, :]) \
        + mod_ref[0, base:base + 1, :]
    hn_ref[...] = hn.astype(BF16)
    o_ref[0] = h_ref[0]
    half_gate = 0.5 * mod_ref[0, base + 2:base + 3, :]

    for j in range(nf):
        slot = (first_tile + j) % FFN_WEIGHT_SLOTS
        nxt = (first_tile + j + 1) % FFN_WEIGHT_SLOTS
        if j + 1 < nf:
            for cp in copies(j + 1, nxt):
                cp.start()
        else:
            @pl.when(step + 1 < nsteps)
            def _():
                for cp in copies(0, nxt):
                    cp.start()
        for cp in copies(j, slot):
            cp.wait()
        hn = hn_ref[...]
        gate = jnp.dot(hn, wg_buf[slot], preferred_element_type=F32)
        up = jnp.dot(hn, wu_buf[slot], preferred_element_type=F32)
        act = (gate * _sigmoid(gate) * up).astype(BF16)
        o_ref[0] += half_gate * jnp.dot(act, wd_buf[slot], preferred_element_type=F32)

    if tail is not None:
        normed = _rms(o_ref[0], tail_ref[...])
        if tail == "final":
            o_ref[0] = normed
        else:
            hn_o_ref[0] = (normed * (1.0 + mod_ref[0, 4:5, :])
                           + mod_ref[0, 3:4, :]).astype(BF16)


def _ffn(h, mod, g, wg, wu, wd, tail_g, *, sel, base, tail, tm, tf):
    nb, l, d = h.shape
    assert wg.shape[-1] % tf == 0 and l % tm == 0
    per_batch = mod.shape[0] > 1
    mod_map = (lambda b, i: (b, 0, 0)) if per_batch else (lambda b, i: (0, 0, 0))
    row = pl.BlockSpec((1, tm, d), lambda b, i: (b, i, 0))
    hbm = pl.BlockSpec(memory_space=pl.ANY)
    out_specs, out_shape = row, jax.ShapeDtypeStruct((nb, l, d), F32)
    if tail == "mixer_norm":
        out_specs, out_shape = [row, row], [out_shape, jax.ShapeDtypeStruct((nb, l, d), BF16)]
    slots = FFN_WEIGHT_SLOTS
    return pl.pallas_call(
        functools.partial(_ffn_kernel, base=base, tail=tail, sel=sel, tf=tf),
        grid=(nb, l // tm),
        in_specs=[row,
                  pl.BlockSpec((1, N_MOD, d), mod_map),
                  pl.BlockSpec((1, d), lambda b, i: (0, 0)),
                  hbm, hbm, hbm,
                  pl.BlockSpec((1, d), lambda b, i: (0, 0))],
        out_specs=out_specs,
        out_shape=out_shape,
        scratch_shapes=[pltpu.VMEM((tm, d), BF16),
                        pltpu.VMEM((slots, d, tf), BF16),
                        pltpu.VMEM((slots, d, tf), BF16),
                        pltpu.VMEM((slots, tf, d), BF16),
                        pltpu.SemaphoreType.DMA((slots, 3))],
        compiler_params=_params("arbitrary", "arbitrary"),
        name="ffn",
    )(h, mod, g.reshape(1, d), wg, wu, wd, tail_g.reshape(1, d))


def _proj_kernel(hn_ref, w_ref, o_ref):
    o_ref[0] = jnp.dot(hn_ref[0], w_ref[...], preferred_element_type=F32)


def _proj(hn, w, *, tm, tn):
    nb, l, d = hn.shape
    n = w.shape[1]
    return pl.pallas_call(
        _proj_kernel,
        grid=(nb, l // tm, n // tn),
        in_specs=[pl.BlockSpec((1, tm, d), lambda b, i, j: (b, i, 0)),
                  pl.BlockSpec((d, tn), lambda b, i, j: (0, j))],
        out_specs=pl.BlockSpec((1, tm, tn), lambda b, i, j: (b, i, j)),
        out_shape=jax.ShapeDtypeStruct((nb, l, n), F32),
        compiler_params=_params("arbitrary", "arbitrary", "arbitrary"),
        name="proj",
    )(hn, w)


POOL_HALO = 8


def _pool_kernel(prev_ref, x_ref, next_ref, pw_ref, ps_ref, o_ref, scr, *, t_rows, seq):
    i = pl.program_id(1)
    last = pl.num_programs(1) - 1
    hal = POOL_HALO
    scr[0:hal] = jnp.where(i > 0, prev_ref[0], 0.0)
    scr[hal:hal + t_rows] = x_ref[0]
    scr[hal + t_rows:2 * hal + t_rows] = jnp.where(i < last, next_ref[0], 0.0)
    t = i * t_rows + lax.broadcasted_iota(jnp.int32, (t_rows, LANES), 0)
    for gi, w in enumerate(POOL_WINDOWS):
        c0 = gi * LANES
        s = scr[hal - w // 2:hal - w // 2 + t_rows, c0:c0 + LANES]
        for dlt in range(-(w // 2) + 1, w // 2):
            s = s + scr[hal + dlt:hal + dlt + t_rows, c0:c0 + LANES]
        lo = jnp.clip(t - w // 2, 0, seq)
        hi = jnp.clip(t - w // 2 + w, 0, seq)
        cnt = (hi - lo).astype(F32)
        p = s / cnt - scr[hal:hal + t_rows, c0:c0 + LANES]
        y = _mm(p, pw_ref[gi]) * ps_ref[:, c0:c0 + LANES]
        o_ref[0, :, c0:c0 + LANES] = y.astype(o_ref.dtype)


def _pool(z, pool_w, pool_scale, *, col_block, t_rows):
    nb, l, _ = z.shape
    c = pool_scale.shape[-1]
    hal = POOL_HALO
    r = t_rows // hal
    nh = l // hal
    cb = col_block
    return pl.pallas_call(
        functools.partial(_pool_kernel, t_rows=t_rows, seq=l),
        grid=(nb, l // t_rows),
        in_specs=[pl.BlockSpec((1, hal, c), lambda b, i: (b, jnp.maximum(i * r - 1, 0), cb)),
                  pl.BlockSpec((1, t_rows, c), lambda b, i: (b, i, cb)),
                  pl.BlockSpec((1, hal, c), lambda b, i: (b, jnp.minimum((i + 1) * r, nh - 1), cb)),
                  pl.BlockSpec(pool_w.shape, lambda b, i: (0, 0, 0)),
                  pl.BlockSpec((1, c), lambda b, i: (0, 0))],
        out_specs=pl.BlockSpec((1, t_rows, c), lambda b, i: (b, i, 0)),
        out_shape=jax.ShapeDtypeStruct((nb, l, c), BF16),
        scratch_shapes=[pltpu.VMEM((t_rows + 2 * hal, c), F32)],
        compiler_params=_params("arbitrary", "arbitrary"),
        name="pool",
    )(z, z, z, pool_w.astype(BF16), pool_scale.reshape(1, c))


def _gelu(x):
    return x * (0.5 * (1.0 + jnp.tanh(math.sqrt(2.0 / math.pi) * (x + 0.044715 * (x * x * x)))))


def _gmlp_kernel(z_ref, lng_ref, lnb_ref, ws_ref, bsb_ref, o_ref, *, t_rows):
    width = o_ref.shape[2]
    groups = ws_ref.shape[0]
    gd = width // groups
    for ci in range(t_rows // GMLP_CHUNK):
        rows = slice(ci * GMLP_CHUNK, (ci + 1) * GMLP_CHUNK)
        u = _gelu(z_ref[0, rows, 0:width])
        v = _gelu(z_ref[0, rows, width:2 * width])
        mu = jnp.mean(v, axis=-1, keepdims=True)
        var = jnp.mean(jnp.square(v - mu), axis=-1, keepdims=True)
        vn = ((v - mu) * lax.rsqrt(var + LN_EPS)) * lng_ref[...] + lnb_ref[...]
        for g in range(groups):
            cols = slice(g * gd, (g + 1) * gd)
            s = _mm(ws_ref[g], vn[:, cols]) + bsb_ref[g]
            o_ref[0, rows, cols] = (u[:, cols] * s).astype(o_ref.dtype)


def _gmlp(zuv, ln_g, ln_b, ws, bs, *, t_rows):
    nb, l, _ = zuv.shape
    c = ln_g.shape[-1]
    c2 = 2 * c
    groups = ws.shape[0]
    bsb = jnp.broadcast_to(bs[:, :, None], (groups, GMLP_CHUNK, c // groups))
    return pl.pallas_call(
        functools.partial(_gmlp_kernel, t_rows=t_rows),
        grid=(nb, l // t_rows),
        in_specs=[pl.BlockSpec((1, t_rows, c2), lambda b, i: (b, i, 0)),
                  pl.BlockSpec((1, c), lambda b, i: (0, 0)),
                  pl.BlockSpec((1, c), lambda b, i: (0, 0)),
                  pl.BlockSpec(ws.shape, lambda b, i: (0, 0, 0)),
                  pl.BlockSpec(bsb.shape, lambda b, i: (0, 0, 0))],
        out_specs=pl.BlockSpec((1, t_rows, c), lambda b, i: (b, i, 0)),
        out_shape=jax.ShapeDtypeStruct((nb, l, c), BF16),
        compiler_params=_params("arbitrary", "arbitrary"),
        name="gmlp",
    )(zuv, ln_g.reshape(1, c), ln_b.reshape(1, c), ws.astype(BF16), bsb)


def _segsum(x, bd):
    hi = x.astype(BF16)
    lo = (x - hi.astype(F32)).astype(BF16)
    out = []
    for p in range(x.shape[-1] // LANES):
        cols = slice(p * LANES, (p + 1) * LANES)
        out.append(jnp.dot(hi[:, cols], bd, preferred_element_type=F32)
                   + jnp.dot(lo[:, cols], bd, preferred_element_type=F32))
    return jnp.concatenate(out, axis=-1)


def _rwkv_prep_kernel(prev_ref, x_ref, next_ref, mu_ref, w0_ref, wup_ref, a0_ref, aup_ref,
                      gup_ref, kk_ref, ka_ref, rk_ref, bd_ref,
                      lw0_o, lw1_o, kd0_o, kd1_o, b0_o, b1_o, kk_o, v_o, r_o, g_o, bv_o,
                      scr, *, t_rows, grid_mode, width):
    i = pl.program_id(1)
    last = pl.num_programs(1) - 1
    hal = GRID_W
    scr[0:hal] = jnp.where(i > 0, prev_ref[0], 0.0)
    scr[hal:hal + t_rows] = x_ref[0]
    scr[hal + t_rows:2 * hal + t_rows] = jnp.where(i < last, next_ref[0], 0.0)
    cols = scr.shape[1]
    if grid_mode:
        q = cols // 4
        bounds = (0, q, 2 * q, 3 * q, cols)
        offs = (-1, 1, -GRID_W, GRID_W)
    else:
        bounds = (0, cols // 2, cols)
        offs = (-1, 1)

    def zs_cols(c0, c1):
        n = c1 - c0
        x = scr[hal:hal + t_rows, c0:c1]
        ch = c0 + lax.broadcasted_iota(jnp.int32, (t_rows, n), 1)
        col = lax.broadcasted_iota(jnp.int32, (t_rows, n), 0) & (GRID_W - 1)
        shifted = None
        for qi, off in enumerate(offs):
            lo, hi = bounds[qi], bounds[qi + 1]
            if hi <= c0 or lo >= c1:
                continue
            src = scr[hal + off:hal + off + t_rows, c0:c1]
            if grid_mode and off == -1:
                src = jnp.where(col == 0, 0.0, src)
            if grid_mode and off == 1:
                src = jnp.where(col == GRID_W - 1, 0.0, src)
            shifted = src if shifted is None else jnp.where(ch >= lo, src, shifted)
        return x + (shifted - x) * mu_ref[:, c0:c1]

    w = width
    r = zs_cols(0, w)
    k = zs_cols(w, 2 * w)
    v = zs_cols(2 * w, 3 * w)
    rest = zs_cols(3 * w, cols)
    wd = jnp.tanh(rest[:, 0:LANES])
    ad = rest[:, LANES:2 * LANES]
    gd = _sigmoid(rest[:, 2 * LANES:3 * LANES])

    bd = bd_ref[...]
    kk0 = k * kk_ref[...]
    ss = _segsum(kk0 * kk0, bd)
    kk = kk0 / jnp.maximum(jnp.sqrt(ss), 1e-12)
    npair = w // LANES

    def put(o_ref, val):
        for p in range(npair):
            o_ref[0, p] = val[:, p * LANES:(p + 1) * LANES].astype(o_ref.dtype)

    put(kk_o, kk)
    put(v_o, v)
    put(r_o, r)
    kd_sum = None
    for d, (lw_o, kd_o, b_o) in enumerate(((lw0_o, kd0_o, b0_o), (lw1_o, kd1_o, b1_o))):
        w_pre = w0_ref[d] + _mm(wd, wup_ref[d])
        put(lw_o, -math.exp(-0.5) * _sigmoid(w_pre))
        a = _sigmoid(a0_ref[d] + _mm(ad, aup_ref[d]))
        kd = k * (1.0 + (a - 1.0) * ka_ref[...])
        put(kd_o, kd)
        put(b_o, kk * a)
        kd_sum = kd if kd_sum is None else kd_sum + kd
    g_o[0] = _mm(gd, gup_ref[...]).astype(g_o.dtype)
    bv_o[0] = (_segsum(r * kd_sum * rk_ref[...], bd) * v).astype(bv_o.dtype)


def _head_blockdiag(width):
    idx = jnp.arange(width) // HEAD_DIM
    return (idx[:, None] == idx[None, :]).astype(BF16)


def _rwkv_prep(zb, lp, *, t_rows, grid_mode):
    nb, l, cols = zb.shape
    w = lp['w0'].shape[-1]
    npair = w // LANES
    hal = GRID_W
    r = t_rows // hal
    nh = l // hal
    lora = lp['w_up'].shape[1]

    def pad_dir(up):
        z = jnp.zeros_like(up[0])
        return jnp.stack([jnp.concatenate([up[0], z], 0), jnp.concatenate([z, up[1]], 0)], 0)

    const2 = lambda b, i: (0, 0)
    const3 = lambda b, i: (0, 0, 0)
    pair_spec = pl.BlockSpec((1, npair, t_rows, LANES), lambda b, i: (b, 0, i, 0))
    wide_spec = pl.BlockSpec((1, t_rows, w), lambda b, i: (b, i, 0))
    pair_shape = lambda dt: jax.ShapeDtypeStruct((nb, npair, l, LANES), dt)
    wide_shape = jax.ShapeDtypeStruct((nb, l, w), BF16)
    return pl.pallas_call(
        functools.partial(_rwkv_prep_kernel, t_rows=t_rows, grid_mode=grid_mode, width=w),
        grid=(nb, l // t_rows),
        in_specs=[pl.BlockSpec((1, hal, cols), lambda b, i: (b, jnp.maximum(i * r - 1, 0), 0)),
                  pl.BlockSpec((1, t_rows, cols), lambda b, i: (b, i, 0)),
                  pl.BlockSpec((1, hal, cols), lambda b, i: (b, jnp.minimum((i + 1) * r, nh - 1), 0)),
                  pl.BlockSpec((1, cols), const2),
                  pl.BlockSpec((2, 1, w), const3),
                  pl.BlockSpec((2, 2 * lora, w), const3),
                  pl.BlockSpec((2, 1, w), const3),
                  pl.BlockSpec((2, 2 * lora, w), const3),
                  pl.BlockSpec(lp['g_up'].shape, const2),
                  pl.BlockSpec((1, w), const2),
                  pl.BlockSpec((1, w), const2),
                  pl.BlockSpec((1, w), const2),
                  pl.BlockSpec((LANES, LANES), const2)],
        out_specs=[pair_spec] * 9 + [wide_spec] * 2,
        out_shape=[pair_shape(F32)] * 2 + [pair_shape(BF16)] * 7 + [wide_shape] * 2,
        scratch_shapes=[pltpu.VMEM((t_rows + 2 * hal, cols), F32)],
        compiler_params=_params("arbitrary", "arbitrary"),
        name="rwkv_prep",
    )(zb, zb, zb, lp['mu'].reshape(1, cols), lp['w0'].reshape(2, 1, w),
      pad_dir(lp['w_up']).astype(BF16), lp['a0'].reshape(2, 1, w),
      pad_dir(lp['a_up']).astype(BF16), lp['g_up'].astype(BF16), lp['k_k'].reshape(1, w),
      lp['k_a'].reshape(1, w), lp['r_k'].reshape(1, w), _head_blockdiag(LANES))


def _stack2(x, m0):
    xs = x.astype(BF16)
    zero = jnp.zeros_like(xs)
    return jnp.concatenate([jnp.where(m0, xs, zero), jnp.where(m0, zero, xs)], axis=0)


_NN = (((1,), (0,)), ((), ()))
_NT = (((1,), (1,)), ((), ()))
_TN = (((0,), (0,)), ((), ()))

SCAN_PREC = {"cum": "rhs2", "gram": "bf16", "init": "bf16", "apply": "bf16", "out": "bf16",
             "state": "bf16"}
SCAN_LOCKSTEP_PAIRS = 8

def _split(a):
    hi = a.astype(BF16)
    return hi, (a - hi.astype(F32)).astype(BF16)


def _dg(a, b, dims, site):
    mode = SCAN_PREC[site]
    if mode == "f32":
        return lax.dot_general(a, b, dims, precision=HIGHEST, preferred_element_type=F32)
    if mode == "bf16":
        return lax.dot_general(a.astype(BF16), b.astype(BF16), dims, preferred_element_type=F32)
    b_hi, b_lo = _split(b)
    if mode == "rhs2":
        a16 = a.astype(BF16)
        return (lax.dot_general(a16, b_hi, dims, preferred_element_type=F32)
                + lax.dot_general(a16, b_lo, dims, preferred_element_type=F32))
    a_hi, a_lo = _split(a)
    return (lax.dot_general(a_hi, b_hi, dims, preferred_element_type=F32)
            + lax.dot_general(a_hi, b_lo, dims, preferred_element_type=F32)
            + lax.dot_general(a_lo, b_hi, dims, preferred_element_type=F32))


def _chunk_step(refs, s_ref, *, rev, tri, masks, m0):
    lw, kd, b, kk, v, r = (x[...].astype(F32) for x in refs)
    strict, incl, diag = masks
    s = s_ref[...]
    c = lw.shape[0]
    cum = _dg(tri, lw, _NN, "cum")
    yield
    cum_prev = cum - lw
    end = 0 if rev else c - 1
    tot = cum[end:end + 1, :]
    mid = cum[c // 2:c // 2 + 1, :]
    e_inv = jnp.exp(mid - cum)
    nkk = -kk
    left = jnp.concatenate([_stack2(nkk * jnp.exp(cum_prev - mid), m0),
                            _stack2(r * jnp.exp(cum - mid), m0)], axis=0)
    right = jnp.concatenate([_stack2(b * e_inv, m0), _stack2(kd * e_inv, m0)], axis=0)
    gram = _dg(left, right, _NT, "gram")
    yield
    c2 = 2 * c
    g16 = gram.astype(BF16)
    zero16 = jnp.zeros((c2, c2), BF16)
    a_ab = jnp.where(strict, g16[0:c2, 0:c2], zero16)
    a_ak = jnp.where(strict, g16[0:c2, c2:2 * c2], zero16)
    a_rb = jnp.where(incl, g16[c2:2 * c2, 0:c2], zero16)
    a_rk = jnp.where(incl, g16[c2:2 * c2, c2:2 * c2], zero16)
    vsw = pltpu.roll(v, HEAD_DIM, 1).astype(BF16)
    zv = jnp.zeros_like(vsw)
    vbd = jnp.concatenate([jnp.where(m0, zv, vsw), jnp.where(m0, vsw, zv)], axis=0)
    e_end = jnp.exp(tot - cum)
    x = (_dg(_stack2(nkk * jnp.exp(cum_prev), m0), s, _NT, "init")
         + _dg(a_ak, vbd, _NN, "init"))
    ybd = _dg(_stack2(r * jnp.exp(cum), m0), s, _NT, "out") + _dg(a_rk, vbd, _NN, "out")
    s_new = s * jnp.exp(tot) + _dg(vbd, _stack2(kd * e_end, m0), _TN, "state")
    steps = max(1, (c - 1).bit_length())
    pw = a_ab
    rhs = jnp.where(diag, pw, x.astype(BF16))
    for it in range(steps):
        yield
        both = _dg(pw, rhs, _NN, "apply")
        x = x + both
        if it < steps - 1:
            both16 = both.astype(BF16)
            pw = jnp.where(diag, both16, zero16)
            rhs = jnp.where(diag, both16, x.astype(BF16))
    yield
    u = jnp.where(diag, 0.0, x)
    ybd = ybd + _dg(a_rb, u, _NN, "out")
    y = pltpu.roll(ybd[0:c] + ybd[c:c2], HEAD_DIM, 1)
    s_new = s_new + _dg(u, _stack2(b * e_end, m0), _TN, "state")
    return y, s_new


def _lockstep(gens):
    results = [None] * len(gens)
    live = list(range(len(gens)))
    while live:
        still = []
        for i in live:
            try:
                next(gens[i])
                still.append(i)
            except StopIteration as stop:
                results[i] = stop.value
        live = still
    return results


def _rwkv_scan_kernel(lw0, kd0, b0, kkf, vf, rf, lw1, kd1, b1, kkb, vb, rb, s0_ref,
                      yf_o, yb_o, sfin_o, st):
    ci = pl.program_id(1)
    c = lw0.shape[2]
    npair = lw0.shape[1]

    @pl.when(ci == 0)
    def _():
        st[...] = s0_ref[:, 0]

    row = lax.broadcasted_iota(jnp.int32, (c, c), 0)
    colm = lax.broadcasted_iota(jnp.int32, (c, c), 1)
    tri_f = (colm <= row).astype(F32)
    tri_b = (colm >= row).astype(F32)
    rows2 = lax.broadcasted_iota(jnp.int32, (2 * c, 2 * c), 0)
    cols2 = lax.broadcasted_iota(jnp.int32, (2 * c, 2 * c), 1)
    r2 = rows2 & (c - 1)
    c2 = cols2 & (c - 1)
    diag = (rows2 < c) == (cols2 < c)
    masks_f = (c2 < r2, c2 <= r2, diag)
    masks_b = (c2 > r2, c2 >= r2, diag)
    m0 =lax.broadcasted_iota(jnp.int32, (1, LANES), 1) < HEAD_DIM

    fwd_refs = (lw0, kd0, b0, kkf, vf, rf)
    bwd_refs = (lw1, kd1, b1, kkb, vb, rb)
    for p0 in range(0, npair, SCAN_LOCKSTEP_PAIRS):
        gens, outs = [], []
        for p in range(p0, min(p0 + SCAN_LOCKSTEP_PAIRS, npair)):
            gens.append(_chunk_step([x.at[0, p] for x in fwd_refs], st.at[0, p], rev=False,
                                    tri=tri_f, masks=masks_f, m0=m0))
            outs.append((yf_o, 0, p))
            gens.append(_chunk_step([x.at[0, p] for x in bwd_refs], st.at[1, p], rev=True,
                                    tri=tri_b, masks=masks_b, m0=m0))
            outs.append((yb_o, 1, p))
        for (y_o, d, p), (y, s_new) in zip(outs, _lockstep(gens)):
            y_o[0, p] = y.astype(y_o.dtype)
            st[d, p] = s_new

    @pl.when(ci == pl.num_programs(1) - 1)
    def _():
        sfin_o[:, 0] = st[...]


def _rwkv_scan(prep, s0):
    lw0, lw1, kd0, kd1, b0, b1, kk, v, r = prep
    nb, npair, l, _ = lw0.shape
    c = SCAN_CHUNK
    assert c == HEAD_DIM and 2 * HEAD_DIM == LANES and l % c == 0
    n = l // c
    fwd = pl.BlockSpec((1, npair, c, LANES), lambda b, i: (b, 0, i, 0))
    bwd = pl.BlockSpec((1, npair, c, LANES), lambda b, i: (b, 0, n - 1 - i, 0))
    st_spec = pl.BlockSpec((2, 1, npair, LANES, LANES), lambda b, i: (0, b, 0, 0, 0))
    y_shape = jax.ShapeDtypeStruct((nb, npair, l, LANES), BF16)
    return pl.pallas_call(
        _rwkv_scan_kernel,
        grid=(nb, n),
        in_specs=[fwd] * 6 + [bwd] * 6 + [st_spec],
        out_specs=[fwd, bwd, st_spec],
        out_shape=[y_shape, y_shape, jax.ShapeDtypeStruct(s0.shape, F32)],
        scratch_shapes=[pltpu.VMEM((2, npair, LANES, LANES), F32)],
        compiler_params=_params("arbitrary", "arbitrary"),
        name="rwkv_scan",
    )(lw0, kd0, b0, kk, v, r, lw1, kd1, b1, kk, v, r, s0)


def _rwkv_post_kernel(yf_ref, yb_ref, bv_ref, g_ref, lng_ref, lnb_ref, bd_ref, o_ref):
    npair = yf_ref.shape[1]
    y = jnp.concatenate([yf_ref[0, p].astype(F32) + yb_ref[0, p].astype(F32)
                         for p in range(npair)], axis=-1)
    bd = bd_ref[...]
    inv = 1.0 / HEAD_DIM
    m = _segsum(y, bd) * inv
    dlt = y - m
    var = _segsum(dlt * dlt, bd) * inv
    yn = dlt * lax.rsqrt(var + RWKV_GN_EPS) * lng_ref[...] + lnb_ref[...]
    o_ref[0] = ((yn + bv_ref[0].astype(F32)) * g_ref[0].astype(F32)).astype(o_ref.dtype)


def _rwkv_post(yf, yb, bv, g, ln_g, ln_b, *, t_rows):
    nb, npair, l, _ = yf.shape
    w = npair * LANES
    pair_spec = pl.BlockSpec((1, npair, t_rows, LANES), lambda b, i: (b, 0, i, 0))
    wide_spec = pl.BlockSpec((1, t_rows, w), lambda b, i: (b, i, 0))
    const2 = lambda b, i: (0, 0)
    return pl.pallas_call(
        _rwkv_post_kernel,
        grid=(nb, l // t_rows),
        in_specs=[pair_spec, pair_spec, wide_spec, wide_spec,
                  pl.BlockSpec((1, w), const2), pl.BlockSpec((1, w), const2),
                  pl.BlockSpec((LANES, LANES), const2)],
        out_specs=wide_spec,
        out_shape=jax.ShapeDtypeStruct((nb, l, w), BF16),
        compiler_params=_params("arbitrary", "arbitrary"),
        name="rwkv_post",
    )(yf, yb, bv, g, ln_g.reshape(1, w), ln_b.reshape(1, w), _head_blockdiag(LANES))


def _merge_kernel(h_ref, hn_ref, mod_ref, yp_ref, yr_ref, yg_ref, wzg_ref, pp_ref, pr_ref, pg_ref,
                  wo_ref, o_ref):
    @pl.when(pl.program_id(2) == 0)
    def _():
        o_ref[0] = h_ref[0]

    hn = hn_ref[0]
    merged = None
    for br, (y_ref, p_ref) in enumerate(((yp_ref, pp_ref), (yr_ref, pr_ref), (yg_ref, pg_ref))):
        gate = _sigmoid(jnp.dot(hn, wzg_ref[br], preferred_element_type=F32))
        term = gate * jnp.dot(y_ref[0], p_ref[...], preferred_element_type=F32)
        merged = term if merged is None else merged + term
    o_ref[0] += mod_ref[0, 5:6, :] * jnp.dot(merged.astype(BF16), wo_ref[...],
                                             preferred_element_type=F32)


def _merge(h, hn, mod, yp, yr, yg, wzg, pp, pr, pg, wo, *, li, tm, tn):
    nb, l, d = h.shape
    per_batch = mod.shape[0] > 1
    mod_map = (lambda b, i, n: (b, 0, 0)) if per_batch else (lambda b, i, n: (0, 0, 0))
    row = lambda width: pl.BlockSpec((1, tm, width), lambda b, i, n: (b, i, 0))
    col = lambda a: pl.BlockSpec((None, a.shape[1], tn), lambda b, i, n: (li, 0, n))
    return pl.pallas_call(
        _merge_kernel,
        grid=(nb, l // tm, d // tn),
        in_specs=[row(d), row(d), pl.BlockSpec((1, N_MOD, d), mod_map),
                  row(yp.shape[2]), row(yr.shape[2]), row(yg.shape[2]),
                  pl.BlockSpec((wzg.shape[0], d, tn), lambda b, i, n: (0, 0, n)),
                  col(pp), col(pr), col(pg),
                  pl.BlockSpec((None, tn, d), lambda b, i, n: (li, n, 0))],
        out_specs=row(d),
        out_shape=jax.ShapeDtypeStruct((nb, l, d), F32),
        compiler_params=_params("arbitrary", "arbitrary", "arbitrary"),
        name="merge",
    )(h, hn, mod, yp, yr, yg, wzg, pp, pr, pg, wo)


def _pick(n, pref):
    t = min(pref, n)
    while t > LANES and (n % t or t % LANES):
        t -= LANES
    return t if n % t == 0 else n


def kernel(x, c, ctx, c_ctx, ada_w, ada_b, norm_g, ffn_w_gate, ffn_w_up, ffn_w_down, w_in,
           pool_w, pool_scale, rwkv_mu, rwkv_w0, rwkv_w_up, rwkv_a0, rwkv_a_up, rwkv_g_up,
           rwkv_k_k, rwkv_k_a, rwkv_r_k, rwkv_ln_g, rwkv_ln_b, gmlp_ln_g, gmlp_ln_b, gmlp_ws,
           gmlp_bs, proj_pool, proj_rwkv, proj_gmlp, w_out, final_norm):
    depth = ada_w.shape[0]
    nb, l, d = x.shape
    lc = ctx.shape[1]
    width = rwkv_w0.shape[-1]
    pool_c = pool_scale.shape[-1]
    gmlp_c = gmlp_ln_g.shape[-1]
    rwkv_cols = rwkv_mu.shape[-1]
    off_rwkv = pool_c
    off_gmlp = off_rwkv + rwkv_cols
    off_gate = off_gmlp + 2 * gmlp_c
    npair = width // LANES

    cond8 = jnp.zeros((8, d), F32).at[:nb].set(c).at[nb].set(c_ctx)
    mod_all = _adaln(cond8, ada_w, ada_b).reshape(depth, 8, N_MOD, d)

    zero_state = jnp.zeros((2, nb, npair, LANES, LANES), F32)
    wg, wu, wd = (a.astype(BF16) for a in (ffn_w_gate, ffn_w_up, ffn_w_down))
    pp, pr, pg, wo = (a.astype(BF16) for a in (proj_pool, proj_rwkv, proj_gmlp, w_out))
    win_all = w_in.astype(BF16)
    tf = _pick(wg.shape[-1], 512)
    assert (2 * gmlp_c) % pool_c == 0
    hc = ctx.reshape(1, nb * lc, d)
    for li in range(depth):
        last = li == depth - 1
        mod_x = mod_all[li, :nb]
        mod_c = mod_all[li, nb:nb + 1]
        win = win_all[li]
        w_zb = win[:, off_rwkv:off_gmlp]
        w_uva = jnp.concatenate([win[:, off_gmlp:off_gate], win[:, :off_rwkv]], axis=1)
        w_zg = win[:, off_gate:].reshape(d, -1, d).transpose(1, 0, 2)
        lp = {'mu': rwkv_mu[li], 'w0': rwkv_w0[li], 'w_up': rwkv_w_up[li], 'a0': rwkv_a0[li],
              'a_up': rwkv_a_up[li], 'g_up': rwkv_g_up[li], 'k_k': rwkv_k_k[li],
              'k_a': rwkv_k_a[li], 'r_k': rwkv_r_k[li]}

        def mixer(h, hn, mod, nseq, seq, grid_mode, s0, need_out):
            flat = h.shape[:2]
            tm = _pick(flat[1], 1024)
            zb = _proj(hn, w_zb, tm=tm, tn=_pick(rwkv_cols, 1152))
            prep = _rwkv_prep(zb.reshape(nseq, seq, -1), lp, t_rows=_pick(seq, 256),
                              grid_mode=grid_mode)
            yf, yb, s_fin = _rwkv_scan(prep[:9], s0)
            if not need_out:
                return None, s_fin
            y_rwkv = _rwkv_post(yf, yb, prep[10], prep[9], rwkv_ln_g[li], rwkv_ln_b[li],
                                t_rows=_pick(seq, 256))
            z_uva = _proj(hn, w_uva, tm=tm,
                          tn=_pick(w_uva.shape[1], 768)).reshape(nseq, seq, -1)
            y_pool = _pool(z_uva, pool_w[li], pool_scale[li], col_block=2 * gmlp_c // pool_c,
                           t_rows=_pick(seq, 512))
            y_gmlp = _gmlp(z_uva, gmlp_ln_g[li], gmlp_ln_b[li], gmlp_ws[li], gmlp_bs[li],
                           t_rows=_pick(seq, 512))
            ys = [y.reshape(flat + (-1,)) for y in (y_pool, y_rwkv, y_gmlp)]
            out = _merge(h, hn, mod, *ys, w_zg, pp, pr, pg, wo,
                         li=li, tm=_pick(flat[1], 512), tn=_pick(d, 512))
            return out, s_fin

        def ffn(h, mod, k, tail, tail_g):
            return _ffn(h, mod, norm_g[li, 2 * k], wg, wu, wd, tail_g, sel=(li, k),
                        base=6 * k, tail=tail, tm=_pick(h.shape[1], 512), tf=tf)

        hc, hnc = ffn(hc, mod_c, 0, "mixer_norm", norm_g[li, 1])
        hc_new, s_ctx = mixer(hc, hnc, mod_c, nb, lc, False, zero_state, not last)
        if not last:
            hc = ffn(hc_new, mod_c, 1, None, final_norm)

        x, hn = ffn(x, mod_x, 0, "mixer_norm", norm_g[li, 1])
        x, _ = mixer(x, hn, mod_x, nb, l, True, s_ctx, True)
        x = ffn(x, mod_x, 1, "final" if last else None, final_norm)
    return x
```

```python
import functools
import math

import jax
import jax.numpy as jnp
from jax import lax
from jax.experimental import pallas as pl
from jax.experimental.pallas import tpu as pltpu

F32 = jnp.float32
BF16 = jnp.bfloat16
HIGHEST = lax.Precision.HIGHEST

NORM_EPS = 1e-6
LN_EPS = 1e-5
RWKV_GN_EPS = 64e-5
GRID_W = 64
N_MOD = 9
POOL_WINDOWS = (2, 4, 8, 16)
HEAD_DIM = 64
GMLP_CHUNK = 128
LANES = 128
SCAN_CHUNK = 64
VMEM_LIMIT = 56 * 1024 * 1024


def _params(*sem):
    return pltpu.CompilerParams(dimension_semantics=sem, vmem_limit_bytes=VMEM_LIMIT)


def _sigmoid(x):
    return 0.5 * jnp.tanh(0.5 * x) + 0.5


def _mm(a, b):
    return jnp.dot(a.astype(BF16), b.astype(BF16), preferred_element_type=F32)


def _mm_hi(a, b):
    return jnp.dot(a, b, precision=HIGHEST, preferred_element_type=F32)


def _mm_nt_hi(a, b):
    return lax.dot_general(a, b, (((1,), (1,)), ((), ())), precision=HIGHEST,
                           preferred_element_type=F32)


def _mm_tn_hi(a, b):
    return lax.dot_general(a, b, (((0,), (0,)), ((), ())), precision=HIGHEST,
                           preferred_element_type=F32)


def _rms(x, g):
    return x * lax.rsqrt(jnp.mean(x * x, axis=-1, keepdims=True) + NORM_EPS) * g


def _adaln_kernel(c_ref, w_ref, b_ref, o_ref):
    c = c_ref[...]
    o_ref[0] = _mm_hi(c * _sigmoid(c), w_ref[0]) + b_ref[0]


def _adaln(cond8, ada_w, ada_b):
    depth, d, n = ada_w.shape
    tn = 1024
    return pl.pallas_call(
        _adaln_kernel,
        grid=(depth, n // tn),
        in_specs=[pl.BlockSpec((8, d), lambda l, j: (0, 0)),
                  pl.BlockSpec((1, d, tn), lambda l, j: (l, 0, j)),
                  pl.BlockSpec((1, 1, tn), lambda l, j: (l, 0, j))],
        out_specs=pl.BlockSpec((1, 8, tn), lambda l, j: (l, 0, j)),
        out_shape=jax.ShapeDtypeStruct((depth, 8, n), F32),
        compiler_params=_params("arbitrary", "arbitrary"),
        name="adaln",
    )(cond8, ada_w, ada_b.reshape(depth, 1, n))


FFN_WEIGHT_SLOTS = 2


def _ffn_kernel(h_ref, mod_ref, g_ref, wg_hbm, wu_hbm, wd_hbm, tail_ref, *rest,
                base, tail, sel, tf):
    if tail == "mixer_norm":
        o_ref, hn_o_ref, hn_ref, wg_buf, wu_buf, wd_buf, sem = rest
    else:
        o_ref, hn_ref, wg_buf, wu_buf, wd_buf, sem = rest
    li, k = sel
    nf = wd_hbm.shape[2] // tf
    step = pl.program_id(0) * pl.num_programs(1) + pl.program_id(1)
    nsteps = pl.num_programs(0) * pl.num_programs(1)
    first_tile = step * nf

    def copies(j, slot):
        cols = pl.ds(j * tf, tf)
        return (pltpu.make_async_copy(wg_hbm.at[li, k, :, cols], wg_buf.at[slot], sem.at[slot, 0]),
                pltpu.make_async_copy(wu_hbm.at[li, k, :, cols], wu_buf.at[slot], sem.at[slot, 1]),
                pltpu.make_async_copy(wd_hbm.at[li, k, cols, :], wd_buf.at[slot], sem.at[slot, 2]))

    @pl.when(step == 0)
    def _():
        for cp in copies(0, 0):
            cp.start()

    hn = _rms(h_ref[0], g_ref[...]) * (1.0 + mod_ref[0, base + 1:base + 2, :]) \
        + mod_ref[0, base:base + 1, :]
    hn_ref[...] = hn.astype(BF16)
    o_ref[0] = h_ref[0]
    half_gate = 0.5 * mod_ref[0, base + 2:base + 3, :]

    for j in range(nf):
        slot = (first_tile + j) % FFN_WEIGHT_SLOTS
        nxt = (first_tile + j + 1) % FFN_WEIGHT_SLOTS
        if j + 1 < nf:
            for cp in copies(j + 1, nxt):
                cp.start()
        else:
            @pl.when(step + 1 < nsteps)
            def _():
                for cp in copies(0, nxt):
                    cp.start()
        for cp in copies(j, slot):
            cp.wait()
        hn = hn_ref[...]
        gate = jnp.dot(hn, wg_buf[slot], preferred_element_type=F32)
        up = jnp.dot(hn, wu_buf[slot], preferred_element_type=F32)
        act = (gate * _sigmoid(gate) * up).astype(BF16)
        o_ref[0] += half_gate * jnp.dot(act, wd_buf[slot], preferred_element_type=F32)

    if tail is not None:
        normed = _rms(o_ref[0], tail_ref[...])
        if tail == "final":
            o_ref[0] = normed
        else:
            hn_o_ref[0] = (normed * (1.0 + mod_ref[0, 4:5, :])
                           + mod_ref[0, 3:4, :]).astype(BF16)


def _ffn(h, mod, g, wg, wu, wd, tail_g, *, sel, base, tail, tm, tf):
    nb, l, d = h.shape
    assert wg.shape[-1] % tf == 0 and l % tm == 0
    per_batch = mod.shape[0] > 1
    mod_map = (lambda b, i: (b, 0, 0)) if per_batch else (lambda b, i: (0, 0, 0))
    row = pl.BlockSpec((1, tm, d), lambda b, i: (b, i, 0))
    hbm = pl.BlockSpec(memory_space=pl.ANY)
    out_specs, out_shape = row, jax.ShapeDtypeStruct((nb, l, d), F32)
    if tail == "mixer_norm":
        out_specs, out_shape = [row, row], [out_shape, jax.ShapeDtypeStruct((nb, l, d), BF16)]
    slots = FFN_WEIGHT_SLOTS
    return pl.pallas_call(
        functools.partial(_ffn_kernel, base=base, tail=tail, sel=sel, tf=tf),
        grid=(nb, l // tm),
        in_specs=[row,
                  pl.BlockSpec((1, N_MOD, d), mod_map),
                  pl.BlockSpec((1, d), lambda b, i: (0, 0)),
                  hbm, hbm, hbm,
                  pl.BlockSpec((1, d), lambda b, i: (0, 0))],
        out_specs=out_specs,
        out_shape=out_shape,
        scratch_shapes=[pltpu.VMEM((tm, d), BF16),
                        pltpu.VMEM((slots, d, tf), BF16),
                        pltpu.VMEM((slots, d, tf), BF16),
                        pltpu.VMEM((slots, tf, d), BF16),
                        pltpu.SemaphoreType.DMA((slots, 3))],
        compiler_params=_params("arbitrary", "arbitrary"),
        name="ffn",
    )(h, mod, g.reshape(1, d), wg, wu, wd, tail_g.reshape(1, d))


def _proj_kernel(hn_ref, w_ref, o_ref):
    o_ref[0] = jnp.dot(hn_ref[0], w_ref[...], preferred_element_type=F32)


def _proj(hn, w, *, tm, tn):
    nb, l, d = hn.shape
    n = w.shape[1]
    return pl.pallas_call(
        _proj_kernel,
        grid=(nb, l // tm, n // tn),
        in_specs=[pl.BlockSpec((1, tm, d), lambda b, i, j: (b, i, 0)),
                  pl.BlockSpec((d, tn), lambda b, i, j: (0, j))],
        out_specs=pl.BlockSpec((1, tm, tn), lambda b, i, j: (b, i, j)),
        out_shape=jax.ShapeDtypeStruct((nb, l, n), F32),
        compiler_params=_params("arbitrary", "arbitrary", "arbitrary"),
        name="proj",
    )(hn, w)


POOL_HALO = 8


def _pool_kernel(prev_ref, x_ref, next_ref, pw_ref, ps_ref, o_ref, scr, *, t_rows, seq):
    i = pl.program_id(1)
    last = pl.num_programs(1) - 1
    hal = POOL_HALO
    scr[0:hal] = jnp.where(i > 0, prev_ref[0], 0.0)
    scr[hal:hal + t_rows] = x_ref[0]
    scr[hal + t_rows:2 * hal + t_rows] = jnp.where(i < last, next_ref[0], 0.0)
    t = i * t_rows + lax.broadcasted_iota(jnp.int32, (t_rows, LANES), 0)
    for gi, w in enumerate(POOL_WINDOWS):
        c0 = gi * LANES
        s = scr[hal - w // 2:hal - w // 2 + t_rows, c0:c0 + LANES]
        for dlt in range(-(w // 2) + 1, w // 2):
            s = s + scr[hal + dlt:hal + dlt + t_rows, c0:c0 + LANES]
        lo = jnp.clip(t - w // 2, 0, seq)
        hi = jnp.clip(t - w // 2 + w, 0, seq)
        cnt = (hi - lo).astype(F32)
        p = s / cnt - scr[hal:hal + t_rows, c0:c0 + LANES]
        y = _mm(p, pw_ref[gi]) * ps_ref[:, c0:c0 + LANES]
        o_ref[0, :, c0:c0 + LANES] = y.astype(o_ref.dtype)


def _pool(z, pool_w, pool_scale, *, col_block, t_rows):
    nb, l, _ = z.shape
    c = pool_scale.shape[-1]
    hal = POOL_HALO
    r = t_rows // hal
    nh = l // hal
    cb = col_block
    return pl.pallas_call(
        functools.partial(_pool_kernel, t_rows=t_rows, seq=l),
        grid=(nb, l // t_rows),
        in_specs=[pl.BlockSpec((1, hal, c), lambda b, i: (b, jnp.maximum(i * r - 1, 0), cb)),
                  pl.BlockSpec((1, t_rows, c), lambda b, i: (b, i, cb)),
                  pl.BlockSpec((1, hal, c), lambda b, i: (b, jnp.minimum((i + 1) * r, nh - 1), cb)),
                  pl.BlockSpec(pool_w.shape, lambda b, i: (0, 0, 0)),
                  pl.BlockSpec((1, c), lambda b, i: (0, 0))],
        out_specs=pl.BlockSpec((1, t_rows, c), lambda b, i: (b, i, 0)),
        out_shape=jax.ShapeDtypeStruct((nb, l, c), BF16),
        scratch_shapes=[pltpu.VMEM((t_rows + 2 * hal, c), F32)],
        compiler_params=_params("arbitrary", "arbitrary"),
        name="pool",
    )(z, z, z, pool_w.astype(BF16), pool_scale.reshape(1, c))


def _gelu(x):
    return x * (0.5 * (1.0 + jnp.tanh(math.sqrt(2.0 / math.pi) * (x + 0.044715 * (x * x * x)))))


def _gmlp_kernel(z_ref, lng_ref, lnb_ref, ws_ref, bsb_ref, o_ref, *, t_rows):
    width = o_ref.shape[2]
    groups = ws_ref.shape[0]
    gd = width // groups
    for ci in range(t_rows // GMLP_CHUNK):
        rows = slice(ci * GMLP_CHUNK, (ci + 1) * GMLP_CHUNK)
        u = _gelu(z_ref[0, rows, 0:width])
        v = _gelu(z_ref[0, rows, width:2 * width])
        mu = jnp.mean(v, axis=-1, keepdims=True)
        var = jnp.mean(jnp.square(v - mu), axis=-1, keepdims=True)
        vn = ((v - mu) * lax.rsqrt(var + LN_EPS)) * lng_ref[...] + lnb_ref[...]
        for g in range(groups):
            cols = slice(g * gd, (g + 1) * gd)
            s = _mm(ws_ref[g], vn[:, cols]) + bsb_ref[g]
            o_ref[0, rows, cols] = (u[:, cols] * s).astype(o_ref.dtype)


def _gmlp(zuv, ln_g, ln_b, ws, bs, *, t_rows):
    nb, l, _ = zuv.shape
    c = ln_g.shape[-1]
    c2 = 2 * c
    groups = ws.shape[0]
    bsb = jnp.broadcast_to(bs[:, :, None], (groups, GMLP_CHUNK, c // groups))
    return pl.pallas_call(
        functools.partial(_gmlp_kernel, t_rows=t_rows),
        grid=(nb, l // t_rows),
        in_specs=[pl.BlockSpec((1, t_rows, c2), lambda b, i: (b, i, 0)),
                  pl.BlockSpec((1, c), lambda b, i: (0, 0)),
                  pl.BlockSpec((1, c), lambda b, i: (0, 0)),
                  pl.BlockSpec(ws.shape, lambda b, i: (0, 0, 0)),
                  pl.BlockSpec(bsb.shape, lambda b, i: (0, 0, 0))],
        out_specs=pl.BlockSpec((1, t_rows, c), lambda b, i: (b, i, 0)),
        out_shape=jax.ShapeDtypeStruct((nb, l, c), BF16),
        compiler_params=_params("arbitrary", "arbitrary"),
        name="gmlp",
    )(zuv, ln_g.reshape(1, c), ln_b.reshape(1, c), ws.astype(BF16), bsb)


def _segsum(x, bd):
    hi = x.astype(BF16)
    lo = (x - hi.astype(F32)).astype(BF16)
    out = []
    for p in range(x.shape[-1] // LANES):
        cols = slice(p * LANES, (p + 1) * LANES)
        out.append(jnp.dot(hi[:, cols], bd, preferred_element_type=F32)
                   + jnp.dot(lo[:, cols], bd, preferred_element_type=F32))
    return jnp.concatenate(out, axis=-1)


def _rwkv_prep_kernel(prev_ref, x_ref, next_ref, mu_ref, w0_ref, wup_ref, a0_ref, aup_ref,
                      gup_ref, kk_ref, ka_ref, rk_ref, bd_ref,
                      lw0_o, lw1_o, kd0_o, kd1_o, b0_o, b1_o, kk_o, v_o, r_o, g_o, bv_o,
                      scr, *, t_rows, grid_mode, width):
    i = pl.program_id(1)
    last = pl.num_programs(1) - 1
    hal = GRID_W
    scr[0:hal] = jnp.where(i > 0, prev_ref[0], 0.0)
    scr[hal:hal + t_rows] = x_ref[0]
    scr[hal + t_rows:2 * hal + t_rows] = jnp.where(i < last, next_ref[0], 0.0)
    cols = scr.shape[1]
    if grid_mode:
        q = cols // 4
        bounds = (0, q, 2 * q, 3 * q, cols)
        offs = (-1, 1, -GRID_W, GRID_W)
    else:
        bounds = (0, cols // 2, cols)
        offs = (-1, 1)

    def zs_cols(c0, c1):
        n = c1 - c0
        x = scr[hal:hal + t_rows, c0:c1]
        ch = c0 + lax.broadcasted_iota(jnp.int32, (t_rows, n), 1)
        col = lax.broadcasted_iota(jnp.int32, (t_rows, n), 0) & (GRID_W - 1)
        shifted = None
        for qi, off in enumerate(offs):
            lo, hi = bounds[qi], bounds[qi + 1]
            if hi <= c0 or lo >= c1:
                continue
            src = scr[hal + off:hal + off + t_rows, c0:c1]
            if grid_mode and off == -1:
                src = jnp.where(col == 0, 0.0, src)
            if grid_mode and off == 1:
                src = jnp.where(col == GRID_W - 1, 0.0, src)
            shifted = src if shifted is None else jnp.where(ch >= lo, src, shifted)
        return x + (shifted - x) * mu_ref[:, c0:c1]

    w = width
    r = zs_cols(0, w)
    k = zs_cols(w, 2 * w)
    v = zs_cols(2 * w, 3 * w)
    rest = zs_cols(3 * w, cols)
    wd = jnp.tanh(rest[:, 0:LANES])
    ad = rest[:, LANES:2 * LANES]
    gd = _sigmoid(rest[:, 2 * LANES:3 * LANES])

    bd = bd_ref[...]
    kk0 = k * kk_ref[...]
    ss = _segsum(kk0 * kk0, bd)
    kk = kk0 / jnp.maximum(jnp.sqrt(ss), 1e-12)
    npair = w // LANES

    def put(o_ref, val):
        for p in range(npair):
            o_ref[0, p] = val[:, p * LANES:(p + 1) * LANES].astype(o_ref.dtype)

    put(kk_o, kk)
    put(v_o, v)
    put(r_o, r)
    kd_sum = None
    for d, (lw_o, kd_o, b_o) in enumerate(((lw0_o, kd0_o, b0_o), (lw1_o, kd1_o, b1_o))):
        w_pre = w0_ref[d] + _mm(wd, wup_ref[d])
        put(lw_o, -math.exp(-0.5) * _sigmoid(w_pre))
        a = _sigmoid(a0_ref[d] + _mm(ad, aup_ref[d]))
        kd = k * (1.0 + (a - 1.0) * ka_ref[...])
        put(kd_o, kd)
        put(b_o, kk * a)
        kd_sum = kd if kd_sum is None else kd_sum + kd
    g_o[0] = _mm(gd, gup_ref[...]).astype(g_o.dtype)
    bv_o[0] = (_segsum(r * kd_sum * rk_ref[...], bd) * v).astype(bv_o.dtype)


def _head_blockdiag(width):
    idx = jnp.arange(width) // HEAD_DIM
    return (idx[:, None] == idx[None, :]).astype(BF16)


def _rwkv_prep(zb, lp, *, t_rows, grid_mode):
    nb, l, cols = zb.shape
    w = lp['w0'].shape[-1]
    npair = w // LANES
    hal = GRID_W
    r = t_rows // hal
    nh = l // hal
    lora = lp['w_up'].shape[1]

    def pad_dir(up):
        z = jnp.zeros_like(up[0])
        return jnp.stack([jnp.concatenate([up[0], z], 0), jnp.concatenate([z, up[1]], 0)], 0)

    const2 = lambda b, i: (0, 0)
    const3 = lambda b, i: (0, 0, 0)
    pair_spec = pl.BlockSpec((1, npair, t_rows, LANES), lambda b, i: (b, 0, i, 0))
    wide_spec = pl.BlockSpec((1, t_rows, w), lambda b, i: (b, i, 0))
    pair_shape = lambda dt: jax.ShapeDtypeStruct((nb, npair, l, LANES), dt)
    wide_shape = jax.ShapeDtypeStruct((nb, l, w), BF16)
    return pl.pallas_call(
        functools.partial(_rwkv_prep_kernel, t_rows=t_rows, grid_mode=grid_mode, width=w),
        grid=(nb, l // t_rows),
        in_specs=[pl.BlockSpec((1, hal, cols), lambda b, i: (b, jnp.maximum(i * r - 1, 0), 0)),
                  pl.BlockSpec((1, t_rows, cols), lambda b, i: (b, i, 0)),
                  pl.BlockSpec((1, hal, cols), lambda b, i: (b, jnp.minimum((i + 1) * r, nh - 1), 0)),
                  pl.BlockSpec((1, cols), const2),
                  pl.BlockSpec((2, 1, w), const3),
                  pl.BlockSpec((2, 2 * lora, w), const3),
                  pl.BlockSpec((2, 1, w), const3),
                  pl.BlockSpec((2, 2 * lora, w), const3),
                  pl.BlockSpec(lp['g_up'].shape, const2),
                  pl.BlockSpec((1, w), const2),
                  pl.BlockSpec((1, w), const2),
                  pl.BlockSpec((1, w), const2),
                  pl.BlockSpec((LANES, LANES), const2)],
        out_specs=[pair_spec] * 9 + [wide_spec] * 2,
        out_shape=[pair_shape(F32)] * 2 + [pair_shape(BF16)] * 7 + [wide_shape] * 2,
        scratch_shapes=[pltpu.VMEM((t_rows + 2 * hal, cols), F32)],
        compiler_params=_params("arbitrary", "arbitrary"),
        name="rwkv_prep",
    )(zb, zb, zb, lp['mu'].reshape(1, cols), lp['w0'].reshape(2, 1, w),
      pad_dir(lp['w_up']).astype(BF16), lp['a0'].reshape(2, 1, w),
      pad_dir(lp['a_up']).astype(BF16), lp['g_up'].astype(BF16), lp['k_k'].reshape(1, w),
      lp['k_a'].reshape(1, w), lp['r_k'].reshape(1, w), _head_blockdiag(LANES))


def _stack2(x, m0):
    xs = x.astype(BF16)
    zero = jnp.zeros_like(xs)
    return jnp.concatenate([jnp.where(m0, xs, zero), jnp.where(m0, zero, xs)], axis=0)


_NN = (((1,), (0,)), ((), ()))
_NT = (((1,), (1,)), ((), ()))
_TN = (((0,), (0,)), ((), ()))

SCAN_PREC = {"cum": "rhs2", "gram": "bf16", "init": "bf16", "apply": "bf16", "out": "bf16",
             "state": "bf16"}
SCAN_LOCKSTEP_PAIRS = 8

def _split(a):
    hi = a.astype(BF16)
    return hi, (a - hi.astype(F32)).astype(BF16)


def _dg(a, b, dims, site):
    mode = SCAN_PREC[site]
    if mode == "f32":
        return lax.dot_general(a, b, dims, precision=HIGHEST, preferred_element_type=F32)
    if mode == "bf16":
        return lax.dot_general(a.astype(BF16), b.astype(BF16), dims, preferred_element_type=F32)
    b_hi, b_lo = _split(b)
    if mode == "rhs2":
        a16 = a.astype(BF16)
        return (lax.dot_general(a16, b_hi, dims, preferred_element_type=F32)
                + lax.dot_general(a16, b_lo, dims, preferred_element_type=F32))
    a_hi, a_lo = _split(a)
    return (lax.dot_general(a_hi, b_hi, dims, preferred_element_type=F32)
            + lax.dot_general(a_hi, b_lo, dims, preferred_element_type=F32)
            + lax.dot_general(a_lo, b_hi, dims, preferred_element_type=F32))


def _chunk_local(refs, rows, *, rev, tri, masks, m0):
    lw, kd, b, kk, v, r = (x[rows].astype(F32) for x in refs)
    strict, incl, _ = masks
    c = lw.shape[0]
    cum = _dg(tri, lw, _NN, "cum")
    yield
    cum_prev = cum - lw
    end = 0 if rev else c - 1
    tot = cum[end:end + 1, :]
    mid = cum[c // 2:c // 2 + 1, :]
    e_inv = jnp.exp(mid - cum)
    nkk = -kk
    left = jnp.concatenate([_stack2(nkk * jnp.exp(cum_prev - mid), m0),
                            _stack2(r * jnp.exp(cum - mid), m0)], axis=0)
    right = jnp.concatenate([_stack2(b * e_inv, m0), _stack2(kd * e_inv, m0)], axis=0)
    gram = _dg(left, right, _NT, "gram")
    yield
    c2 = 2 * c
    g16 = gram.astype(BF16)
    zero16 = jnp.zeros((c2, c2), BF16)
    vsw = pltpu.roll(v, HEAD_DIM, 1).astype(BF16)
    zv = jnp.zeros_like(vsw)
    e_end = jnp.exp(tot - cum)
    return dict(
        a_ab=jnp.where(strict, g16[0:c2, 0:c2], zero16),
        a_ak=jnp.where(strict, g16[0:c2, c2:2 * c2], zero16),
        a_rb=jnp.where(incl, g16[c2:2 * c2, 0:c2], zero16),
        a_rk=jnp.where(incl, g16[c2:2 * c2, c2:2 * c2], zero16),
        vbd=jnp.concatenate([jnp.where(m0, zv, vsw), jnp.where(m0, vsw, zv)], axis=0),
        a0=_stack2(nkk * jnp.exp(cum_prev), m0), r0=_stack2(r * jnp.exp(cum), m0),
        bh=_stack2(b * e_end, m0), kh=_stack2(kd * e_end, m0), decay=jnp.exp(tot))


def _chunk_solve(loc, s, diag):
    vbd = loc["vbd"]
    c2 = vbd.shape[0]
    x = _dg(loc["a0"], s, _NT, "init") + _dg(loc["a_ak"], vbd, _NN, "init")
    ybd = _dg(loc["r0"], s, _NT, "out") + _dg(loc["a_rk"], vbd, _NN, "out")
    s_new = s * loc["decay"] + _dg(vbd, loc["kh"], _TN, "state")
    steps = max(1, (c2 // 2 - 1).bit_length())
    pw = loc["a_ab"]
    rhs = jnp.where(diag, pw, x.astype(BF16))
    for it in range(steps):
        yield
        both = _dg(pw, rhs, _NN, "apply")
        x = x + both
        if it < steps - 1:
            both16 = both.astype(BF16)
            pw = jnp.where(diag, both16, jnp.zeros_like(both16))
            rhs = jnp.where(diag, both16, x.astype(BF16))
    yield
    u = jnp.where(diag, 0.0, x)
    ybd = ybd + _dg(loc["a_rb"], u, _NN, "out")
    y = pltpu.roll(ybd[0:c2 // 2] + ybd[c2 // 2:c2], HEAD_DIM, 1)
    return y, s_new + _dg(u, loc["bh"], _TN, "state")


SCAN_OVERLAP_STAGES = (4, 5, 6)


def _chain(refs, s_ref, order, **kw):
    c = SCAN_CHUNK
    rows = [slice(k * c, (k + 1) * c) for k in order]
    diag = kw["masks"][2]
    loc_a = yield from _chunk_local(refs, rows[0], **kw)
    local_b, loc_b = _chunk_local(refs, rows[1], **kw), None
    solve_a, stage = _chunk_solve(loc_a, s_ref[...], diag), 0
    while True:
        try:
            next(solve_a)
        except StopIteration as stop:
            y_a, s_mid = stop.value
            break
        stage += 1
        if stage in SCAN_OVERLAP_STAGES and loc_b is None:
            try:
                next(local_b)
            except StopIteration as stop:
                loc_b = stop.value
        yield
    while loc_b is None:
        try:
            next(local_b)
        except StopIteration as stop:
            loc_b = stop.value
    yield
    y_b, s_end = yield from _chunk_solve(loc_b, s_mid, diag)
    return [(rows[0], y_a), (rows[1], y_b)], s_end


def _lockstep(gens):
    results = [None] * len(gens)
    live = list(range(len(gens)))
    while live:
        still = []
        for i in live:
            try:
                next(gens[i])
                still.append(i)
            except StopIteration as stop:
                results[i] = stop.value
        live = still
    return results


def _rwkv_scan_kernel(lw0, kd0, b0, kkf, vf, rf, lw1, kd1, b1, kkb, vb, rb, s0_ref,
                      yf_o, yb_o, sfin_o, st):
    ci = pl.program_id(1)
    c = SCAN_CHUNK
    npair = lw0.shape[1]

    @pl.when(ci == 0)
    def _():
        st[...] = s0_ref[:, 0]

    row = lax.broadcasted_iota(jnp.int32, (c, c), 0)
    colm = lax.broadcasted_iota(jnp.int32, (c, c), 1)
    tri_f = (colm <= row).astype(F32)
    tri_b = (colm >= row).astype(F32)
    rows2 = lax.broadcasted_iota(jnp.int32, (2 * c, 2 * c), 0)
    cols2 = lax.broadcasted_iota(jnp.int32, (2 * c, 2 * c), 1)
    r2 = rows2 & (c - 1)
    c2 = cols2 & (c - 1)
    diag = (rows2 < c) == (cols2 < c)
    masks_f = (c2 < r2, c2 <= r2, diag)
    masks_b = (c2 > r2, c2 >= r2, diag)
    m0 =lax.broadcasted_iota(jnp.int32, (1, LANES), 1) < HEAD_DIM

    fwd_refs = (lw0, kd0, b0, kkf, vf, rf)
    bwd_refs = (lw1, kd1, b1, kkb, vb, rb)
    for p0 in range(0, npair, SCAN_LOCKSTEP_PAIRS):
        gens, outs = [], []
        for p in range(p0, min(p0 + SCAN_LOCKSTEP_PAIRS, npair)):
            gens.append(_chain([x.at[0, p] for x in fwd_refs], st.at[0, p], (0, 1), rev=False,
                               tri=tri_f, masks=masks_f, m0=m0))
            outs.append((yf_o, 0, p))
            gens.append(_chain([x.at[0, p] for x in bwd_refs], st.at[1, p], (1, 0), rev=True,
                               tri=tri_b, masks=masks_b, m0=m0))
            outs.append((yb_o, 1, p))
        for (y_o, d, p), (ys, s_new) in zip(outs, _lockstep(gens)):
            for rows, y in ys:
                y_o[0, p, rows] = y.astype(y_o.dtype)
            st[d, p] = s_new

    @pl.when(ci == pl.num_programs(1) - 1)
    def _():
        sfin_o[:, 0] = st[...]


def _rwkv_scan(prep, s0):
    lw0, lw1, kd0, kd1, b0, b1, kk, v, r = prep
    nb, npair, l, _ = lw0.shape
    c = SCAN_CHUNK
    rows = 2 * c
    assert c == HEAD_DIM and 2 * HEAD_DIM == LANES and l % rows == 0
    n = l // rows
    fwd = pl.BlockSpec((1, npair, rows, LANES), lambda b, i: (b, 0, i, 0))
    bwd = pl.BlockSpec((1, npair, rows, LANES), lambda b, i: (b, 0, n - 1 - i, 0))
    st_spec = pl.BlockSpec((2, 1, npair, LANES, LANES), lambda b, i: (0, b, 0, 0, 0))
    y_shape = jax.ShapeDtypeStruct((nb, npair, l, LANES), BF16)
    return pl.pallas_call(
        _rwkv_scan_kernel,
        grid=(nb, n),
        in_specs=[fwd] * 6 + [bwd] * 6 + [st_spec],
        out_specs=[fwd, bwd, st_spec],
        out_shape=[y_shape, y_shape, jax.ShapeDtypeStruct(s0.shape, F32)],
        scratch_shapes=[pltpu.VMEM((2, npair, LANES, LANES), F32)],
        compiler_params=_params("arbitrary", "arbitrary"),
        name="rwkv_scan",
    )(lw0, kd0, b0, kk, v, r, lw1, kd1, b1, kk, v, r, s0)


def _rwkv_post_kernel(yf_ref, yb_ref, bv_ref, g_ref, lng_ref, lnb_ref, bd_ref, o_ref):
    npair = yf_ref.shape[1]
    y = jnp.concatenate([yf_ref[0, p].astype(F32) + yb_ref[0, p].astype(F32)
                         for p in range(npair)], axis=-1)
    bd = bd_ref[...]
    inv = 1.0 / HEAD_DIM
    m = _segsum(y, bd) * inv
    dlt = y - m
    var = _segsum(dlt * dlt, bd) * inv
    yn = dlt * lax.rsqrt(var + RWKV_GN_EPS) * lng_ref[...] + lnb_ref[...]
    o_ref[0] = ((yn + bv_ref[0].astype(F32)) * g_ref[0].astype(F32)).astype(o_ref.dtype)


def _rwkv_post(yf, yb, bv, g, ln_g, ln_b, *, t_rows):
    nb, npair, l, _ = yf.shape
    w = npair * LANES
    pair_spec = pl.BlockSpec((1, npair, t_rows, LANES), lambda b, i: (b, 0, i, 0))
    wide_spec = pl.BlockSpec((1, t_rows, w), lambda b, i: (b, i, 0))
    const2 = lambda b, i: (0, 0)
    return pl.pallas_call(
        _rwkv_post_kernel,
        grid=(nb, l // t_rows),
        in_specs=[pair_spec, pair_spec, wide_spec, wide_spec,
                  pl.BlockSpec((1, w), const2), pl.BlockSpec((1, w), const2),
                  pl.BlockSpec((LANES, LANES), const2)],
        out_specs=wide_spec,
        out_shape=jax.ShapeDtypeStruct((nb, l, w), BF16),
        compiler_params=_params("arbitrary", "arbitrary"),
        name="rwkv_post",
    )(yf, yb, bv, g, ln_g.reshape(1, w), ln_b.reshape(1, w), _head_blockdiag(LANES))


def _merge_kernel(h_ref, hn_ref, mod_ref, yp_ref, yr_ref, yg_ref, wzg_ref, pp_ref, pr_ref, pg_ref,
                  wo_ref, o_ref):
    @pl.when(pl.program_id(2) == 0)
    def _():
        o_ref[0] = h_ref[0]

    hn = hn_ref[0]
    merged = None
    for br, (y_ref, p_ref) in enumerate(((yp_ref, pp_ref), (yr_ref, pr_ref), (yg_ref, pg_ref))):
        gate = _sigmoid(jnp.dot(hn, wzg_ref[br], preferred_element_type=F32))
        term = gate * jnp.dot(y_ref[0], p_ref[...], preferred_element_type=F32)
        merged = term if merged is None else merged + term
    o_ref[0] += mod_ref[0, 5:6, :] * jnp.dot(merged.astype(BF16), wo_ref[...],
                                             preferred_element_type=F32)


def _merge(h, hn, mod, yp, yr, yg, wzg, pp, pr, pg, wo, *, li, tm, tn):
    nb, l, d = h.shape
    per_batch = mod.shape[0] > 1
    mod_map = (lambda b, i, n: (b, 0, 0)) if per_batch else (lambda b, i, n: (0, 0, 0))
    row = lambda width: pl.BlockSpec((1, tm, width), lambda b, i, n: (b, i, 0))
    col = lambda a: pl.BlockSpec((None, a.shape[1], tn), lambda b, i, n: (li, 0, n))
    return pl.pallas_call(
        _merge_kernel,
        grid=(nb, l // tm, d // tn),
        in_specs=[row(d), row(d), pl.BlockSpec((1, N_MOD, d), mod_map),
                  row(yp.shape[2]), row(yr.shape[2]), row(yg.shape[2]),
                  pl.BlockSpec((wzg.shape[0], d, tn), lambda b, i, n: (0, 0, n)),
                  col(pp), col(pr), col(pg),
                  pl.BlockSpec((None, tn, d), lambda b, i, n: (li, n, 0))],
        out_specs=row(d),
        out_shape=jax.ShapeDtypeStruct((nb, l, d), F32),
        compiler_params=_params("arbitrary", "arbitrary", "arbitrary"),
        name="merge",
    )(h, hn, mod, yp, yr, yg, wzg, pp, pr, pg, wo)


def _pick(n, pref):
    t = min(pref, n)
    while t > LANES and (n % t or t % LANES):
        t -= LANES
    return t if n % t == 0 else n


def kernel(x, c, ctx, c_ctx, ada_w, ada_b, norm_g, ffn_w_gate, ffn_w_up, ffn_w_down, w_in,
           pool_w, pool_scale, rwkv_mu, rwkv_w0, rwkv_w_up, rwkv_a0, rwkv_a_up, rwkv_g_up,
           rwkv_k_k, rwkv_k_a, rwkv_r_k, rwkv_ln_g, rwkv_ln_b, gmlp_ln_g, gmlp_ln_b, gmlp_ws,
           gmlp_bs, proj_pool, proj_rwkv, proj_gmlp, w_out, final_norm):
    depth = ada_w.shape[0]
    nb, l, d = x.shape
    lc = ctx.shape[1]
    width = rwkv_w0.shape[-1]
    pool_c = pool_scale.shape[-1]
    gmlp_c = gmlp_ln_g.shape[-1]
    rwkv_cols = rwkv_mu.shape[-1]
    off_rwkv = pool_c
    off_gmlp = off_rwkv + rwkv_cols
    off_gate = off_gmlp + 2 * gmlp_c
    npair = width // LANES

    cond8 = jnp.zeros((8, d), F32).at[:nb].set(c).at[nb].set(c_ctx)
    mod_all = _adaln(cond8, ada_w, ada_b).reshape(depth, 8, N_MOD, d)

    zero_state = jnp.zeros((2, nb, npair, LANES, LANES), F32)
    wg, wu, wd = (a.astype(BF16) for a in (ffn_w_gate, ffn_w_up, ffn_w_down))
    pp, pr, pg, wo = (a.astype(BF16) for a in (proj_pool, proj_rwkv, proj_gmlp, w_out))
    win_all = w_in.astype(BF16)
    tf = _pick(wg.shape[-1], 512)
    assert (2 * gmlp_c) % pool_c == 0
    hc = ctx.reshape(1, nb * lc, d)
    for li in range(depth):
        last = li == depth - 1
        mod_x = mod_all[li, :nb]
        mod_c = mod_all[li, nb:nb + 1]
        win = win_all[li]
        w_zb = win[:, off_rwkv:off_gmlp]
        w_uva = jnp.concatenate([win[:, off_gmlp:off_gate], win[:, :off_rwkv]], axis=1)
        w_zg = win[:, off_gate:].reshape(d, -1, d).transpose(1, 0, 2)
        lp = {'mu': rwkv_mu[li], 'w0': rwkv_w0[li], 'w_up': rwkv_w_up[li], 'a0': rwkv_a0[li],
              'a_up': rwkv_a_up[li], 'g_up': rwkv_g_up[li], 'k_k': rwkv_k_k[li],
              'k_a': rwkv_k_a[li], 'r_k': rwkv_r_k[li]}

        def mixer(h, hn, mod, nseq, seq, grid_mode, s0, need_out):
            flat = h.shape[:2]
            tm = _pick(flat[1], 1024)
            zb = _proj(hn, w_zb, tm=tm, tn=_pick(rwkv_cols, 1152))
            prep = _rwkv_prep(zb.reshape(nseq, seq, -1), lp, t_rows=_pick(seq, 256),
                              grid_mode=grid_mode)
            yf, yb, s_fin = _rwkv_scan(prep[:9], s0)
            if not need_out:
                return None, s_fin
            y_rwkv = _rwkv_post(yf, yb, prep[10], prep[9], rwkv_ln_g[li], rwkv_ln_b[li],
                                t_rows=_pick(seq, 256))
            z_uva = _proj(hn, w_uva, tm=tm,
                          tn=_pick(w_uva.shape[1], 768)).reshape(nseq, seq, -1)
            y_pool = _pool(z_uva, pool_w[li], pool_scale[li], col_block=2 * gmlp_c // pool_c,
                           t_rows=_pick(seq, 512))
            y_gmlp = _gmlp(z_uva, gmlp_ln_g[li], gmlp_ln_b[li], gmlp_ws[li], gmlp_bs[li],
                           t_rows=_pick(seq, 512))
            ys = [y.reshape(flat + (-1,)) for y in (y_pool, y_rwkv, y_gmlp)]
            out = _merge(h, hn, mod, *ys, w_zg, pp, pr, pg, wo,
                         li=li, tm=_pick(flat[1], 512), tn=_pick(d, 512))
            return out, s_fin

        def ffn(h, mod, k, tail, tail_g):
            return _ffn(h, mod, norm_g[li, 2 * k], wg, wu, wd, tail_g, sel=(li, k),
                        base=6 * k, tail=tail, tm=_pick(h.shape[1], 512), tf=tf)

        hc, hnc = ffn(hc, mod_c, 0, "mixer_norm", norm_g[li, 1])
        hc_new, s_ctx = mixer(hc, hnc, mod_c, nb, lc, False, zero_state, not last)
        if not last:
            hc = ffn(hc_new, mod_c, 1, None, final_norm)

        x, hn = ffn(x, mod_x, 0, "mixer_norm", norm_g[li, 1])
        x, _ = mixer(x, hn, mod_x, nb, l, True, s_ctx, True)
        x = ffn(x, mod_x, 1, "final" if last else None, final_norm)
    return x
```

```python
import functools
import math

import jax
import jax.numpy as jnp
from jax import lax
from jax.experimental import pallas as pl
from jax.experimental.pallas import tpu as pltpu

F32 = jnp.float32
BF16 = jnp.bfloat16
HIGHEST = lax.Precision.HIGHEST

NORM_EPS = 1e-6
LN_EPS = 1e-5
RWKV_GN_EPS = 64e-5
GRID_W = 64
N_MOD = 9
POOL_WINDOWS = (2, 4, 8, 16)
HEAD_DIM = 64
GMLP_CHUNK = 128
LANES = 128
SCAN_CHUNK = 64
VMEM_LIMIT = 56 * 1024 * 1024


def _params(*sem):
    return pltpu.CompilerParams(dimension_semantics=sem, vmem_limit_bytes=VMEM_LIMIT)


def _sigmoid(x):
    return 0.5 * jnp.tanh(0.5 * x) + 0.5


def _mm(a, b):
    return jnp.dot(a.astype(BF16), b.astype(BF16), preferred_element_type=F32)


def _mm_hi(a, b):
    return jnp.dot(a, b, precision=HIGHEST, preferred_element_type=F32)


def _mm_nt_hi(a, b):
    return lax.dot_general(a, b, (((1,), (1,)), ((), ())), precision=HIGHEST,
                           preferred_element_type=F32)


def _mm_tn_hi(a, b):
    return lax.dot_general(a, b, (((0,), (0,)), ((), ())), precision=HIGHEST,
                           preferred_element_type=F32)


def _rms(x, g):
    return x * lax.rsqrt(jnp.mean(x * x, axis=-1, keepdims=True) + NORM_EPS) * g


def _adaln_kernel(c_ref, w_ref, b_ref, o_ref):
    c = c_ref[...]
    o_ref[0] = _mm_hi(c * _sigmoid(c), w_ref[0]) + b_ref[0]


def _adaln(cond8, ada_w, ada_b):
    depth, d, n = ada_w.shape
    tn = 1024
    return pl.pallas_call(
        _adaln_kernel,
        grid=(depth, n // tn),
        in_specs=[pl.BlockSpec((8, d), lambda l, j: (0, 0)),
                  pl.BlockSpec((1, d, tn), lambda l, j: (l, 0, j)),
                  pl.BlockSpec((1, 1, tn), lambda l, j: (l, 0, j))],
        out_specs=pl.BlockSpec((1, 8, tn), lambda l, j: (l, 0, j)),
        out_shape=jax.ShapeDtypeStruct((depth, 8, n), F32),
        compiler_params=_params("arbitrary", "arbitrary"),
        name="adaln",
    )(cond8, ada_w, ada_b.reshape(depth, 1, n))


FFN_WEIGHT_SLOTS = 2


def _ffn_kernel(h_ref, mod_ref, g_ref, wg_hbm, wu_hbm, wd_hbm, tail_ref, *rest,
                base, tail, sel, tf):
    if tail == "mixer_norm":
        o_ref, hn_o_ref, hn_ref, wg_buf, wu_buf, wd_buf, sem = rest
    else:
        o_ref, hn_ref, wg_buf, wu_buf, wd_buf, sem = rest
    li, k = sel
    nf = wd_hbm.shape[2] // tf
    step = pl.program_id(0) * pl.num_programs(1) + pl.program_id(1)
    nsteps = pl.num_programs(0) * pl.num_programs(1)
    first_tile = step * nf

    def copies(j, slot):
        cols = pl.ds(j * tf, tf)
        return (pltpu.make_async_copy(wg_hbm.at[li, k, :, cols], wg_buf.at[slot], sem.at[slot, 0]),
                pltpu.make_async_copy(wu_hbm.at[li, k, :, cols], wu_buf.at[slot], sem.at[slot, 1]),
                pltpu.make_async_copy(wd_hbm.at[li, k, cols, :], wd_buf.at[slot], sem.at[slot, 2]))

    @pl.when(step == 0)
    def _():
        for cp in copies(0, 0):
            cp.start()

    hn = _rms(h_ref[0], g_ref[...]) * (1.0 + mod_ref[0, base + 1:base + 2, :]) \
        + mod_ref[0, base:base + 1, :]
    hn_ref[...] = hn.astype(BF16)
    o_ref[0] = h_ref[0]
    half_gate = 0.5 * mod_ref[0, base + 2:base + 3, :]

    for j in range(nf):
        slot = (first_tile + j) % FFN_WEIGHT_SLOTS
        nxt = (first_tile + j + 1) % FFN_WEIGHT_SLOTS
        if j + 1 < nf:
            for cp in copies(j + 1, nxt):
                cp.start()
        else:
            @pl.when(step + 1 < nsteps)
            def _():
                for cp in copies(0, nxt):
                    cp.start()
        for cp in copies(j, slot):
            cp.wait()
        hn = hn_ref[...]
        gate = jnp.dot(hn, wg_buf[slot], preferred_element_type=F32)
        up = jnp.dot(hn, wu_buf[slot], preferred_element_type=F32)
        act = (gate * _sigmoid(gate) * up).astype(BF16)
        o_ref[0] += half_gate * jnp.dot(act, wd_buf[slot], preferred_element_type=F32)

    if tail is not None:
        normed = _rms(o_ref[0], tail_ref[...])
        if tail == "final":
            o_ref[0] = normed
        else:
            hn_o_ref[0] = (normed * (1.0 + mod_ref[0, 4:5, :])
                           + mod_ref[0, 3:4, :]).astype(BF16)


def _ffn(h, mod, g, wg, wu, wd, tail_g, *, sel, base, tail, tm, tf):
    nb, l, d = h.shape
    assert wg.shape[-1] % tf == 0 and l % tm == 0
    per_batch = mod.shape[0] > 1
    mod_map = (lambda b, i: (b, 0, 0)) if per_batch else (lambda b, i: (0, 0, 0))
    row = pl.BlockSpec((1, tm, d), lambda b, i: (b, i, 0))
    hbm = pl.BlockSpec(memory_space=pl.ANY)
    out_specs, out_shape = row, jax.ShapeDtypeStruct((nb, l, d), F32)
    if tail == "mixer_norm":
        out_specs, out_shape = [row, row], [out_shape, jax.ShapeDtypeStruct((nb, l, d), BF16)]
    slots = FFN_WEIGHT_SLOTS
    return pl.pallas_call(
        functools.partial(_ffn_kernel, base=base, tail=tail, sel=sel, tf=tf),
        grid=(nb, l // tm),
        in_specs=[row,
                  pl.BlockSpec((1, N_MOD, d), mod_map),
                  pl.BlockSpec((1, d), lambda b, i: (0, 0)),
                  hbm, hbm, hbm,
                  pl.BlockSpec((1, d), lambda b, i: (0, 0))],
        out_specs=out_specs,
        out_shape=out_shape,
        scratch_shapes=[pltpu.VMEM((tm, d), BF16),
                        pltpu.VMEM((slots, d, tf), BF16),
                        pltpu.VMEM((slots, d, tf), BF16),
                        pltpu.VMEM((slots, tf, d), BF16),
                        pltpu.SemaphoreType.DMA((slots, 3))],
        compiler_params=_params("arbitrary", "arbitrary"),
        name="ffn",
    )(h, mod, g.reshape(1, d), wg, wu, wd, tail_g.reshape(1, d))


def _proj_kernel(hn_ref, w_ref, o_ref):
    o_ref[0] = jnp.dot(hn_ref[0], w_ref[...], preferred_element_type=F32)


def _proj(hn, w, *, tm, tn):
    nb, l, d = hn.shape
    n = w.shape[1]
    return pl.pallas_call(
        _proj_kernel,
        grid=(nb, l // tm, n // tn),
        in_specs=[pl.BlockSpec((1, tm, d), lambda b, i, j: (b, i, 0)),
                  pl.BlockSpec((d, tn), lambda b, i, j: (0, j))],
        out_specs=pl.BlockSpec((1, tm, tn), lambda b, i, j: (b, i, j)),
        out_shape=jax.ShapeDtypeStruct((nb, l, n), F32),
        compiler_params=_params("arbitrary", "arbitrary", "arbitrary"),
        name="proj",
    )(hn, w)


POOL_HALO = 8


def _pool_kernel(prev_ref, x_ref, next_ref, pw_ref, ps_ref, o_ref, scr, *, t_rows, seq):
    i = pl.program_id(1)
    last = pl.num_programs(1) - 1
    hal = POOL_HALO
    scr[0:hal] = jnp.where(i > 0, prev_ref[0], 0.0)
    scr[hal:hal + t_rows] = x_ref[0]
    scr[hal + t_rows:2 * hal + t_rows] = jnp.where(i < last, next_ref[0], 0.0)
    t = i * t_rows + lax.broadcasted_iota(jnp.int32, (t_rows, LANES), 0)
    for gi, w in enumerate(POOL_WINDOWS):
        c0 = gi * LANES
        s = scr[hal - w // 2:hal - w // 2 + t_rows, c0:c0 + LANES]
        for dlt in range(-(w // 2) + 1, w // 2):
            s = s + scr[hal + dlt:hal + dlt + t_rows, c0:c0 + LANES]
        lo = jnp.clip(t - w // 2, 0, seq)
        hi = jnp.clip(t - w // 2 + w, 0, seq)
        cnt = (hi - lo).astype(F32)
        p = s / cnt - scr[hal:hal + t_rows, c0:c0 + LANES]
        y = _mm(p, pw_ref[gi]) * ps_ref[:, c0:c0 + LANES]
        o_ref[0, :, c0:c0 + LANES] = y.astype(o_ref.dtype)


def _pool(z, pool_w, pool_scale, *, col_block, t_rows):
    nb, l, _ = z.shape
    c = pool_scale.shape[-1]
    hal = POOL_HALO
    r = t_rows // hal
    nh = l // hal
    cb = col_block
    return pl.pallas_call(
        functools.partial(_pool_kernel, t_rows=t_rows, seq=l),
        grid=(nb, l // t_rows),
        in_specs=[pl.BlockSpec((1, hal, c), lambda b, i: (b, jnp.maximum(i * r - 1, 0), cb)),
                  pl.BlockSpec((1, t_rows, c), lambda b, i: (b, i, cb)),
                  pl.BlockSpec((1, hal, c), lambda b, i: (b, jnp.minimum((i + 1) * r, nh - 1), cb)),
                  pl.BlockSpec(pool_w.shape, lambda b, i: (0, 0, 0)),
                  pl.BlockSpec((1, c), lambda b, i: (0, 0))],
        out_specs=pl.BlockSpec((1, t_rows, c), lambda b, i: (b, i, 0)),
        out_shape=jax.ShapeDtypeStruct((nb, l, c), BF16),
        scratch_shapes=[pltpu.VMEM((t_rows + 2 * hal, c), F32)],
        compiler_params=_params("arbitrary", "arbitrary"),
        name="pool",
    )(z, z, z, pool_w.astype(BF16), pool_scale.reshape(1, c))


def _gelu(x):
    return x * (0.5 * (1.0 + jnp.tanh(math.sqrt(2.0 / math.pi) * (x + 0.044715 * (x * x * x)))))


def _gmlp_kernel(z_ref, lng_ref, lnb_ref, ws_ref, bsb_ref, o_ref, *, t_rows):
    width = o_ref.shape[2]
    groups = ws_ref.shape[0]
    gd = width // groups
    for ci in range(t_rows // GMLP_CHUNK):
        rows = slice(ci * GMLP_CHUNK, (ci + 1) * GMLP_CHUNK)
        u = _gelu(z_ref[0, rows, 0:width])
        v = _gelu(z_ref[0, rows, width:2 * width])
        mu = jnp.mean(v, axis=-1, keepdims=True)
        var = jnp.mean(jnp.square(v - mu), axis=-1, keepdims=True)
        vn = ((v - mu) * lax.rsqrt(var + LN_EPS)) * lng_ref[...] + lnb_ref[...]
        for g in range(groups):
            cols = slice(g * gd, (g + 1) * gd)
            s = _mm(ws_ref[g], vn[:, cols]) + bsb_ref[g]
            o_ref[0, rows, cols] = (u[:, cols] * s).astype(o_ref.dtype)


def _gmlp(zuv, ln_g, ln_b, ws, bs, *, t_rows):
    nb, l, _ = zuv.shape
    c = ln_g.shape[-1]
    c2 = 2 * c
    groups = ws.shape[0]
    bsb = jnp.broadcast_to(bs[:, :, None], (groups, GMLP_CHUNK, c // groups))
    return pl.pallas_call(
        functools.partial(_gmlp_kernel, t_rows=t_rows),
        grid=(nb, l // t_rows),
        in_specs=[pl.BlockSpec((1, t_rows, c2), lambda b, i: (b, i, 0)),
                  pl.BlockSpec((1, c), lambda b, i: (0, 0)),
                  pl.BlockSpec((1, c), lambda b, i: (0, 0)),
                  pl.BlockSpec(ws.shape, lambda b, i: (0, 0, 0)),
                  pl.BlockSpec(bsb.shape, lambda b, i: (0, 0, 0))],
        out_specs=pl.BlockSpec((1, t_rows, c), lambda b, i: (b, i, 0)),
        out_shape=jax.ShapeDtypeStruct((nb, l, c), BF16),
        compiler_params=_params("arbitrary", "arbitrary"),
        name="gmlp",
    )(zuv, ln_g.reshape(1, c), ln_b.reshape(1, c), ws.astype(BF16), bsb)


def _segsum(x, bd):
    hi = x.astype(BF16)
    lo = (x - hi.astype(F32)).astype(BF16)
    out = []
    for p in range(x.shape[-1] // LANES):
        cols = slice(p * LANES, (p + 1) * LANES)
        out.append(jnp.dot(hi[:, cols], bd, preferred_element_type=F32)
                   + jnp.dot(lo[:, cols], bd, preferred_element_type=F32))
    return jnp.concatenate(out, axis=-1)


def _rwkv_prep_kernel(prev_ref, x_ref, next_ref, mu_ref, w0_ref, wup_ref, a0_ref, aup_ref,
                      gup_ref, kk_ref, ka_ref, rk_ref, bd_ref,
                      lw0_o, lw1_o, kd0_o, kd1_o, b0_o, b1_o, kk_o, v_o, r_o, g_o, bv_o,
                      scr, *, t_rows, grid_mode, width):
    i = pl.program_id(1)
    last = pl.num_programs(1) - 1
    hal = GRID_W
    scr[0:hal] = jnp.where(i > 0, prev_ref[0], 0.0)
    scr[hal:hal + t_rows] = x_ref[0]
    scr[hal + t_rows:2 * hal + t_rows] = jnp.where(i < last, next_ref[0], 0.0)
    cols = scr.shape[1]
    if grid_mode:
        q = cols // 4
        bounds = (0, q, 2 * q, 3 * q, cols)
        offs = (-1, 1, -GRID_W, GRID_W)
    else:
        bounds = (0, cols // 2, cols)
        offs = (-1, 1)

    def zs_cols(c0, c1):
        n = c1 - c0
        x = scr[hal:hal + t_rows, c0:c1]
        ch = c0 + lax.broadcasted_iota(jnp.int32, (t_rows, n), 1)
        col = lax.broadcasted_iota(jnp.int32, (t_rows, n), 0) & (GRID_W - 1)
        shifted = None
        for qi, off in enumerate(offs):
            lo, hi = bounds[qi], bounds[qi + 1]
            if hi <= c0 or lo >= c1:
                continue
            src = scr[hal + off:hal + off + t_rows, c0:c1]
            if grid_mode and off == -1:
                src = jnp.where(col == 0, 0.0, src)
            if grid_mode and off == 1:
                src = jnp.where(col == GRID_W - 1, 0.0, src)
            shifted = src if shifted is None else jnp.where(ch >= lo, src, shifted)
        return x + (shifted - x) * mu_ref[:, c0:c1]

    w = width
    r = zs_cols(0, w)
    k = zs_cols(w, 2 * w)
    v = zs_cols(2 * w, 3 * w)
    rest = zs_cols(3 * w, cols)
    wd = jnp.tanh(rest[:, 0:LANES])
    ad = rest[:, LANES:2 * LANES]
    gd = _sigmoid(rest[:, 2 * LANES:3 * LANES])

    bd = bd_ref[...]
    kk0 = k * kk_ref[...]
    ss = _segsum(kk0 * kk0, bd)
    kk = kk0 / jnp.maximum(jnp.sqrt(ss), 1e-12)
    npair = w // LANES

    def put(o_ref, val):
        for p in range(npair):
            o_ref[0, p] = val[:, p * LANES:(p + 1) * LANES].astype(o_ref.dtype)

    put(kk_o, kk)
    put(v_o, v)
    put(r_o, r)
    kd_sum = None
    for d, (lw_o, kd_o, b_o) in enumerate(((lw0_o, kd0_o, b0_o), (lw1_o, kd1_o, b1_o))):
        w_pre = w0_ref[d] + _mm(wd, wup_ref[d])
        put(lw_o, -math.exp(-0.5) * _sigmoid(w_pre))
        a = _sigmoid(a0_ref[d] + _mm(ad, aup_ref[d]))
        kd = k * (1.0 + (a - 1.0) * ka_ref[...])
        put(kd_o, kd)
        put(b_o, kk * a)
        kd_sum = kd if kd_sum is None else kd_sum + kd
    g_o[0] = _mm(gd, gup_ref[...]).astype(g_o.dtype)
    bv_o[0] = (_segsum(r * kd_sum * rk_ref[...], bd) * v).astype(bv_o.dtype)


def _head_blockdiag(width):
    idx = jnp.arange(width) // HEAD_DIM
    return (idx[:, None] == idx[None, :]).astype(BF16)


def _rwkv_prep(zb, lp, *, t_rows, grid_mode):
    nb, l, cols = zb.shape
    w = lp['w0'].shape[-1]
    npair = w // LANES
    hal = GRID_W
    r = t_rows // hal
    nh = l // hal
    lora = lp['w_up'].shape[1]

    def pad_dir(up):
        z = jnp.zeros_like(up[0])
        return jnp.stack([jnp.concatenate([up[0], z], 0), jnp.concatenate([z, up[1]], 0)], 0)

    const2 = lambda b, i: (0, 0)
    const3 = lambda b, i: (0, 0, 0)
    pair_spec = pl.BlockSpec((1, npair, t_rows, LANES), lambda b, i: (b, 0, i, 0))
    wide_spec = pl.BlockSpec((1, t_rows, w), lambda b, i: (b, i, 0))
    pair_shape = lambda dt: jax.ShapeDtypeStruct((nb, npair, l, LANES), dt)
    wide_shape = jax.ShapeDtypeStruct((nb, l, w), BF16)
    return pl.pallas_call(
        functools.partial(_rwkv_prep_kernel, t_rows=t_rows, grid_mode=grid_mode, width=w),
        grid=(nb, l // t_rows),
        in_specs=[pl.BlockSpec((1, hal, cols), lambda b, i: (b, jnp.maximum(i * r - 1, 0), 0)),
                  pl.BlockSpec((1, t_rows, cols), lambda b, i: (b, i, 0)),
                  pl.BlockSpec((1, hal, cols), lambda b, i: (b, jnp.minimum((i + 1) * r, nh - 1), 0)),
                  pl.BlockSpec((1, cols), const2),
                  pl.BlockSpec((2, 1, w), const3),
                  pl.BlockSpec((2, 2 * lora, w), const3),
                  pl.BlockSpec((2, 1, w), const3),
                  pl.BlockSpec((2, 2 * lora, w), const3),
                  pl.BlockSpec(lp['g_up'].shape, const2),
                  pl.BlockSpec((1, w), const2),
                  pl.BlockSpec((1, w), const2),
                  pl.BlockSpec((1, w), const2),
                  pl.BlockSpec((LANES, LANES), const2)],
        out_specs=[pair_spec] * 9 + [wide_spec] * 2,
        out_shape=[pair_shape(F32)] * 2 + [pair_shape(BF16)] * 7 + [wide_shape] * 2,
        scratch_shapes=[pltpu.VMEM((t_rows + 2 * hal, cols), F32)],
        compiler_params=_params("arbitrary", "arbitrary"),
        name="rwkv_prep",
    )(zb, zb, zb, lp['mu'].reshape(1, cols), lp['w0'].reshape(2, 1, w),
      pad_dir(lp['w_up']).astype(BF16), lp['a0'].reshape(2, 1, w),
      pad_dir(lp['a_up']).astype(BF16), lp['g_up'].astype(BF16), lp['k_k'].reshape(1, w),
      lp['k_a'].reshape(1, w), lp['r_k'].reshape(1, w), _head_blockdiag(LANES))


def _stack2(x, m0):
    xs = x.astype(BF16)
    zero = jnp.zeros_like(xs)
    return jnp.concatenate([jnp.where(m0, xs, zero), jnp.where(m0, zero, xs)], axis=0)


_NN = (((1,), (0,)), ((), ()))
_NT = (((1,), (1,)), ((), ()))
_TN = (((0,), (0,)), ((), ()))

SCAN_PREC = {"cum": "rhs2", "gram": "bf16", "init": "bf16", "apply": "bf16", "out": "bf16",
             "state": "bf16"}
SCAN_LOCKSTEP_PAIRS = 8

def _split(a):
    hi = a.astype(BF16)
    return hi, (a - hi.astype(F32)).astype(BF16)


def _dg(a, b, dims, site):
    mode = SCAN_PREC[site]
    if mode == "f32":
        return lax.dot_general(a, b, dims, precision=HIGHEST, preferred_element_type=F32)
    if mode == "bf16":
        return lax.dot_general(a.astype(BF16), b.astype(BF16), dims, preferred_element_type=F32)
    b_hi, b_lo = _split(b)
    if mode == "rhs2":
        a16 = a.astype(BF16)
        return (lax.dot_general(a16, b_hi, dims, preferred_element_type=F32)
                + lax.dot_general(a16, b_lo, dims, preferred_element_type=F32))
    a_hi, a_lo = _split(a)
    return (lax.dot_general(a_hi, b_hi, dims, preferred_element_type=F32)
            + lax.dot_general(a_hi, b_lo, dims, preferred_element_type=F32)
            + lax.dot_general(a_lo, b_hi, dims, preferred_element_type=F32))


def _chunk_local(refs, rows, *, rev, tri, masks, m0):
    lw, kd, b, kk, v, r = (x[rows].astype(F32) for x in refs)
    strict, incl, _ = masks
    c = lw.shape[0]
    cum = _dg(tri, lw, _NN, "cum")
    yield
    cum_prev = cum - lw
    end = 0 if rev else c - 1
    tot = cum[end:end + 1, :]
    mid = cum[c // 2:c // 2 + 1, :]
    e_inv = jnp.exp(mid - cum)
    nkk = -kk
    left = jnp.concatenate([_stack2(nkk * jnp.exp(cum_prev - mid), m0),
                            _stack2(r * jnp.exp(cum - mid), m0)], axis=0)
    right = jnp.concatenate([_stack2(b * e_inv, m0), _stack2(kd * e_inv, m0)], axis=0)
    gram = _dg(left, right, _NT, "gram")
    yield
    c2 = 2 * c
    g16 = gram.astype(BF16)
    zero16 = jnp.zeros((c2, c2), BF16)
    vsw = pltpu.roll(v, HEAD_DIM, 1).astype(BF16)
    zv = jnp.zeros_like(vsw)
    e_end = jnp.exp(tot - cum)
    return dict(
        a_ab=jnp.where(strict, g16[0:c2, 0:c2], zero16),
        a_ak=jnp.where(strict, g16[0:c2, c2:2 * c2], zero16),
        a_rb=jnp.where(incl, g16[c2:2 * c2, 0:c2], zero16),
        a_rk=jnp.where(incl, g16[c2:2 * c2, c2:2 * c2], zero16),
        vbd=jnp.concatenate([jnp.where(m0, zv, vsw), jnp.where(m0, vsw, zv)], axis=0),
        a0=_stack2(nkk * jnp.exp(cum_prev), m0), r0=_stack2(r * jnp.exp(cum), m0),
        bh=_stack2(b * e_end, m0), kh=_stack2(kd * e_end, m0), decay=jnp.exp(tot))


def _chunk_solve(loc, s, diag):
    vbd = loc["vbd"]
    c2 = vbd.shape[0]
    x = _dg(loc["a0"], s, _NT, "init") + _dg(loc["a_ak"], vbd, _NN, "init")
    ybd = _dg(loc["r0"], s, _NT, "out") + _dg(loc["a_rk"], vbd, _NN, "out")
    s_new = s * loc["decay"] + _dg(vbd, loc["kh"], _TN, "state")
    steps = max(1, (c2 // 2 - 1).bit_length())
    pw = loc["a_ab"]
    rhs = jnp.where(diag, pw, x.astype(BF16))
    for it in range(steps):
        yield
        both = _dg(pw, rhs, _NN, "apply")
        x = x + both
        if it < steps - 1:
            both16 = both.astype(BF16)
            pw = jnp.where(diag, both16, jnp.zeros_like(both16))
            rhs = jnp.where(diag, both16, x.astype(BF16))
    yield
    u = jnp.where(diag, 0.0, x)
    ybd = ybd + _dg(loc["a_rb"], u, _NN, "out")
    y = pltpu.roll(ybd[0:c2 // 2] + ybd[c2 // 2:c2], HEAD_DIM, 1)
    return y, s_new + _dg(u, loc["bh"], _TN, "state")


SCAN_CHUNKS_PER_STEP = 4
SCAN_OVERLAP_STAGES = (4, 5, 6)


def _chain(refs, s_ref, order, **kw):
    c = SCAN_CHUNK
    rows = [slice(k * c, (k + 1) * c) for k in order]
    diag = kw["masks"][2]
    s = s_ref[...]
    ys = []
    loc = yield from _chunk_local(refs, rows[0], **kw)
    for k in range(len(rows)):
        nxt_local = _chunk_local(refs, rows[k + 1], **kw) if k + 1 < len(rows) else None
        nxt_loc = None
        solve, stage = _chunk_solve(loc, s, diag), 0
        while True:
            try:
                next(solve)
            except StopIteration as stop:
                y, s = stop.value
                break
            stage += 1
            if nxt_local is not None and nxt_loc is None and stage in SCAN_OVERLAP_STAGES:
                try:
                    next(nxt_local)
                except StopIteration as stop:
                    nxt_loc = stop.value
            yield
        ys.append((rows[k], y))
        if nxt_local is not None:
            while nxt_loc is None:
                try:
                    next(nxt_local)
                except StopIteration as stop:
                    nxt_loc = stop.value
            loc = nxt_loc
            yield
    return ys, s


def _lockstep(gens):
    results = [None] * len(gens)
    live = list(range(len(gens)))
    while live:
        still = []
        for i in live:
            try:
                next(gens[i])
                still.append(i)
            except StopIteration as stop:
                results[i] = stop.value
        live = still
    return results


def _rwkv_scan_kernel(lw0, kd0, b0, kkf, vf, rf, lw1, kd1, b1, kkb, vb, rb, s0_ref,
                      yf_o, yb_o, sfin_o, st):
    ci = pl.program_id(1)
    c = SCAN_CHUNK
    npair = lw0.shape[1]

    @pl.when(ci == 0)
    def _():
        st[...] = s0_ref[:, 0]

    row = lax.broadcasted_iota(jnp.int32, (c, c), 0)
    colm = lax.broadcasted_iota(jnp.int32, (c, c), 1)
    tri_f = (colm <= row).astype(F32)
    tri_b = (colm >= row).astype(F32)
    rows2 = lax.broadcasted_iota(jnp.int32, (2 * c, 2 * c), 0)
    cols2 = lax.broadcasted_iota(jnp.int32, (2 * c, 2 * c), 1)
    r2 = rows2 & (c - 1)
    c2 = cols2 & (c - 1)
    diag = (rows2 < c) == (cols2 < c)
    masks_f = (c2 < r2, c2 <= r2, diag)
    masks_b = (c2 > r2, c2 >= r2, diag)
    m0 =lax.broadcasted_iota(jnp.int32, (1, LANES), 1) < HEAD_DIM

    fwd_refs = (lw0, kd0, b0, kkf, vf, rf)
    bwd_refs = (lw1, kd1, b1, kkb, vb, rb)
    order = tuple(range(lw0.shape[2] // c))
    for p0 in range(0, npair, SCAN_LOCKSTEP_PAIRS):
        gens, outs = [], []
        for p in range(p0, min(p0 + SCAN_LOCKSTEP_PAIRS, npair)):
            gens.append(_chain([x.at[0, p] for x in fwd_refs], st.at[0, p], order, rev=False,
                               tri=tri_f, masks=masks_f, m0=m0))
            outs.append((yf_o, 0, p))
            gens.append(_chain([x.at[0, p] for x in bwd_refs], st.at[1, p], order[::-1], rev=True,
                               tri=tri_b, masks=masks_b, m0=m0))
            outs.append((yb_o, 1, p))
        for (y_o, d, p), (ys, s_new) in zip(outs, _lockstep(gens)):
            for rows, y in ys:
                y_o[0, p, rows] = y.astype(y_o.dtype)
            st[d, p] = s_new

    @pl.when(ci == pl.num_programs(1) - 1)
    def _():
        sfin_o[:, 0] = st[...]


def _rwkv_scan(prep, s0):
    lw0, lw1, kd0, kd1, b0, b1, kk, v, r = prep
    nb, npair, l, _ = lw0.shape
    c = SCAN_CHUNK
    rows = SCAN_CHUNKS_PER_STEP * c
    assert c == HEAD_DIM and 2 * HEAD_DIM == LANES and l % rows == 0
    n = l // rows
    fwd = pl.BlockSpec((1, npair, rows, LANES), lambda b, i: (b, 0, i, 0))
    bwd = pl.BlockSpec((1, npair, rows, LANES), lambda b, i: (b, 0, n - 1 - i, 0))
    st_spec = pl.BlockSpec((2, 1, npair, LANES, LANES), lambda b, i: (0, b, 0, 0, 0))
    y_shape = jax.ShapeDtypeStruct((nb, npair, l, LANES), BF16)
    return pl.pallas_call(
        _rwkv_scan_kernel,
        grid=(nb, n),
        in_specs=[fwd] * 6 + [bwd] * 6 + [st_spec],
        out_specs=[fwd, bwd, st_spec],
        out_shape=[y_shape, y_shape, jax.ShapeDtypeStruct(s0.shape, F32)],
        scratch_shapes=[pltpu.VMEM((2, npair, LANES, LANES), F32)],
        compiler_params=_params("arbitrary", "arbitrary"),
        name="rwkv_scan",
    )(lw0, kd0, b0, kk, v, r, lw1, kd1, b1, kk, v, r, s0)


def _rwkv_post_kernel(yf_ref, yb_ref, bv_ref, g_ref, lng_ref, lnb_ref, bd_ref, o_ref):
    npair = yf_ref.shape[1]
    y = jnp.concatenate([yf_ref[0, p].astype(F32) + yb_ref[0, p].astype(F32)
                         for p in range(npair)], axis=-1)
    bd = bd_ref[...]
    inv = 1.0 / HEAD_DIM
    m = _segsum(y, bd) * inv
    dlt = y - m
    var = _segsum(dlt * dlt, bd) * inv
    yn = dlt * lax.rsqrt(var + RWKV_GN_EPS) * lng_ref[...] + lnb_ref[...]
    o_ref[0] = ((yn + bv_ref[0].astype(F32)) * g_ref[0].astype(F32)).astype(o_ref.dtype)


def _rwkv_post(yf, yb, bv, g, ln_g, ln_b, *, t_rows):
    nb, npair, l, _ = yf.shape
    w = npair * LANES
    pair_spec = pl.BlockSpec((1, npair, t_rows, LANES), lambda b, i: (b, 0, i, 0))
    wide_spec = pl.BlockSpec((1, t_rows, w), lambda b, i: (b, i, 0))
    const2 = lambda b, i: (0, 0)
    return pl.pallas_call(
        _rwkv_post_kernel,
        grid=(nb, l // t_rows),
        in_specs=[pair_spec, pair_spec, wide_spec, wide_spec,
                  pl.BlockSpec((1, w), const2), pl.BlockSpec((1, w), const2),
                  pl.BlockSpec((LANES, LANES), const2)],
        out_specs=wide_spec,
        out_shape=jax.ShapeDtypeStruct((nb, l, w), BF16),
        compiler_params=_params("arbitrary", "arbitrary"),
        name="rwkv_post",
    )(yf, yb, bv, g, ln_g.reshape(1, w), ln_b.reshape(1, w), _head_blockdiag(LANES))


def _merge_kernel(h_ref, hn_ref, mod_ref, yp_ref, yr_ref, yg_ref, wzg_ref, pp_ref, pr_ref, pg_ref,
                  wo_ref, o_ref):
    @pl.when(pl.program_id(2) == 0)
    def _():
        o_ref[0] = h_ref[0]

    hn = hn_ref[0]
    merged = None
    for br, (y_ref, p_ref) in enumerate(((yp_ref, pp_ref), (yr_ref, pr_ref), (yg_ref, pg_ref))):
        gate = _sigmoid(jnp.dot(hn, wzg_ref[br], preferred_element_type=F32))
        term = gate * jnp.dot(y_ref[0], p_ref[...], preferred_element_type=F32)
        merged = term if merged is None else merged + term
    o_ref[0] += mod_ref[0, 5:6, :] * jnp.dot(merged.astype(BF16), wo_ref[...],
                                             preferred_element_type=F32)


def _merge(h, hn, mod, yp, yr, yg, wzg, pp, pr, pg, wo, *, li, tm, tn):
    nb, l, d = h.shape
    per_batch = mod.shape[0] > 1
    mod_map = (lambda b, i, n: (b, 0, 0)) if per_batch else (lambda b, i, n: (0, 0, 0))
    row = lambda width: pl.BlockSpec((1, tm, width), lambda b, i, n: (b, i, 0))
    col = lambda a: pl.BlockSpec((None, a.shape[1], tn), lambda b, i, n: (li, 0, n))
    return pl.pallas_call(
        _merge_kernel,
        grid=(nb, l // tm, d // tn),
        in_specs=[row(d), row(d), pl.BlockSpec((1, N_MOD, d), mod_map),
                  row(yp.shape[2]), row(yr.shape[2]), row(yg.shape[2]),
                  pl.BlockSpec((wzg.shape[0], d, tn), lambda b, i, n: (0, 0, n)),
                  col(pp), col(pr), col(pg),
                  pl.BlockSpec((None, tn, d), lambda b, i, n: (li, n, 0))],
        out_specs=row(d),
        out_shape=jax.ShapeDtypeStruct((nb, l, d), F32),
        compiler_params=_params("arbitrary", "arbitrary", "arbitrary"),
        name="merge",
    )(h, hn, mod, yp, yr, yg, wzg, pp, pr, pg, wo)


def _pick(n, pref):
    t = min(pref, n)
    while t > LANES and (n % t or t % LANES):
        t -= LANES
    return t if n % t == 0 else n


def kernel(x, c, ctx, c_ctx, ada_w, ada_b, norm_g, ffn_w_gate, ffn_w_up, ffn_w_down, w_in,
           pool_w, pool_scale, rwkv_mu, rwkv_w0, rwkv_w_up, rwkv_a0, rwkv_a_up, rwkv_g_up,
           rwkv_k_k, rwkv_k_a, rwkv_r_k, rwkv_ln_g, rwkv_ln_b, gmlp_ln_g, gmlp_ln_b, gmlp_ws,
           gmlp_bs, proj_pool, proj_rwkv, proj_gmlp, w_out, final_norm):
    depth = ada_w.shape[0]
    nb, l, d = x.shape
    lc = ctx.shape[1]
    width = rwkv_w0.shape[-1]
    pool_c = pool_scale.shape[-1]
    gmlp_c = gmlp_ln_g.shape[-1]
    rwkv_cols = rwkv_mu.shape[-1]
    off_rwkv = pool_c
    off_gmlp = off_rwkv + rwkv_cols
    off_gate = off_gmlp + 2 * gmlp_c
    npair = width // LANES

    cond8 = jnp.zeros((8, d), F32).at[:nb].set(c).at[nb].set(c_ctx)
    mod_all = _adaln(cond8, ada_w, ada_b).reshape(depth, 8, N_MOD, d)

    zero_state = jnp.zeros((2, nb, npair, LANES, LANES), F32)
    wg, wu, wd = (a.astype(BF16) for a in (ffn_w_gate, ffn_w_up, ffn_w_down))
    pp, pr, pg, wo = (a.astype(BF16) for a in (proj_pool, proj_rwkv, proj_gmlp, w_out))
    win_all = w_in.astype(BF16)
    tf = _pick(wg.shape[-1], 512)
    assert (2 * gmlp_c) % pool_c == 0
    hc = ctx.reshape(1, nb * lc, d)
    for li in range(depth):
        last = li == depth - 1
        mod_x = mod_all[li, :nb]
        mod_c = mod_all[li, nb:nb + 1]
        win = win_all[li]
        w_zb = win[:, off_rwkv:off_gmlp]
        w_uva = jnp.concatenate([win[:, off_gmlp:off_gate], win[:, :off_rwkv]], axis=1)
        w_zg = win[:, off_gate:].reshape(d, -1, d).transpose(1, 0, 2)
        lp = {'mu': rwkv_mu[li], 'w0': rwkv_w0[li], 'w_up': rwkv_w_up[li], 'a0': rwkv_a0[li],
              'a_up': rwkv_a_up[li], 'g_up': rwkv_g_up[li], 'k_k': rwkv_k_k[li],
              'k_a': rwkv_k_a[li], 'r_k': rwkv_r_k[li]}

        def mixer(h, hn, mod, nseq, seq, grid_mode, s0, need_out):
            flat = h.shape[:2]
            tm = _pick(flat[1], 1024)
            zb = _proj(hn, w_zb, tm=tm, tn=_pick(rwkv_cols, 1152))
            prep = _rwkv_prep(zb.reshape(nseq, seq, -1), lp, t_rows=_pick(seq, 256),
                              grid_mode=grid_mode)
            yf, yb, s_fin = _rwkv_scan(prep[:9], s0)
            if not need_out:
                return None, s_fin
            y_rwkv = _rwkv_post(yf, yb, prep[10], prep[9], rwkv_ln_g[li], rwkv_ln_b[li],
                                t_rows=_pick(seq, 256))
            z_uva = _proj(hn, w_uva, tm=tm,
                          tn=_pick(w_uva.shape[1], 768)).reshape(nseq, seq, -1)
            y_pool = _pool(z_uva, pool_w[li], pool_scale[li], col_block=2 * gmlp_c // pool_c,
                           t_rows=_pick(seq, 512))
            y_gmlp = _gmlp(z_uva, gmlp_ln_g[li], gmlp_ln_b[li], gmlp_ws[li], gmlp_bs[li],
                           t_rows=_pick(seq, 512))
            ys = [y.reshape(flat + (-1,)) for y in (y_pool, y_rwkv, y_gmlp)]
            out = _merge(h, hn, mod, *ys, w_zg, pp, pr, pg, wo,
                         li=li, tm=_pick(flat[1], 512), tn=_pick(d, 512))
            return out, s_fin

        def ffn(h, mod, k, tail, tail_g):
            return _ffn(h, mod, norm_g[li, 2 * k], wg, wu, wd, tail_g, sel=(li, k),
                        base=6 * k, tail=tail, tm=_pick(h.shape[1], 512), tf=tf)

        hc, hnc = ffn(hc, mod_c, 0, "mixer_norm", norm_g[li, 1])
        hc_new, s_ctx = mixer(hc, hnc, mod_c, nb, lc, False, zero_state, not last)
        if not last:
            hc = ffn(hc_new, mod_c, 1, None, final_norm)

        x, hn = ffn(x, mod_x, 0, "mixer_norm", norm_g[li, 1])
        x, _ = mixer(x, hn, mod_x, nb, l, True, s_ctx, True)
        x = ffn(x, mod_x, 1, "final" if last else None, final_norm)
    return x
```

```python
import functools
import math

import jax
import jax.numpy as jnp
from jax import lax
from jax.experimental import pallas as pl
from jax.experimental.pallas import tpu as pltpu

F32 = jnp.float32
BF16 = jnp.bfloat16
HIGHEST = lax.Precision.HIGHEST

NORM_EPS = 1e-6
LN_EPS = 1e-5
RWKV_GN_EPS = 64e-5
GRID_W = 64
N_MOD = 9
POOL_WINDOWS = (2, 4, 8, 16)
HEAD_DIM = 64
GMLP_CHUNK = 128
LANES = 128
SCAN_CHUNK = 64
VMEM_LIMIT = 56 * 1024 * 1024


def _params(*sem):
    return pltpu.CompilerParams(dimension_semantics=sem, vmem_limit_bytes=VMEM_LIMIT)


def _sigmoid(x):
    return 0.5 * jnp.tanh(0.5 * x) + 0.5


def _mm(a, b):
    return jnp.dot(a.astype(BF16), b.astype(BF16), preferred_element_type=F32)


def _mm_hi(a, b):
    return jnp.dot(a, b, precision=HIGHEST, preferred_element_type=F32)


def _rms(x, g):
    return x * lax.rsqrt(jnp.mean(x * x, axis=-1, keepdims=True) + NORM_EPS) * g


def _adaln_kernel(c_ref, w_ref, b_ref, o_ref):
    c = c_ref[...]
    o_ref[0] = _mm_hi(c * _sigmoid(c), w_ref[0]) + b_ref[0]


def _adaln(cond8, ada_w, ada_b):
    depth, d, n = ada_w.shape
    tn = 1024
    return pl.pallas_call(
        _adaln_kernel,
        grid=(depth, n // tn),
        in_specs=[pl.BlockSpec((8, d), lambda l, j: (0, 0)),
                  pl.BlockSpec((1, d, tn), lambda l, j: (l, 0, j)),
                  pl.BlockSpec((1, 1, tn), lambda l, j: (l, 0, j))],
        out_specs=pl.BlockSpec((1, 8, tn), lambda l, j: (l, 0, j)),
        out_shape=jax.ShapeDtypeStruct((depth, 8, n), F32),
        compiler_params=_params("arbitrary", "arbitrary"),
        name="adaln",
    )(cond8, ada_w, ada_b.reshape(depth, 1, n))


FFN_WEIGHT_SLOTS = 2


def _ffn_kernel(h_ref, mod_ref, g_ref, wg_hbm, wu_hbm, wd_hbm, tail_ref, *rest,
                base, tail, sel, tf):
    if tail == "mixer_norm":
        o_ref, hn_o_ref, hn_ref, wg_buf, wu_buf, wd_buf, sem = rest
    else:
        o_ref, hn_ref, wg_buf, wu_buf, wd_buf, sem = rest
    li, k = sel
    nf = wd_hbm.shape[2] // tf
    step = pl.program_id(0) * pl.num_programs(1) + pl.program_id(1)
    nsteps = pl.num_programs(0) * pl.num_programs(1)
    first_tile = step * nf

    def copies(j, slot):
        cols = pl.ds(j * tf, tf)
        return (pltpu.make_async_copy(wg_hbm.at[li, k, :, cols], wg_buf.at[slot], sem.at[slot, 0]),
                pltpu.make_async_copy(wu_hbm.at[li, k, :, cols], wu_buf.at[slot], sem.at[slot, 1]),
                pltpu.make_async_copy(wd_hbm.at[li, k, cols, :], wd_buf.at[slot], sem.at[slot, 2]))

    @pl.when(step == 0)
    def _():
        for cp in copies(0, 0):
            cp.start()

    hn = _rms(h_ref[0], g_ref[...]) * (1.0 + mod_ref[0, base + 1:base + 2, :]) \
        + mod_ref[0, base:base + 1, :]
    hn_ref[...] = hn.astype(BF16)
    o_ref[0] = h_ref[0]
    half_gate = 0.5 * mod_ref[0, base + 2:base + 3, :]

    for j in range(nf):
        slot = (first_tile + j) % FFN_WEIGHT_SLOTS
        nxt = (first_tile + j + 1) % FFN_WEIGHT_SLOTS
        if j + 1 < nf:
            for cp in copies(j + 1, nxt):
                cp.start()
        else:
            @pl.when(step + 1 < nsteps)
            def _():
                for cp in copies(0, nxt):
                    cp.start()
        for cp in copies(j, slot):
            cp.wait()
        hn = hn_ref[...]
        gate = jnp.dot(hn, wg_buf[slot], preferred_element_type=F32)
        up = jnp.dot(hn, wu_buf[slot], preferred_element_type=F32)
        act = (gate * _sigmoid(gate) * up).astype(BF16)
        o_ref[0] += half_gate * jnp.dot(act, wd_buf[slot], preferred_element_type=F32)

    if tail is not None:
        normed = _rms(o_ref[0], tail_ref[...])
        if tail == "final":
            o_ref[0] = normed
        else:
            hn_o_ref[0] = (normed * (1.0 + mod_ref[0, 4:5, :])
                           + mod_ref[0, 3:4, :]).astype(BF16)


def _ffn(h, mod, g, wg, wu, wd, tail_g, *, sel, base, tail, tm, tf):
    nb, l, d = h.shape
    assert wg.shape[-1] % tf == 0 and l % tm == 0
    per_batch = mod.shape[0] > 1
    mod_map = (lambda b, i: (b, 0, 0)) if per_batch else (lambda b, i: (0, 0, 0))
    row = pl.BlockSpec((1, tm, d), lambda b, i: (b, i, 0))
    hbm = pl.BlockSpec(memory_space=pl.ANY)
    out_specs, out_shape = row, jax.ShapeDtypeStruct((nb, l, d), F32)
    if tail == "mixer_norm":
        out_specs, out_shape = [row, row], [out_shape, jax.ShapeDtypeStruct((nb, l, d), BF16)]
    slots = FFN_WEIGHT_SLOTS
    return pl.pallas_call(
        functools.partial(_ffn_kernel, base=base, tail=tail, sel=sel, tf=tf),
        grid=(nb, l // tm),
        in_specs=[row,
                  pl.BlockSpec((1, N_MOD, d), mod_map),
                  pl.BlockSpec((1, d), lambda b, i: (0, 0)),
                  hbm, hbm, hbm,
                  pl.BlockSpec((1, d), lambda b, i: (0, 0))],
        out_specs=out_specs,
        out_shape=out_shape,
        scratch_shapes=[pltpu.VMEM((tm, d), BF16),
                        pltpu.VMEM((slots, d, tf), BF16),
                        pltpu.VMEM((slots, d, tf), BF16),
                        pltpu.VMEM((slots, tf, d), BF16),
                        pltpu.SemaphoreType.DMA((slots, 3))],
        compiler_params=_params("arbitrary", "arbitrary"),
        name="ffn",
    )(h, mod, g.reshape(1, d), wg, wu, wd, tail_g.reshape(1, d))


def _proj_kernel(hn_ref, w_ref, o_ref):
    o_ref[0] = jnp.dot(hn_ref[0], w_ref[...], preferred_element_type=F32)


def _proj(hn, w, *, tm, tn):
    nb, l, d = hn.shape
    n = w.shape[1]
    return pl.pallas_call(
        _proj_kernel,
        grid=(nb, l // tm, n // tn),
        in_specs=[pl.BlockSpec((1, tm, d), lambda b, i, j: (b, i, 0)),
                  pl.BlockSpec((d, tn), lambda b, i, j: (0, j))],
        out_specs=pl.BlockSpec((1, tm, tn), lambda b, i, j: (b, i, j)),
        out_shape=jax.ShapeDtypeStruct((nb, l, n), F32),
        compiler_params=_params("arbitrary", "arbitrary", "arbitrary"),
        name="proj",
    )(hn, w)


POOL_HALO = 8


def _pool_kernel(prev_ref, x_ref, next_ref, pw_ref, ps_ref, o_ref, scr, *, t_rows, seq):
    i = pl.program_id(1)
    last = pl.num_programs(1) - 1
    hal = POOL_HALO
    scr[0:hal] = jnp.where(i > 0, prev_ref[0], 0.0)
    scr[hal:hal + t_rows] = x_ref[0]
    scr[hal + t_rows:2 * hal + t_rows] = jnp.where(i < last, next_ref[0], 0.0)
    t = i * t_rows + lax.broadcasted_iota(jnp.int32, (t_rows, LANES), 0)
    for gi, w in enumerate(POOL_WINDOWS):
        c0 = gi * LANES
        s = scr[hal - w // 2:hal - w // 2 + t_rows, c0:c0 + LANES]
        for dlt in range(-(w // 2) + 1, w // 2):
            s = s + scr[hal + dlt:hal + dlt + t_rows, c0:c0 + LANES]
        lo = jnp.clip(t - w // 2, 0, seq)
        hi = jnp.clip(t - w // 2 + w, 0, seq)
        cnt = (hi - lo).astype(F32)
        p = s / cnt - scr[hal:hal + t_rows, c0:c0 + LANES]
        y = _mm(p, pw_ref[gi]) * ps_ref[:, c0:c0 + LANES]
        o_ref[0, :, c0:c0 + LANES] = y.astype(o_ref.dtype)


def _pool(z, pool_w, pool_scale, *, col_block, t_rows):
    nb, l, _ = z.shape
    c = pool_scale.shape[-1]
    hal = POOL_HALO
    r = t_rows // hal
    nh = l // hal
    cb = col_block
    return pl.pallas_call(
        functools.partial(_pool_kernel, t_rows=t_rows, seq=l),
        grid=(nb, l // t_rows),
        in_specs=[pl.BlockSpec((1, hal, c), lambda b, i: (b, jnp.maximum(i * r - 1, 0), cb)),
                  pl.BlockSpec((1, t_rows, c), lambda b, i: (b, i, cb)),
                  pl.BlockSpec((1, hal, c), lambda b, i: (b, jnp.minimum((i + 1) * r, nh - 1), cb)),
                  pl.BlockSpec(pool_w.shape, lambda b, i: (0, 0, 0)),
                  pl.BlockSpec((1, c), lambda b, i: (0, 0))],
        out_specs=pl.BlockSpec((1, t_rows, c), lambda b, i: (b, i, 0)),
        out_shape=jax.ShapeDtypeStruct((nb, l, c), BF16),
        scratch_shapes=[pltpu.VMEM((t_rows + 2 * hal, c), F32)],
        compiler_params=_params("arbitrary", "arbitrary"),
        name="pool",
    )(z, z, z, pool_w.astype(BF16), pool_scale.reshape(1, c))


def _gelu(x):
    return x * (0.5 * (1.0 + jnp.tanh(math.sqrt(2.0 / math.pi) * (x + 0.044715 * (x * x * x)))))


def _gmlp_kernel(z_ref, lng_ref, lnb_ref, ws_ref, bsb_ref, o_ref, *, t_rows):
    width = o_ref.shape[2]
    groups = ws_ref.shape[0]
    gd = width // groups
    for ci in range(t_rows // GMLP_CHUNK):
        rows = slice(ci * GMLP_CHUNK, (ci + 1) * GMLP_CHUNK)
        u = _gelu(z_ref[0, rows, 0:width])
        v = _gelu(z_ref[0, rows, width:2 * width])
        mu = jnp.mean(v, axis=-1, keepdims=True)
        var = jnp.mean(jnp.square(v - mu), axis=-1, keepdims=True)
        vn = ((v - mu) * lax.rsqrt(var + LN_EPS)) * lng_ref[...] + lnb_ref[...]
        for g in range(groups):
            cols = slice(g * gd, (g + 1) * gd)
            s = _mm(ws_ref[g], vn[:, cols]) + bsb_ref[g]
            o_ref[0, rows, cols] = (u[:, cols] * s).astype(o_ref.dtype)


def _gmlp(zuv, ln_g, ln_b, ws, bs, *, t_rows):
    nb, l, _ = zuv.shape
    c = ln_g.shape[-1]
    c2 = 2 * c
    groups = ws.shape[0]
    bsb = jnp.broadcast_to(bs[:, :, None], (groups, GMLP_CHUNK, c // groups))
    return pl.pallas_call(
        functools.partial(_gmlp_kernel, t_rows=t_rows),
        grid=(nb, l // t_rows),
        in_specs=[pl.BlockSpec((1, t_rows, c2), lambda b, i: (b, i, 0)),
                  pl.BlockSpec((1, c), lambda b, i: (0, 0)),
                  pl.BlockSpec((1, c), lambda b, i: (0, 0)),
                  pl.BlockSpec(ws.shape, lambda b, i: (0, 0, 0)),
                  pl.BlockSpec(bsb.shape, lambda b, i: (0, 0, 0))],
        out_specs=pl.BlockSpec((1, t_rows, c), lambda b, i: (b, i, 0)),
        out_shape=jax.ShapeDtypeStruct((nb, l, c), BF16),
        compiler_params=_params("arbitrary", "arbitrary"),
        name="gmlp",
    )(zuv, ln_g.reshape(1, c), ln_b.reshape(1, c), ws.astype(BF16), bsb)


def _segsum(x, bd):
    hi = x.astype(BF16)
    lo = (x - hi.astype(F32)).astype(BF16)
    out = []
    for p in range(x.shape[-1] // LANES):
        cols = slice(p * LANES, (p + 1) * LANES)
        out.append(jnp.dot(hi[:, cols], bd, preferred_element_type=F32)
                   + jnp.dot(lo[:, cols], bd, preferred_element_type=F32))
    return jnp.concatenate(out, axis=-1)


def _rwkv_prep_kernel(prev_ref, x_ref, next_ref, mu_ref, w0_ref, wup_ref, a0_ref, aup_ref,
                      gup_ref, kk_ref, ka_ref, rk_ref, bd_ref,
                      lw0_o, lw1_o, kd0_o, kd1_o, b0_o, b1_o, kk_o, v_o, r_o, g_o, bv_o,
                      scr, *, t_rows, grid_mode, width):
    i = pl.program_id(1)
    last = pl.num_programs(1) - 1
    hal = GRID_W
    scr[0:hal] = jnp.where(i > 0, prev_ref[0], 0.0)
    scr[hal:hal + t_rows] = x_ref[0]
    scr[hal + t_rows:2 * hal + t_rows] = jnp.where(i < last, next_ref[0], 0.0)
    cols = scr.shape[1]
    if grid_mode:
        q = cols // 4
        bounds = (0, q, 2 * q, 3 * q, cols)
        offs = (-1, 1, -GRID_W, GRID_W)
    else:
        bounds = (0, cols // 2, cols)
        offs = (-1, 1)

    def zs_cols(c0, c1):
        n = c1 - c0
        x = scr[hal:hal + t_rows, c0:c1]
        ch = c0 + lax.broadcasted_iota(jnp.int32, (t_rows, n), 1)
        col = lax.broadcasted_iota(jnp.int32, (t_rows, n), 0) & (GRID_W - 1)
        shifted = None
        for qi, off in enumerate(offs):
            lo, hi = bounds[qi], bounds[qi + 1]
            if hi <= c0 or lo >= c1:
                continue
            src = scr[hal + off:hal + off + t_rows, c0:c1]
            if grid_mode and off == -1:
                src = jnp.where(col == 0, 0.0, src)
            if grid_mode and off == 1:
                src = jnp.where(col == GRID_W - 1, 0.0, src)
            shifted = src if shifted is None else jnp.where(ch >= lo, src, shifted)
        return x + (shifted - x) * mu_ref[:, c0:c1]

    w = width
    r = zs_cols(0, w)
    k = zs_cols(w, 2 * w)
    v = zs_cols(2 * w, 3 * w)
    rest = zs_cols(3 * w, cols)
    wd = jnp.tanh(rest[:, 0:LANES])
    ad = rest[:, LANES:2 * LANES]
    gd = _sigmoid(rest[:, 2 * LANES:3 * LANES])

    bd = bd_ref[...]
    kk0 = k * kk_ref[...]
    ss = _segsum(kk0 * kk0, bd)
    kk = kk0 / jnp.maximum(jnp.sqrt(ss), 1e-12)
    npair = w // LANES

    def put(o_ref, val):
        for p in range(npair):
            o_ref[0, p] = val[:, p * LANES:(p + 1) * LANES].astype(o_ref.dtype)

    put(kk_o, kk)
    put(v_o, v)
    put(r_o, r)
    kd_sum = None
    for d, (lw_o, kd_o, b_o) in enumerate(((lw0_o, kd0_o, b0_o), (lw1_o, kd1_o, b1_o))):
        w_pre = w0_ref[d] + _mm(wd, wup_ref[d])
        put(lw_o, -math.exp(-0.5) * _sigmoid(w_pre))
        a = _sigmoid(a0_ref[d] + _mm(ad, aup_ref[d]))
        kd = k * (1.0 + (a - 1.0) * ka_ref[...])
        put(kd_o, kd)
        put(b_o, kk * a)
        kd_sum = kd if kd_sum is None else kd_sum + kd
    g_o[0] = _mm(gd, gup_ref[...]).astype(g_o.dtype)
    bv_o[0] = (_segsum(r * kd_sum * rk_ref[...], bd) * v).astype(bv_o.dtype)


def _head_blockdiag(width):
    idx = jnp.arange(width) // HEAD_DIM
    return (idx[:, None] == idx[None, :]).astype(BF16)


def _rwkv_prep(zb, lp, *, t_rows, grid_mode):
    nb, l, cols = zb.shape
    w = lp['w0'].shape[-1]
    npair = w // LANES
    hal = GRID_W
    r = t_rows // hal
    nh = l // hal
    lora = lp['w_up'].shape[1]

    def pad_dir(up):
        z = jnp.zeros_like(up[0])
        return jnp.stack([jnp.concatenate([up[0], z], 0), jnp.concatenate([z, up[1]], 0)], 0)

    const2 = lambda b, i: (0, 0)
    const3 = lambda b, i: (0, 0, 0)
    pair_spec = pl.BlockSpec((1, npair, t_rows, LANES), lambda b, i: (b, 0, i, 0))
    wide_spec = pl.BlockSpec((1, t_rows, w), lambda b, i: (b, i, 0))
    pair_shape = lambda dt: jax.ShapeDtypeStruct((nb, npair, l, LANES), dt)
    wide_shape = jax.ShapeDtypeStruct((nb, l, w), BF16)
    return pl.pallas_call(
        functools.partial(_rwkv_prep_kernel, t_rows=t_rows, grid_mode=grid_mode, width=w),
        grid=(nb, l // t_rows),
        in_specs=[pl.BlockSpec((1, hal, cols), lambda b, i: (b, jnp.maximum(i * r - 1, 0), 0)),
                  pl.BlockSpec((1, t_rows, cols), lambda b, i: (b, i, 0)),
                  pl.BlockSpec((1, hal, cols), lambda b, i: (b, jnp.minimum((i + 1) * r, nh - 1), 0)),
                  pl.BlockSpec((1, cols), const2),
                  pl.BlockSpec((2, 1, w), const3),
                  pl.BlockSpec((2, 2 * lora, w), const3),
                  pl.BlockSpec((2, 1, w), const3),
                  pl.BlockSpec((2, 2 * lora, w), const3),
                  pl.BlockSpec(lp['g_up'].shape, const2),
                  pl.BlockSpec((1, w), const2),
                  pl.BlockSpec((1, w), const2),
                  pl.BlockSpec((1, w), const2),
                  pl.BlockSpec((LANES, LANES), const2)],
        out_specs=[pair_spec] * 9 + [wide_spec] * 2,
        out_shape=[pair_shape(F32)] * 2 + [pair_shape(BF16)] * 7 + [wide_shape] * 2,
        scratch_shapes=[pltpu.VMEM((t_rows + 2 * hal, cols), F32)],
        compiler_params=_params("arbitrary", "arbitrary"),
        name="rwkv_prep",
    )(zb, zb, zb, lp['mu'].reshape(1, cols), lp['w0'].reshape(2, 1, w),
      pad_dir(lp['w_up']).astype(BF16), lp['a0'].reshape(2, 1, w),
      pad_dir(lp['a_up']).astype(BF16), lp['g_up'].astype(BF16), lp['k_k'].reshape(1, w),
      lp['k_a'].reshape(1, w), lp['r_k'].reshape(1, w), _head_blockdiag(LANES))


def _stack2(x, m0):
    xs = x.astype(BF16)
    zero = jnp.zeros_like(xs)
    return jnp.concatenate([jnp.where(m0, xs, zero), jnp.where(m0, zero, xs)], axis=0)


_NN = (((1,), (0,)), ((), ()))
_NT = (((1,), (1,)), ((), ()))
_TN = (((0,), (0,)), ((), ()))

SCAN_LOCKSTEP_PAIRS = 8


def _dg(a, b, dims):
    return lax.dot_general(a.astype(BF16), b.astype(BF16), dims, preferred_element_type=F32)


def _scan_order_sums(tri, lw):
    tri16 = tri.astype(BF16)
    hi = lw.astype(BF16)
    lo = (lw - hi.astype(F32)).astype(BF16)
    return (jnp.dot(tri16, hi, preferred_element_type=F32)
            + jnp.dot(tri16, lo, preferred_element_type=F32))


def _chunk_local(refs, rows, *, rev, tri, masks, m0):
    lw, kd, b, kk, v, r = (x[rows].astype(F32) for x in refs)
    strict, incl, _ = masks
    c = lw.shape[0]
    cum = _scan_order_sums(tri, lw)
    yield
    cum_prev = cum - lw
    end = 0 if rev else c - 1
    tot = cum[end:end + 1, :]
    mid = cum[c // 2:c // 2 + 1, :]
    e_inv = jnp.exp(mid - cum)
    nkk = -kk
    left = jnp.concatenate([_stack2(nkk * jnp.exp(cum_prev - mid), m0),
                            _stack2(r * jnp.exp(cum - mid), m0)], axis=0)
    right = jnp.concatenate([_stack2(b * e_inv, m0), _stack2(kd * e_inv, m0)], axis=0)
    gram = _dg(left, right, _NT)
    yield
    c2 = 2 * c
    g16 = gram.astype(BF16)
    zero16 = jnp.zeros((c2, c2), BF16)
    vsw = pltpu.roll(v, HEAD_DIM, 1).astype(BF16)
    zv = jnp.zeros_like(vsw)
    e_end = jnp.exp(tot - cum)
    return dict(
        a_ab=jnp.where(strict, g16[0:c2, 0:c2], zero16),
        a_ak=jnp.where(strict, g16[0:c2, c2:2 * c2], zero16),
        a_rb=jnp.where(incl, g16[c2:2 * c2, 0:c2], zero16),
        a_rk=jnp.where(incl, g16[c2:2 * c2, c2:2 * c2], zero16),
        vbd=jnp.concatenate([jnp.where(m0, zv, vsw), jnp.where(m0, vsw, zv)], axis=0),
        a0=_stack2(nkk * jnp.exp(cum_prev), m0), r0=_stack2(r * jnp.exp(cum), m0),
        bh=_stack2(b * e_end, m0), kh=_stack2(kd * e_end, m0), decay=jnp.exp(tot))


def _chunk_solve(loc, s, diag):
    vbd = loc["vbd"]
    c2 = vbd.shape[0]
    x = _dg(loc["a0"], s, _NT) + _dg(loc["a_ak"], vbd, _NN)
    ybd = _dg(loc["r0"], s, _NT) + _dg(loc["a_rk"], vbd, _NN)
    s_new = s * loc["decay"] + _dg(vbd, loc["kh"], _TN)
    steps = max(1, (c2 // 2 - 1).bit_length())
    pw = loc["a_ab"]
    rhs = jnp.where(diag, pw, x.astype(BF16))
    for it in range(steps):
        yield
        both = _dg(pw, rhs, _NN)
        x = x + both
        if it < steps - 1:
            both16 = both.astype(BF16)
            pw = jnp.where(diag, both16, jnp.zeros_like(both16))
            rhs = jnp.where(diag, both16, x.astype(BF16))
    yield
    u = jnp.where(diag, 0.0, x)
    ybd = ybd + _dg(loc["a_rb"], u, _NN)
    y = pltpu.roll(ybd[0:c2 // 2] + ybd[c2 // 2:c2], HEAD_DIM, 1)
    return y, s_new + _dg(u, loc["bh"], _TN)


SCAN_CHUNKS_PER_STEP = 4
SCAN_OVERLAP_STAGES = (4, 5, 6)


def _chain(refs, s_ref, order, **kw):
    c = SCAN_CHUNK
    rows = [slice(k * c, (k + 1) * c) for k in order]
    diag = kw["masks"][2]
    s = s_ref[...]
    ys = []
    loc = yield from _chunk_local(refs, rows[0], **kw)
    for k in range(len(rows)):
        nxt_local = _chunk_local(refs, rows[k + 1], **kw) if k + 1 < len(rows) else None
        nxt_loc = None
        solve, stage = _chunk_solve(loc, s, diag), 0
        while True:
            try:
                next(solve)
            except StopIteration as stop:
                y, s = stop.value
                break
            stage += 1
            if nxt_local is not None and nxt_loc is None and stage in SCAN_OVERLAP_STAGES:
                try:
                    next(nxt_local)
                except StopIteration as stop:
                    nxt_loc = stop.value
            yield
        ys.append((rows[k], y))
        if nxt_local is not None:
            while nxt_loc is None:
                try:
                    next(nxt_local)
                except StopIteration as stop:
                    nxt_loc = stop.value
            loc = nxt_loc
            yield
    return ys, s


def _lockstep(gens):
    results = [None] * len(gens)
    live = list(range(len(gens)))
    while live:
        still = []
        for i in live:
            try:
                next(gens[i])
                still.append(i)
            except StopIteration as stop:
                results[i] = stop.value
        live = still
    return results


def _rwkv_scan_kernel(lw0, kd0, b0, kkf, vf, rf, lw1, kd1, b1, kkb, vb, rb, s0_ref,
                      yf_o, yb_o, sfin_o, st):
    ci = pl.program_id(1)
    c = SCAN_CHUNK
    npair = lw0.shape[1]

    @pl.when(ci == 0)
    def _():
        st[...] = s0_ref[:, 0]

    row = lax.broadcasted_iota(jnp.int32, (c, c), 0)
    colm = lax.broadcasted_iota(jnp.int32, (c, c), 1)
    tri_f = (colm <= row).astype(F32)
    tri_b = (colm >= row).astype(F32)
    rows2 = lax.broadcasted_iota(jnp.int32, (2 * c, 2 * c), 0)
    cols2 = lax.broadcasted_iota(jnp.int32, (2 * c, 2 * c), 1)
    r2 = rows2 & (c - 1)
    c2 = cols2 & (c - 1)
    diag = (rows2 < c) == (cols2 < c)
    masks_f = (c2 < r2, c2 <= r2, diag)
    masks_b = (c2 > r2, c2 >= r2, diag)
    m0 =lax.broadcasted_iota(jnp.int32, (1, LANES), 1) < HEAD_DIM

    fwd_refs = (lw0, kd0, b0, kkf, vf, rf)
    bwd_refs = (lw1, kd1, b1, kkb, vb, rb)
    order = tuple(range(lw0.shape[2] // c))
    for p0 in range(0, npair, SCAN_LOCKSTEP_PAIRS):
        gens, outs = [], []
        for p in range(p0, min(p0 + SCAN_LOCKSTEP_PAIRS, npair)):
            gens.append(_chain([x.at[0, p] for x in fwd_refs], st.at[0, p], order, rev=False,
                               tri=tri_f, masks=masks_f, m0=m0))
            outs.append((yf_o, 0, p))
            gens.append(_chain([x.at[0, p] for x in bwd_refs], st.at[1, p], order[::-1], rev=True,
                               tri=tri_b, masks=masks_b, m0=m0))
            outs.append((yb_o, 1, p))
        for (y_o, d, p), (ys, s_new) in zip(outs, _lockstep(gens)):
            for rows, y in ys:
                y_o[0, p, rows] = y.astype(y_o.dtype)
            st[d, p] = s_new

    @pl.when(ci == pl.num_programs(1) - 1)
    def _():
        sfin_o[:, 0] = st[...]


def _rwkv_scan(prep, s0):
    lw0, lw1, kd0, kd1, b0, b1, kk, v, r = prep
    nb, npair, l, _ = lw0.shape
    c = SCAN_CHUNK
    rows = min(SCAN_CHUNKS_PER_STEP * c, l)
    assert c == HEAD_DIM and 2 * HEAD_DIM == LANES and l % rows == 0 and rows % c == 0
    n = l // rows
    fwd = pl.BlockSpec((1, npair, rows, LANES), lambda b, i: (b, 0, i, 0))
    bwd = pl.BlockSpec((1, npair, rows, LANES), lambda b, i: (b, 0, n - 1 - i, 0))
    st_spec = pl.BlockSpec((2, 1, npair, LANES, LANES), lambda b, i: (0, b, 0, 0, 0))
    y_shape = jax.ShapeDtypeStruct((nb, npair, l, LANES), BF16)
    return pl.pallas_call(
        _rwkv_scan_kernel,
        grid=(nb, n),
        in_specs=[fwd] * 6 + [bwd] * 6 + [st_spec],
        out_specs=[fwd, bwd, st_spec],
        out_shape=[y_shape, y_shape, jax.ShapeDtypeStruct(s0.shape, F32)],
        scratch_shapes=[pltpu.VMEM((2, npair, LANES, LANES), F32)],
        compiler_params=_params("arbitrary", "arbitrary"),
        name="rwkv_scan",
    )(lw0, kd0, b0, kk, v, r, lw1, kd1, b1, kk, v, r, s0)


def _rwkv_post_kernel(yf_ref, yb_ref, bv_ref, g_ref, lng_ref, lnb_ref, bd_ref, o_ref):
    npair = yf_ref.shape[1]
    y = jnp.concatenate([yf_ref[0, p].astype(F32) + yb_ref[0, p].astype(F32)
                         for p in range(npair)], axis=-1)
    bd = bd_ref[...]
    inv = 1.0 / HEAD_DIM
    m = _segsum(y, bd) * inv
    dlt = y - m
    var = _segsum(dlt * dlt, bd) * inv
    yn = dlt * lax.rsqrt(var + RWKV_GN_EPS) * lng_ref[...] + lnb_ref[...]
    o_ref[0] = ((yn + bv_ref[0].astype(F32)) * g_ref[0].astype(F32)).astype(o_ref.dtype)


def _rwkv_post(yf, yb, bv, g, ln_g, ln_b, *, t_rows):
    nb, npair, l, _ = yf.shape
    w = npair * LANES
    pair_spec = pl.BlockSpec((1, npair, t_rows, LANES), lambda b, i: (b, 0, i, 0))
    wide_spec = pl.BlockSpec((1, t_rows, w), lambda b, i: (b, i, 0))
    const2 = lambda b, i: (0, 0)
    return pl.pallas_call(
        _rwkv_post_kernel,
        grid=(nb, l // t_rows),
        in_specs=[pair_spec, pair_spec, wide_spec, wide_spec,
                  pl.BlockSpec((1, w), const2), pl.BlockSpec((1, w), const2),
                  pl.BlockSpec((LANES, LANES), const2)],
        out_specs=wide_spec,
        out_shape=jax.ShapeDtypeStruct((nb, l, w), BF16),
        compiler_params=_params("arbitrary", "arbitrary"),
        name="rwkv_post",
    )(yf, yb, bv, g, ln_g.reshape(1, w), ln_b.reshape(1, w), _head_blockdiag(LANES))


def _merge_kernel(h_ref, hn_ref, mod_ref, yp_ref, yr_ref, yg_ref, wzg_ref, pp_ref, pr_ref, pg_ref,
                  wo_ref, o_ref):
    @pl.when(pl.program_id(2) == 0)
    def _():
        o_ref[0] = h_ref[0]

    hn = hn_ref[0]
    merged = None
    for br, (y_ref, p_ref) in enumerate(((yp_ref, pp_ref), (yr_ref, pr_ref), (yg_ref, pg_ref))):
        gate = _sigmoid(jnp.dot(hn, wzg_ref[br], preferred_element_type=F32))
        term = gate * jnp.dot(y_ref[0], p_ref[...], preferred_element_type=F32)
        merged = term if merged is None else merged + term
    o_ref[0] += mod_ref[0, 5:6, :] * jnp.dot(merged.astype(BF16), wo_ref[...],
                                             preferred_element_type=F32)


def _merge(h, hn, mod, yp, yr, yg, wzg, pp, pr, pg, wo, *, li, tm, tn):
    nb, l, d = h.shape
    per_batch = mod.shape[0] > 1
    mod_map = (lambda b, i, n: (b, 0, 0)) if per_batch else (lambda b, i, n: (0, 0, 0))
    row = lambda width: pl.BlockSpec((1, tm, width), lambda b, i, n: (b, i, 0))
    col = lambda a: pl.BlockSpec((None, a.shape[1], tn), lambda b, i, n: (li, 0, n))
    return pl.pallas_call(
        _merge_kernel,
        grid=(nb, l // tm, d // tn),
        in_specs=[row(d), row(d), pl.BlockSpec((1, N_MOD, d), mod_map),
                  row(yp.shape[2]), row(yr.shape[2]), row(yg.shape[2]),
                  pl.BlockSpec((wzg.shape[0], d, tn), lambda b, i, n: (0, 0, n)),
                  col(pp), col(pr), col(pg),
                  pl.BlockSpec((None, tn, d), lambda b, i, n: (li, n, 0))],
        out_specs=row(d),
        out_shape=jax.ShapeDtypeStruct((nb, l, d), F32),
        compiler_params=_params("arbitrary", "arbitrary", "arbitrary"),
        name="merge",
    )(h, hn, mod, yp, yr, yg, wzg, pp, pr, pg, wo)


def _pick(n, pref):
    t = min(pref, n)
    while t > LANES and (n % t or t % LANES):
        t -= LANES
    return t if n % t == 0 else n


def kernel(x, c, ctx, c_ctx, ada_w, ada_b, norm_g, ffn_w_gate, ffn_w_up, ffn_w_down, w_in,
           pool_w, pool_scale, rwkv_mu, rwkv_w0, rwkv_w_up, rwkv_a0, rwkv_a_up, rwkv_g_up,
           rwkv_k_k, rwkv_k_a, rwkv_r_k, rwkv_ln_g, rwkv_ln_b, gmlp_ln_g, gmlp_ln_b, gmlp_ws,
           gmlp_bs, proj_pool, proj_rwkv, proj_gmlp, w_out, final_norm):
    depth = ada_w.shape[0]
    nb, l, d = x.shape
    lc = ctx.shape[1]
    width = rwkv_w0.shape[-1]
    pool_c = pool_scale.shape[-1]
    gmlp_c = gmlp_ln_g.shape[-1]
    rwkv_cols = rwkv_mu.shape[-1]
    off_rwkv = pool_c
    off_gmlp = off_rwkv + rwkv_cols
    off_gate = off_gmlp + 2 * gmlp_c
    npair = width // LANES

    cond8 = jnp.zeros((8, d), F32).at[:nb].set(c).at[nb].set(c_ctx)
    mod_all = _adaln(cond8, ada_w, ada_b).reshape(depth, 8, N_MOD, d)

    zero_state = jnp.zeros((2, nb, npair, LANES, LANES), F32)
    wg, wu, wd = (a.astype(BF16) for a in (ffn_w_gate, ffn_w_up, ffn_w_down))
    pp, pr, pg, wo = (a.astype(BF16) for a in (proj_pool, proj_rwkv, proj_gmlp, w_out))
    win_all = w_in.astype(BF16)
    tf = _pick(wg.shape[-1], 512)
    assert (2 * gmlp_c) % pool_c == 0
    hc = ctx.reshape(1, nb * lc, d)
    for li in range(depth):
        last = li == depth - 1
        mod_x = mod_all[li, :nb]
        mod_c = mod_all[li, nb:nb + 1]
        win = win_all[li]
        w_zb = win[:, off_rwkv:off_gmlp]
        w_uva = jnp.concatenate([win[:, off_gmlp:off_gate], win[:, :off_rwkv]], axis=1)
        w_zg = win[:, off_gate:].reshape(d, -1, d).transpose(1, 0, 2)
        lp = {'mu': rwkv_mu[li], 'w0': rwkv_w0[li], 'w_up': rwkv_w_up[li], 'a0': rwkv_a0[li],
              'a_up': rwkv_a_up[li], 'g_up': rwkv_g_up[li], 'k_k': rwkv_k_k[li],
              'k_a': rwkv_k_a[li], 'r_k': rwkv_r_k[li]}

        def mixer(h, hn, mod, nseq, seq, grid_mode, s0, need_out):
            flat = h.shape[:2]
            tm = _pick(flat[1], 1024)
            zb = _proj(hn, w_zb, tm=tm, tn=_pick(rwkv_cols, 1152))
            prep = _rwkv_prep(zb.reshape(nseq, seq, -1), lp, t_rows=_pick(seq, 256),
                              grid_mode=grid_mode)
            yf, yb, s_fin = _rwkv_scan(prep[:9], s0)
            if not need_out:
                return None, s_fin
            y_rwkv = _rwkv_post(yf, yb, prep[10], prep[9], rwkv_ln_g[li], rwkv_ln_b[li],
                                t_rows=_pick(seq, 256))
            z_uva = _proj(hn, w_uva, tm=tm,
                          tn=_pick(w_uva.shape[1], 768)).reshape(nseq, seq, -1)
            y_pool = _pool(z_uva, pool_w[li], pool_scale[li], col_block=2 * gmlp_c // pool_c,
                           t_rows=_pick(seq, 512))
            y_gmlp = _gmlp(z_uva, gmlp_ln_g[li], gmlp_ln_b[li], gmlp_ws[li], gmlp_bs[li],
                           t_rows=_pick(seq, 512))
            ys = [y.reshape(flat + (-1,)) for y in (y_pool, y_rwkv, y_gmlp)]
            out = _merge(h, hn, mod, *ys, w_zg, pp, pr, pg, wo,
                         li=li, tm=_pick(flat[1], 512), tn=_pick(d, 512))
            return out, s_fin

        def ffn(h, mod, k, tail, tail_g):
            return _ffn(h, mod, norm_g[li, 2 * k], wg, wu, wd, tail_g, sel=(li, k),
                        base=6 * k, tail=tail, tm=_pick(h.shape[1], 512), tf=tf)

        hc, hnc = ffn(hc, mod_c, 0, "mixer_norm", norm_g[li, 1])
        hc_new, s_ctx = mixer(hc, hnc, mod_c, nb, lc, False, zero_state, not last)
        if not last:
            hc = ffn(hc_new, mod_c, 1, None, final_norm)

        x, hn = ffn(x, mod_x, 0, "mixer_norm", norm_g[li, 1])
        x, _ = mixer(x, hn, mod_x, nb, l, True, s_ctx, True)
        x = ffn(x, mod_x, 1, "final" if last else None, final_norm)
    return x
```
